```python
import numpy as np
import jax
import jax.numpy as jnp
from jax import lax

D_MODEL = 2048
BATCH = 4
SEQ = 2048
DEPTH = 4

EPS = 1e-6
NEG = -1e30
D_FF = 5632
Q_BLOCK = 128

A_GROUPS = ((128, 1), (512, 4), (2048, 16))
A_HEADS_PER_GROUP = 4
A_HEADS = A_HEADS_PER_GROUP * len(A_GROUPS)
A_HEAD_DIM = 64
A_WIDTH = A_HEADS * A_HEAD_DIM
A_OUT = A_HEADS_PER_GROUP * A_HEAD_DIM

B_WINDOWS = (2, 4, 8, 16)
B_GROUP_DIM = 128
B_WIDTH = B_GROUP_DIM * len(B_WINDOWS)

C_KV_GROUPS = 2
C_HEADS_PER_GROUP = 3
C_HEADS = C_KV_GROUPS * C_HEADS_PER_GROUP
C_HEAD_DIM = 128
C_WIDTH = C_HEADS * C_HEAD_DIM
C_KV_WIDTH = C_KV_GROUPS * C_HEAD_DIM
CMP_LEN = 32
CMP_STRIDE = 16
SEL_BLOCK = 64
SEL_TOPN = 8
FORCE_BONUS = 100.0
WIN = 512

N_BRANCH = 3
COL_SIZES = (A_WIDTH, A_WIDTH, A_WIDTH, B_WIDTH, C_WIDTH, C_KV_WIDTH, C_KV_WIDTH, C_KV_WIDTH, C_KV_WIDTH, C_KV_WIDTH, C_KV_WIDTH, 3 * C_HEADS, N_BRANCH * D_MODEL)
IN_COLS = sum(COL_SIZES)

kernel_name = 'hybrid_dilated_pool_nsa_trunk'


def _rmsnorm(x, g):
    xf = x.astype(jnp.float32)
    y = xf * lax.rsqrt(jnp.mean(xf * xf, axis=-1, keepdims=True) + EPS)
    return y.astype(x.dtype) * g


def _swiglu(x, w_i, w_o):
    a, b = jnp.split(x @ w_i, 2, axis=-1)
    return (jax.nn.silu(a) * b) @ w_o


def _dilated_attention(q, k, v):
    bsz, s, _ = q.shape
    nq = s // Q_BLOCK
    shp = (bsz, s, len(A_GROUPS), A_HEADS_PER_GROUP, A_HEAD_DIM)
    qh = q.reshape(shp).transpose(2, 0, 3, 1, 4)
    kh = k.reshape(shp).transpose(2, 0, 3, 1, 4)
    vh = v.reshape(shp).transpose(2, 0, 3, 1, 4)
    scale = A_HEAD_DIM ** -0.5
    outs, lses = [], []
    for gi, (win, dil) in enumerate(A_GROUPS):
        offs = jnp.arange(win // dil + 1) * dil
        kg, vg = kh[gi], vh[gi]
        q_chunks = qh[gi].reshape(bsz, A_HEADS_PER_GROUP, nq, Q_BLOCK, A_HEAD_DIM).transpose(2, 0, 1, 3, 4)

        def block(args, kg=kg, vg=vg, offs=offs):
            q_blk, start = args
            t = start + jnp.arange(Q_BLOCK)
            idx = t[:, None] - offs[None, :]
            valid = idx >= 0
            idx = jnp.maximum(idx, 0)
            kk = jnp.take(kg, idx, axis=2)
            vv = jnp.take(vg, idx, axis=2)
            sc = jnp.einsum('bhqd,bhqkd->bhqk', q_blk, kk, preferred_element_type=jnp.float32) * scale
            sc = jnp.where(valid, sc, NEG)
            m = jnp.max(sc, axis=-1, keepdims=True)
            p = jnp.exp(sc - m)
            l = jnp.sum(p, axis=-1, keepdims=True)
            o = jnp.einsum('bhqk,bhqkd->bhqd', p, vv.astype(jnp.float32)) / l
            return o, (m + jnp.log(l))[..., 0]

        o, lse = lax.map(block, (q_chunks, jnp.arange(nq) * Q_BLOCK))
        outs.append(o.transpose(1, 2, 0, 3, 4).reshape(bsz, A_HEADS_PER_GROUP, s, A_HEAD_DIM))
        lses.append(lse.transpose(1, 2, 0, 3).reshape(bsz, A_HEADS_PER_GROUP, s))
    w = jax.nn.softmax(jnp.stack(lses), axis=0)
    o = jnp.sum(w[..., None] * jnp.stack(outs), axis=0)
    return o.transpose(0, 2, 1, 3).reshape(bsz, s, A_OUT).astype(q.dtype)


def _pool_mixer(v, w_pool, scale):
    bsz, s, _ = v.shape
    vf = v.astype(jnp.float32).reshape(bsz, s, len(B_WINDOWS), B_GROUP_DIM)
    csum = jnp.concatenate([jnp.zeros_like(vf[:, :1]), jnp.cumsum(vf, axis=1)], axis=1)
    t = jnp.arange(s)
    pooled = []
    for gi, w in enumerate(B_WINDOWS):
        lo = jnp.maximum(t + 1 - w, 0)
        cnt = (t + 1 - lo).astype(jnp.float32)
        seg = csum[:, t + 1, gi] - csum[:, lo, gi]
        pooled.append(seg / cnt[None, :, None] - vf[:, :, gi])
    z = jnp.stack(pooled, axis=2).astype(v.dtype)
    z = jnp.einsum('bsgc,gcd->bsgd', z, w_pool)
    return z.reshape(bsz, s, B_WIDTH) * scale


def _nsa(q, k_cmp, v_cmp, k_slc, v_slc, k_win, v_win, gates, pe_k, w1_k, w2_k, pe_v, w1_v, w2_v):
    bsz, s, _ = q.shape
    G, Hg, dh = C_KV_GROUPS, C_HEADS_PER_GROUP, C_HEAD_DIM
    scale = dh ** -0.5
    nq = s // Q_BLOCK
    qh = q.reshape(bsz, s, G, Hg, dh).transpose(0, 2, 3, 1, 4)

    def kvh(z):
        return z.reshape(bsz, s, G, dh).transpose(0, 2, 1, 3)

    t = jnp.arange(s)

    n_cmp = (s - CMP_LEN) // CMP_STRIDE + 1
    blk = np.arange(n_cmp)[:, None] * CMP_STRIDE + np.arange(CMP_LEN)[None, :]

    def compress(z, pe, w1, w2):
        zb = kvh(z)[:, :, blk] + pe
        return jax.nn.gelu(zb.reshape(bsz, G, n_cmp, CMP_LEN * dh) @ w1) @ w2

    kc = compress(k_cmp, pe_k, w1_k, w2_k)
    vc = compress(v_cmp, pe_v, w1_v, w2_v)
    cmp_end = np.arange(n_cmp) * CMP_STRIDE + CMP_LEN - 1
    cvalid = cmp_end[None, :] <= t[:, None]
    sc = jnp.einsum('bgnsd,bgcd->bgnsc', qh, kc, preferred_element_type=jnp.float32) * scale
    sc = jnp.where(cvalid, sc, NEG)
    p_cmp = jnp.where(cvalid, jax.nn.softmax(sc, axis=-1), 0.0)
    o_cmp = jnp.einsum('bgnsc,bgcd->bgnsd', p_cmp, vc.astype(jnp.float32))

    nb = s // SEL_BLOCK
    ci = np.arange(n_cmp)[:, None] * CMP_STRIDE
    bj = np.arange(nb)[None, :] * SEL_BLOCK
    overlap = ((ci < bj + SEL_BLOCK) & (ci + CMP_LEN > bj)).astype(np.float32)
    imp = jnp.einsum('bgnsc,cj->bgsj', p_cmp, overlap)
    jb = np.arange(nb)[None, :]
    tb = (t // SEL_BLOCK)[:, None]
    forced = (jb == 0) | (jb == tb) | (jb == tb - 1)
    score = jnp.where(jb > tb, -1.0, imp + jnp.where(forced, FORCE_BONUS, 0.0))
    k_sel = min(SEL_TOPN, nb)
    _, sel_idx = lax.top_k(score, k_sel)

    ks = kvh(k_slc).reshape(bsz, G, nb, SEL_BLOCK, dh)
    vs = kvh(v_slc).reshape(bsz, G, nb, SEL_BLOCK, dh)
    q_chunks = qh.reshape(bsz, G, Hg, nq, Q_BLOCK, dh).transpose(3, 0, 1, 2, 4, 5)
    idx_chunks = sel_idx.reshape(bsz, G, nq, Q_BLOCK, k_sel).transpose(2, 0, 1, 3, 4)
    gather = jax.vmap(jax.vmap(lambda blocks, ix: blocks[ix]))

    def sel_block(args):
        q_blk, ix, start = args
        tq = start + jnp.arange(Q_BLOCK)
        flat = ix.reshape(bsz, G, Q_BLOCK * k_sel)
        kk = gather(ks, flat).reshape(bsz, G, Q_BLOCK, k_sel * SEL_BLOCK, dh)
        vv = gather(vs, flat).reshape(bsz, G, Q_BLOCK, k_sel * SEL_BLOCK, dh)
        kpos = (ix[..., None] * SEL_BLOCK + jnp.arange(SEL_BLOCK)).reshape(bsz, G, Q_BLOCK, k_sel * SEL_BLOCK)
        valid = kpos <= tq[None, None, :, None]
        sc_s = jnp.einsum('bgnqd,bgqkd->bgnqk', q_blk, kk, preferred_element_type=jnp.float32) * scale
        sc_s = jnp.where(valid[:, :, None], sc_s, NEG)
        p = jax.nn.softmax(sc_s, axis=-1)
        return jnp.einsum('bgnqk,bgqkd->bgnqd', p, vv.astype(jnp.float32))

    o_slc = lax.map(sel_block, (q_chunks, idx_chunks, jnp.arange(nq) * Q_BLOCK))
    o_slc = o_slc.transpose(1, 2, 3, 0, 4, 5).reshape(bsz, G, Hg, s, dh)

    nband = WIN // Q_BLOCK + 1

    def band(z):
        zp = jnp.pad(kvh(z), ((0, 0), (0, 0), (WIN, 0), (0, 0))).reshape(bsz, G, nq + nband - 1, Q_BLOCK, dh)
        return jnp.concatenate([zp[:, :, j:j + nq] for j in range(nband)], axis=3)

    kw = band(k_win)
    vw = band(v_win)
    qb = qh.reshape(bsz, G, Hg, nq, Q_BLOCK, dh)
    qpos = np.arange(nq)[:, None] * Q_BLOCK + np.arange(Q_BLOCK)[None, :]
    kpos = np.arange(nq)[:, None] * Q_BLOCK - WIN + np.arange(nband * Q_BLOCK)[None, :]
    wvalid = (kpos[:, None, :] <= qpos[:, :, None]) & (kpos[:, None, :] > qpos[:, :, None] - WIN) & (kpos[:, None, :] >= 0)
    sc_w = jnp.einsum('bgnicd,bgikd->bgnick', qb, kw, preferred_element_type=jnp.float32) * scale
    sc_w = jnp.where(wvalid, sc_w, NEG)
    p_w = jax.nn.softmax(sc_w, axis=-1)
    o_win = jnp.einsum('bgnick,bgikd->bgnicd', p_w, vw.astype(jnp.float32)).reshape(bsz, G, Hg, s, dh)

    g = jax.nn.sigmoid(gates.astype(jnp.float32)).reshape(bsz, s, G, Hg, 3).transpose(0, 2, 3, 1, 4)
    o = g[..., 0:1] * o_cmp + g[..., 1:2] * o_slc + g[..., 2:3] * o_win
    return o.transpose(0, 3, 1, 2, 4).reshape(bsz, s, C_WIDTH).astype(q.dtype)


def _mixing(u, w_in, pool_w, pool_scale, pe_k, w1_k, w2_k, pe_v, w1_v, w2_v, proj_a, proj_b, proj_c, w_out):
    bsz, s, _ = u.shape
    splits = np.cumsum(COL_SIZES)[:-1].tolist()
    qa, ka, va, xb, qc, kcmp, vcmp, kslc, vslc, kwin, vwin, gc, gm = jnp.split(u @ w_in, splits, axis=-1)
    ya = _dilated_attention(qa, ka, va) @ proj_a
    yb = _pool_mixer(xb, pool_w, pool_scale) @ proj_b
    yc = _nsa(qc, kcmp, vcmp, kslc, vslc, kwin, vwin, gc, pe_k, w1_k, w2_k, pe_v, w1_v, w2_v) @ proj_c
    g = jax.nn.sigmoid(gm.reshape(bsz, s, N_BRANCH, D_MODEL))
    return (g[:, :, 0] * ya + g[:, :, 1] * yb + g[:, :, 2] * yc) @ w_out


def setup_inputs(seed: int = 0) -> dict:
    key = jax.random.key(seed)
    ks = jax.random.split(key, 24)
    f32 = jnp.float32

    def dense(k, shape):
        return jax.random.normal(k, shape, f32) * (shape[-2] ** -0.5)

    def gain(k, shape):
        return 1.0 + 0.02 * jax.random.normal(k, shape, f32)

    L = DEPTH
    cdim = CMP_LEN * C_HEAD_DIM
    return {
        'x': jax.random.normal(ks[0], (BATCH, SEQ, D_MODEL), f32),
        'ffn1_norm': gain(ks[1], (L, D_MODEL)),
        'ffn1_wi': dense(ks[2], (L, D_MODEL, 2 * D_FF)),
        'ffn1_wo': dense(ks[3], (L, D_FF, D_MODEL)),
        'mix_norm': gain(ks[4], (L, D_MODEL)),
        'w_in': dense(ks[5], (L, D_MODEL, IN_COLS)),
        'pool_w': dense(ks[6], (L, len(B_WINDOWS), B_GROUP_DIM, B_GROUP_DIM)),
        'pool_scale': gain(ks[7], (L, B_WIDTH)),
        'cmp_pe_k': 0.02 * jax.random.normal(ks[8], (L, CMP_LEN, C_HEAD_DIM), f32),
        'cmp_w1_k': dense(ks[9], (L, cdim, C_HEAD_DIM)),
        'cmp_w2_k': dense(ks[10], (L, C_HEAD_DIM, C_HEAD_DIM)),
        'cmp_pe_v': 0.02 * jax.random.normal(ks[11], (L, CMP_LEN, C_HEAD_DIM), f32),
        'cmp_w1_v': dense(ks[12], (L, cdim, C_HEAD_DIM)),
        'cmp_w2_v': dense(ks[13], (L, C_HEAD_DIM, C_HEAD_DIM)),
        'proj_a': dense(ks[14], (L, A_OUT, D_MODEL)),
        'proj_b': dense(ks[15], (L, B_WIDTH, D_MODEL)),
        'proj_c': dense(ks[16], (L, C_WIDTH, D_MODEL)),
        'w_out': dense(ks[17], (L, D_MODEL, D_MODEL)),
        'ffn2_norm': gain(ks[18], (L, D_MODEL)),
        'ffn2_wi': dense(ks[19], (L, D_MODEL, 2 * D_FF)),
        'ffn2_wo': dense(ks[20], (L, D_FF, D_MODEL)),
        'final_norm': gain(ks[21], (D_MODEL,)),
    }


def reference(x, ffn1_norm, ffn1_wi, ffn1_wo, mix_norm, w_in, pool_w, pool_scale, cmp_pe_k, cmp_w1_k, cmp_w2_k, cmp_pe_v, cmp_w1_v, cmp_w2_v, proj_a, proj_b, proj_c, w_out, ffn2_norm, ffn2_wi, ffn2_wo, final_norm):
    for l in range(DEPTH):
        x = x + 0.5 * _swiglu(_rmsnorm(x, ffn1_norm[l]), ffn1_wi[l], ffn1_wo[l])
        x = x + _mixing(_rmsnorm(x, mix_norm[l]), w_in[l], pool_w[l], pool_scale[l], cmp_pe_k[l], cmp_w1_k[l], cmp_w2_k[l], cmp_pe_v[l], cmp_w1_v[l], cmp_w2_v[l], proj_a[l], proj_b[l], proj_c[l], w_out[l])
        x = x + 0.5 * _swiglu(_rmsnorm(x, ffn2_norm[l]), ffn2_wi[l], ffn2_wo[l])
    return _rmsnorm(x, final_norm)
```

```python
import functools

import numpy as np
import jax
import jax.numpy as jnp
from jax import lax
from jax.experimental import pallas as pl
from jax.experimental.pallas import tpu as pltpu

F32 = jnp.float32
BF16 = jnp.bfloat16

EPS = 1e-6
NEG = -1e30
Q_BLOCK = 128

A_GROUPS = ((128, 1), (512, 4), (2048, 16))
A_HEADS_PER_GROUP = 4
A_HEAD_DIM = 64
A_GROUP_WIDTH = A_HEADS_PER_GROUP * A_HEAD_DIM
A_WIDTH = A_GROUP_WIDTH * len(A_GROUPS)

B_WINDOWS = (2, 4, 8, 16)
B_GROUP_DIM = 128
B_WIDTH = B_GROUP_DIM * len(B_WINDOWS)

C_KV_GROUPS = 2
C_HEADS_PER_GROUP = 3
C_HEADS = C_KV_GROUPS * C_HEADS_PER_GROUP
C_HEAD_DIM = 128
C_GROUP_WIDTH = C_HEADS_PER_GROUP * C_HEAD_DIM
C_WIDTH = C_HEADS * C_HEAD_DIM
C_KV_WIDTH = C_KV_GROUPS * C_HEAD_DIM
CMP_LEN = 32
CMP_STRIDE = 16
SEL_BLOCK = 64
SEL_TOPN = 8
FORCE_BONUS = 100.0
WIN = 512
N_GATES = 3 * C_HEADS

N_BRANCH = 3
LANES = 128

OFF_GM = 0
OFF_QA = 6144
OFF_KA = OFF_QA + A_WIDTH
OFF_VA = OFF_KA + A_WIDTH
OFF_QC = OFF_VA + A_WIDTH
OFF_XB = OFF_QC + C_WIDTH
OFF_KCMP = OFF_XB + B_WIDTH
OFF_VCMP = OFF_KCMP + C_KV_WIDTH
OFF_KSLC = OFF_VCMP + C_KV_WIDTH
OFF_VSLC = OFF_KSLC + C_KV_WIDTH
OFF_KWIN = OFF_VSLC + C_KV_WIDTH
OFF_VWIN = OFF_KWIN + C_KV_WIDTH
N_COLS = OFF_VWIN + C_KV_WIDTH

VMEM_LIMIT = 56 * 1024 * 1024


def _params(*sem):
    return pltpu.CompilerParams(dimension_semantics=sem, vmem_limit_bytes=VMEM_LIMIT)


def _dot(a, b):
    return jnp.dot(a, b, preferred_element_type=F32)


def _dot_nt(a, b):
    return lax.dot_general(a, b, (((1,), (1,)), ((), ())), preferred_element_type=F32)


def _sigmoid(x):
    return 1.0 / (1.0 + jnp.exp(-x))


def _rms_scaled(x, g):
    return x * lax.rsqrt(jnp.mean(x * x, axis=-1, keepdims=True) + EPS) * g


def _ffn_up_kernel(x_ref, g_ref, wa_ref, wb_ref, h_ref, xn_ref):
    @pl.when(pl.program_id(1) == 0)
    def _():
        xn_ref[...] = _rms_scaled(x_ref[...], g_ref[...]).astype(BF16)

    xn = xn_ref[...]
    a = _dot(xn, wa_ref[...])
    b = _dot(xn, wb_ref[...])
    h_ref[...] = (a * _sigmoid(a) * b).astype(BF16)


def _ffn_up(x, g, wi, layer, *, tm=512, tn=512):
    m, d = x.shape
    d_ff = wi.shape[-1] // 2
    nj = d_ff // tn
    return pl.pallas_call(
        _ffn_up_kernel,
        out_shape=jax.ShapeDtypeStruct((m, d_ff), BF16),
        grid=(m // tm, nj),
        in_specs=[
            pl.BlockSpec((tm, d), lambda i, j: (i, 0)),
            pl.BlockSpec((None, 1, d), lambda i, j: (layer, 0, 0)),
            pl.BlockSpec((None, d, tn), lambda i, j: (layer, 0, j)),
            pl.BlockSpec((None, d, tn), lambda i, j: (layer, 0, j + nj)),
        ],
        out_specs=pl.BlockSpec((tm, tn), lambda i, j: (i, j)),
        scratch_shapes=[pltpu.VMEM((tm, d), BF16)],
        compiler_params=_params("parallel", "arbitrary"),
        name="ffn_up",
    )(x, g, wi, wi)


def _ffn_down_kernel(h_ref, w_ref, r_ref, o_ref):
    o_ref[...] = r_ref[...] + 0.5 * _dot(h_ref[...], w_ref[...])


def _ffn_down(h, wo, res, layer, *, tm=512, tn=512):
    m, d_ff = h.shape
    d = res.shape[-1]
    return pl.pallas_call(
        _ffn_down_kernel,
        out_shape=jax.ShapeDtypeStruct((m, d), F32),
        grid=(m // tm, d // tn),
        in_specs=[
            pl.BlockSpec((tm, d_ff), lambda i, j: (i, 0)),
            pl.BlockSpec((None, d_ff, tn), lambda i, j: (layer, 0, j)),
            pl.BlockSpec((tm, tn), lambda i, j: (i, j)),
        ],
        out_specs=pl.BlockSpec((tm, tn), lambda i, j: (i, j)),
        compiler_params=_params("parallel", "arbitrary"),
        name="ffn_down",
    )(h, wo, res)


def _in_proj_kernel(x_ref, g_ref, w_ref, wg_ref, o_ref, og_ref, xn_ref):
    @pl.when(pl.program_id(1) == 0)
    def _():
        xn = _rms_scaled(x_ref[...], g_ref[...]).astype(BF16)
        xn_ref[...] = xn
        og_ref[...] = _dot(xn, wg_ref[...])

    o_ref[...] = _dot(xn_ref[...], w_ref[...])


def _in_proj(x, g, w_main, w_gate, layer, *, tm=512, tn=512):
    m, d = x.shape
    n = w_main.shape[-1]
    ng = w_gate.shape[-1]
    return pl.pallas_call(
        _in_proj_kernel,
        out_shape=(jax.ShapeDtypeStruct((m, n), F32), jax.ShapeDtypeStruct((m, ng), F32)),
        grid=(m // tm, n // tn),
        in_specs=[
            pl.BlockSpec((tm, d), lambda i, j: (i, 0)),
            pl.BlockSpec((None, 1, d), lambda i, j: (layer, 0, 0)),
            pl.BlockSpec((None, d, tn), lambda i, j: (layer, 0, j)),
            pl.BlockSpec((None, d, ng), lambda i, j: (layer, 0, 0)),
        ],
        out_specs=(
            pl.BlockSpec((tm, tn), lambda i, j: (i, j)),
            pl.BlockSpec((tm, ng), lambda i, j: (i, 0)),
        ),
        scratch_shapes=[pltpu.VMEM((tm, d), BF16)],
        compiler_params=_params("parallel", "arbitrary"),
        name="in_proj",
    )(x, g, w_main, w_gate)


def _dil_attn_kernel(q_ref, kc_ref, kp_ref, vc_ref, vp_ref, o_ref, l_ref, *, class_blocks):
    g = pl.program_id(1)
    n = pl.program_id(2)
    lb = jnp.int32(class_blocks[-1])
    for gi in range(len(class_blocks) - 2, -1, -1):
        lb = jnp.where(g == gi, jnp.int32(class_blocks[gi]), lb)
    prev_ok = lax.rem(n, lb) != 0
    r = lax.broadcasted_iota(jnp.int32, (Q_BLOCK, Q_BLOCK), 0)
    c = lax.broadcasted_iota(jnp.int32, (Q_BLOCK, Q_BLOCK), 1)
    mask_cur = c <= r
    mask_prev = c >= r + jnp.where(prev_ok, 0, Q_BLOCK)
    scale = A_HEAD_DIM ** -0.5
    for h in range(A_HEADS_PER_GROUP):
        hs = slice(h * A_HEAD_DIM, (h + 1) * A_HEAD_DIM)
        qh = (q_ref[:, hs] * scale).astype(BF16)
        s_c = _dot_nt(qh, kc_ref[:, hs].astype(BF16))
        s_p = _dot_nt(qh, kp_ref[:, hs].astype(BF16))
        s_c = jnp.where(mask_cur, s_c, NEG)
        s_p = jnp.where(mask_prev, s_p, NEG)
        m = jnp.maximum(jnp.max(s_c, axis=-1, keepdims=True), jnp.max(s_p, axis=-1, keepdims=True))
        p_c = jnp.exp(s_c - m)
        p_p = jnp.exp(s_p - m)
        l = jnp.sum(p_c, axis=-1, keepdims=True) + jnp.sum(p_p, axis=-1, keepdims=True)
        o = _dot(p_c.astype(BF16), vc_ref[:, hs].astype(BF16)) + _dot(p_p.astype(BF16), vp_ref[:, hs].astype(BF16))
        o_ref[:, hs] = o / l
        l_ref[:, hs] = jnp.broadcast_to(m + jnp.log(l), (Q_BLOCK, A_HEAD_DIM))


def _dil_attn(qp, kp, vp):
    bsz, ng, s, w = qp.shape
    nq = s // Q_BLOCK
    class_blocks = tuple((s // dil) // Q_BLOCK for _, dil in A_GROUPS)
    cur = pl.BlockSpec((None, None, Q_BLOCK, w), lambda b, g, n: (b, g, n, 0))
    prev = pl.BlockSpec((None, None, Q_BLOCK, w), lambda b, g, n: (b, g, jnp.maximum(n - 1, 0), 0))
    out = jax.ShapeDtypeStruct((bsz, ng, s, w), F32)
    return pl.pallas_call(
        functools.partial(_dil_attn_kernel, class_blocks=class_blocks),
        out_shape=(out, out),
        grid=(bsz, ng, nq),
        in_specs=[cur, cur, prev, cur, prev],
        out_specs=(cur, cur),
        compiler_params=_params("parallel", "parallel", "arbitrary"),
        name="dilated_attention",
    )(qp, kp, kp, vp, vp)


def _class_major(z, dil):
    bsz, s, c = z.shape
    return z.reshape(bsz, s // dil, dil, c).transpose(0, 2, 1, 3).reshape(bsz, s, c)


def _time_major(z, dil):
    bsz, s, c = z.shape
    return z.reshape(bsz, dil, s // dil, c).transpose(0, 2, 1, 3).reshape(bsz, s, c)


def _pool_kernel(x_ref, w_ref, sc_ref, o_ref):
    s = x_ref.shape[0]
    t = lax.broadcasted_iota(jnp.int32, (s, B_GROUP_DIM), 0)
    for gi, win in enumerate(B_WINDOWS):
        cs = slice(gi * B_GROUP_DIM, (gi + 1) * B_GROUP_DIM)
        x = x_ref[:, cs]
        acc = x
        k = 1
        while k < win:
            acc = acc + jnp.where(t >= k, pltpu.roll(acc, k, axis=0), 0.0)
            k *= 2
        cnt = jnp.minimum(t + 1, win).astype(F32)
        z = acc / cnt - x
        o_ref[:, cs] = _dot(z.astype(BF16), w_ref[gi]) * sc_ref[:, cs]


def _pool_mixer(cols3, pool_w, pool_scale, layer):
    bsz, s, _ = cols3.shape
    ng = len(B_WINDOWS)
    return pl.pallas_call(
        _pool_kernel,
        out_shape=jax.ShapeDtypeStruct((bsz, s, B_WIDTH), F32),
        grid=(bsz,),
        in_specs=[
            pl.BlockSpec((None, s, B_WIDTH), lambda b: (b, 0, OFF_XB // B_WIDTH)),
            pl.BlockSpec((None, ng, B_GROUP_DIM, B_GROUP_DIM), lambda b: (layer, 0, 0, 0)),
            pl.BlockSpec((None, 1, B_WIDTH), lambda b: (layer, 0, 0)),
        ],
        out_specs=pl.BlockSpec((None, s, B_WIDTH), lambda b: (b, 0, 0)),
        compiler_params=_params("parallel"),
        name="pool_mixer",
    )(cols3, pool_w, pool_scale)


def _gelu_tanh(x):
    return 0.5 * x * (1.0 + jnp.tanh(np.sqrt(2.0 / np.pi).astype(np.float32) * (x + 0.044715 * (x * x * x))))


def _compress_one(z_ref, pe_ref, w1_ref, w2_ref, o_ref):
    nch = z_ref.shape[0] // CMP_STRIDE
    dh = C_HEAD_DIM
    first = jnp.zeros((nch, dh), F32)
    second = jnp.zeros((nch, dh), F32)
    for p in range(CMP_STRIDE):
        zp = z_ref[pl.ds(p, nch, stride=CMP_STRIDE), :]
        a = (zp + pe_ref[p:p + 1, :]).astype(BF16)
        b = (zp + pe_ref[CMP_STRIDE + p:CMP_STRIDE + p + 1, :]).astype(BF16)
        first = first + _dot(a, w1_ref[p * dh:(p + 1) * dh, :])
        second = second + _dot(b, w1_ref[(CMP_STRIDE + p) * dh:(CMP_STRIDE + p + 1) * dh, :])
    pre = first + pltpu.roll(second, nch - 1, axis=0)
    o_ref[...] = _dot(_gelu_tanh(pre).astype(BF16), w2_ref[...])


def _compress_kernel(zk_ref, zv_ref, pek_ref, w1k_ref, w2k_ref, pev_ref, w1v_ref, w2v_ref, kc_ref, vc_ref):
    _compress_one(zk_ref, pek_ref, w1k_ref, w2k_ref, kc_ref)
    _compress_one(zv_ref, pev_ref, w1v_ref, w2v_ref, vc_ref)


def _compress(cols3, pe_k, w1_k, w2_k, pe_v, w1_v, w2_v, layer):
    bsz, s, _ = cols3.shape
    nch = s // CMP_STRIDE
    dh = C_HEAD_DIM
    cdim = CMP_LEN * dh

    def col(off):
        return pl.BlockSpec((None, s, dh), lambda b, g: (b, 0, off // dh + g))

    def wspec(shape):
        return pl.BlockSpec((None,) + shape, lambda b, g: (layer,) + (0,) * len(shape))

    out = jax.ShapeDtypeStruct((bsz, C_KV_GROUPS, nch, dh), F32)
    ospec = pl.BlockSpec((None, None, nch, dh), lambda b, g: (b, g, 0, 0))
    return pl.pallas_call(
        _compress_kernel,
        out_shape=(out, out),
        grid=(bsz, C_KV_GROUPS),
        in_specs=[col(OFF_KCMP), col(OFF_VCMP),
                  wspec((CMP_LEN, dh)), wspec((cdim, dh)), wspec((dh, dh)),
                  wspec((CMP_LEN, dh)), wspec((cdim, dh)), wspec((dh, dh))],
        out_specs=(ospec, ospec),
        compiler_params=_params("parallel", "parallel"),
        name="nsa_compress",
    )(cols3, cols3, pe_k, w1_k, w2_k, pe_v, w1_v, w2_v)


def _split3(x):
    hi = x.astype(BF16)
    r1 = x - hi.astype(F32)
    mid = r1.astype(BF16)
    lo = (r1 - mid.astype(F32)).astype(BF16)
    return hi, mid, lo


def _nsa_kernel(q_ref, kc_ref, vc_ref, ks_ref, vs_ref, kw_ref, vw_ref, gt_ref, ov_ref, ex_ref, o_ref, sel_ref):
    n = pl.program_id(2)
    nkt = sel_ref.shape[0]
    nb = nkt * (Q_BLOCK // SEL_BLOCK)
    hg = C_HEADS_PER_GROUP
    dh = C_HEAD_DIM
    scale = dh ** -0.5
    t0 = n * Q_BLOCK
    row = t0 + lax.broadcasted_iota(jnp.int32, (Q_BLOCK, LANES), 0)
    lane = lax.broadcasted_iota(jnp.int32, (Q_BLOCK, LANES), 1)

    qs = [(q_ref[:, h * dh:(h + 1) * dh] * scale).astype(BF16) for h in range(hg)]

    cvalid = lane * CMP_STRIDE + (CMP_LEN - 1) <= row
    kc = kc_ref[...].astype(BF16)
    vc = vc_ref[...].astype(BF16)
    o_cmp = []
    psum = jnp.zeros((Q_BLOCK, LANES), F32)
    for h in range(hg):
        s = jnp.where(cvalid, _dot_nt(qs[h], kc), NEG)
        m = jnp.max(s, axis=-1, keepdims=True)
        p = jnp.exp(s - m)
        p = jnp.where(cvalid, p / jnp.sum(p, axis=-1, keepdims=True), 0.0)
        o_cmp.append(_dot(p.astype(BF16), vc))
        psum = psum + p

    hi, mid, lo = _split3(psum)
    ov = ov_ref[...]
    imp = _dot(hi, ov) + _dot(mid, ov) + _dot(lo, ov)
    tb = lax.shift_right_logical(row, int(np.log2(SEL_BLOCK)))
    forced = (lane == 0) | (lane == tb) | (lane == tb - 1)
    score = jnp.where(lane > tb, -1.0, imp + jnp.where(forced, FORCE_BONUS, 0.0))
    score = jnp.where(lane < nb, score, -2.0)
    lane_f = lane.astype(F32)
    sel = jnp.zeros((Q_BLOCK, LANES), F32)
    for _ in range(min(SEL_TOPN, nb)):
        mx = jnp.max(score, axis=-1, keepdims=True)
        idx = jnp.min(jnp.where(score == mx, lane_f, float(LANES)), axis=-1, keepdims=True)
        hit = lane_f == idx
        sel = jnp.where(hit, 1.0, sel)
        score = jnp.where(hit, -3.0, score)
    sel_b = sel.astype(BF16)
    for kt in range(nkt):
        sel_ref[kt] = _dot(sel_b, ex_ref[:, kt * Q_BLOCK:(kt + 1) * Q_BLOCK])

    q3 = jnp.concatenate(qs, axis=0)

    def attend(k_ref, v_ref, lo_tile, mask_fn):
        def body(kt, carry):
            m, l, acc = carry
            off = pl.multiple_of(kt * Q_BLOCK, Q_BLOCK)
            k = k_ref[pl.ds(off, Q_BLOCK), :].astype(BF16)
            v = v_ref[pl.ds(off, Q_BLOCK), :].astype(BF16)
            ok = mask_fn(kt, off + lane)
            s = jnp.where(jnp.concatenate([ok] * hg, axis=0), _dot_nt(q3, k), NEG)
            m_new = jnp.maximum(m, jnp.max(s, axis=-1, keepdims=True))
            alpha = jnp.exp(m - m_new)
            p = jnp.exp(s - m_new)
            l = alpha * l + jnp.sum(p, axis=-1, keepdims=True)
            acc = alpha * acc + _dot(p.astype(BF16), v)
            return m_new, l, acc

        init = (jnp.full((hg * Q_BLOCK, 1), NEG, F32), jnp.zeros((hg * Q_BLOCK, 1), F32),
                jnp.zeros((hg * Q_BLOCK, dh), F32))
        _, l, acc = lax.fori_loop(lo_tile, n + 1, body, init)
        return acc / l

    o_slc = attend(ks_ref, vs_ref, 0, lambda kt, kpos: (sel_ref[kt] > 0.5) & (kpos <= row))
    o_win = attend(kw_ref, vw_ref, jnp.maximum(n - WIN // Q_BLOCK, 0),
                   lambda kt, kpos: (kpos <= row) & (kpos > row - WIN))

    gt = _sigmoid(gt_ref[...])
    for h in range(hg):
        rs = slice(h * Q_BLOCK, (h + 1) * Q_BLOCK)
        o_ref[:, h * dh:(h + 1) * dh] = (gt[:, 3 * h:3 * h + 1] * o_cmp[h]
                                         + gt[:, 3 * h + 1:3 * h + 2] * o_slc[rs]
                                         + gt[:, 3 * h + 2:3 * h + 3] * o_win[rs])


def _nsa_constants(s):
    nch = s // CMP_STRIDE
    nb = s // SEL_BLOCK
    ci = np.arange(nch)[:, None] * CMP_STRIDE
    bj = np.arange(LANES)[None, :] * SEL_BLOCK
    overlap = (ci < bj + SEL_BLOCK) & (ci + CMP_LEN > bj) & (np.arange(LANES)[None, :] < nb)
    overlap &= np.arange(nch)[:, None] < (s - CMP_LEN) // CMP_STRIDE + 1
    expand = np.arange(LANES)[:, None] == (np.arange(s)[None, :] // SEL_BLOCK)
    return jnp.asarray(overlap, BF16), jnp.asarray(expand, BF16)


def _nsa(cols3, kc, vc, gates3):
    bsz, s, _ = cols3.shape
    nq = s // Q_BLOCK
    nch = s // CMP_STRIDE
    dh = C_HEAD_DIM
    assert nch == LANES and s // SEL_BLOCK <= LANES
    overlap, expand = _nsa_constants(s)

    def kv(off):
        return pl.BlockSpec((None, s, dh), lambda b, g, n: (b, 0, off // dh + g))

    cmp_spec = pl.BlockSpec((None, None, nch, dh), lambda b, g, n: (b, g, 0, 0))
    return pl.pallas_call(
        _nsa_kernel,
        out_shape=jax.ShapeDtypeStruct((bsz, s, C_WIDTH), F32),
        grid=(bsz, C_KV_GROUPS, nq),
        in_specs=[
            pl.BlockSpec((None, Q_BLOCK, C_GROUP_WIDTH), lambda b, g, n: (b, n, OFF_QC // C_GROUP_WIDTH + g)),
            cmp_spec, cmp_spec,
            kv(OFF_KSLC), kv(OFF_VSLC), kv(OFF_KWIN), kv(OFF_VWIN),
            pl.BlockSpec((None, Q_BLOCK, LANES), lambda b, g, n: (b, n, g)),
            pl.BlockSpec((nch, LANES), lambda b, g, n: (0, 0)),
            pl.BlockSpec((LANES, s), lambda b, g, n: (0, 0)),
        ],
        out_specs=pl.BlockSpec((None, Q_BLOCK, C_GROUP_WIDTH), lambda b, g, n: (b, n, g)),
        scratch_shapes=[pltpu.VMEM((nq, Q_BLOCK, Q_BLOCK), F32)],
        compiler_params=_params("parallel", "parallel", "arbitrary"),
        name="nsa_attention",
    )(cols3, kc, vc, cols3, cols3, cols3, cols3, gates3, overlap, expand)


def _merge_kernel(oa_ref, la_ref, zb_ref, oc_ref, g0_ref, g1_ref, g2_ref, x_ref,
                  pa_ref, pb_ref, pc_ref, wo_ref, o_ref):
    lse = la_ref[...]
    e = jnp.exp(lse - jnp.max(lse, axis=0, keepdims=True))
    oa = jnp.sum(e * oa_ref[...], axis=0) / jnp.sum(e, axis=0)
    ya = _dot(oa.astype(BF16), pa_ref[...])
    yb = _dot(zb_ref[...].astype(BF16), pb_ref[...])
    yc = _dot(oc_ref[...].astype(BF16), pc_ref[...])
    mix = _sigmoid(g0_ref[...]) * ya + _sigmoid(g1_ref[...]) * yb + _sigmoid(g2_ref[...]) * yc
    o_ref[...] = x_ref[...] + _dot(mix.astype(BF16), wo_ref[...])


def _merge(oa, la, zb, oc, cols, x, proj_a, proj_b, proj_c, w_out, layer, *, tm=256):
    m, d = x.shape
    ng = oa.shape[0]

    def rows(w):
        return pl.BlockSpec((tm, w), lambda i: (i, 0))

    def gate(k):
        return pl.BlockSpec((tm, d), lambda i: (i, OFF_GM // d + k))

    def weight(k):
        return pl.BlockSpec((None, k, d), lambda i: (layer, 0, 0), pipeline_mode=pl.Buffered(1))

    grp = pl.BlockSpec((ng, tm, A_GROUP_WIDTH), lambda i: (0, i, 0))
    return pl.pallas_call(
        _merge_kernel,
        out_shape=jax.ShapeDtypeStruct((m, d), F32),
        grid=(m // tm,),
        in_specs=[grp, grp, rows(B_WIDTH), rows(C_WIDTH), gate(0), gate(1), gate(2), rows(d),
                  weight(A_GROUP_WIDTH), weight(B_WIDTH), weight(C_WIDTH), weight(d)],
        out_specs=rows(d),
        compiler_params=_params("parallel"),
        name="merge",
    )(oa, la, zb, oc, cols, cols, cols, x, proj_a, proj_b, proj_c, w_out)


def _final_norm_kernel(x_ref, g_ref, o_ref):
    o_ref[...] = _rms_scaled(x_ref[...], g_ref[...])


def _final_norm(x, g, *, tm=512):
    m, d = x.shape
    return pl.pallas_call(
        _final_norm_kernel,
        out_shape=jax.ShapeDtypeStruct((m, d), F32),
        grid=(m // tm,),
        in_specs=[pl.BlockSpec((tm, d), lambda i: (i, 0)), pl.BlockSpec((1, d), lambda i: (0, 0))],
        out_specs=pl.BlockSpec((tm, d), lambda i: (i, 0)),
        compiler_params=_params("parallel"),
        name="final_norm",
    )(x, g)


def _reorder_w_in(w_in):
    a3 = 3 * A_WIDTH
    xb_end = a3 + B_WIDTH
    qc_end = xb_end + C_WIDTH
    kv_end = qc_end + 6 * C_KV_WIDTH
    gm_start = kv_end + N_GATES
    w_main = jnp.concatenate(
        [w_in[..., gm_start:], w_in[..., :a3], w_in[..., xb_end:qc_end], w_in[..., a3:xb_end],
         w_in[..., qc_end:kv_end]], axis=-1).astype(BF16)
    per_group = 3 * C_HEADS_PER_GROUP
    pad = jnp.zeros(w_in.shape[:-1] + (LANES - per_group,), w_in.dtype)
    w_gate = jnp.concatenate(
        [piece for g in range(C_KV_GROUPS)
         for piece in (w_in[..., kv_end + g * per_group:kv_end + (g + 1) * per_group], pad)], axis=-1).astype(BF16)
    return w_main, w_gate


def _mixing(x, layer, g_mix, w_main, w_gate, pool_w, pool_scale, pe_k, w1_k, w2_k, pe_v, w1_v, w2_v,
            proj_a, proj_b, proj_c, w_out, bsz, s):
    m, d = x.shape
    cols, gates = _in_proj(x, g_mix, w_main, w_gate, layer)
    cols3 = cols.reshape(bsz, s, N_COLS)
    gates3 = gates.reshape(bsz, s, C_KV_GROUPS * LANES)

    def grouped(off):
        return jnp.stack([_class_major(cols3[:, :, off + gi * A_GROUP_WIDTH:off + (gi + 1) * A_GROUP_WIDTH], dil)
                          for gi, (_, dil) in enumerate(A_GROUPS)], axis=1)

    oa_p, la_p = _dil_attn(grouped(OFF_QA), grouped(OFF_KA), grouped(OFF_VA))
    oa = jnp.stack([_time_major(oa_p[:, gi], dil) for gi, (_, dil) in enumerate(A_GROUPS)]).reshape(-1, m, A_GROUP_WIDTH)
    la = jnp.stack([_time_major(la_p[:, gi], dil) for gi, (_, dil) in enumerate(A_GROUPS)]).reshape(-1, m, A_GROUP_WIDTH)

    zb = _pool_mixer(cols3, pool_w, pool_scale, layer).reshape(m, B_WIDTH)

    kc, vc = _compress(cols3, pe_k, w1_k, w2_k, pe_v, w1_v, w2_v, layer)
    oc = _nsa(cols3, kc, vc, gates3).reshape(m, C_WIDTH)

    return _merge(oa, la, zb, oc, cols, x, proj_a, proj_b, proj_c, w_out, layer)


def kernel(x, ffn1_norm, ffn1_wi, ffn1_wo, mix_norm, w_in, pool_w, pool_scale, cmp_pe_k, cmp_w1_k, cmp_w2_k,
           cmp_pe_v, cmp_w1_v, cmp_w2_v, proj_a, proj_b, proj_c, w_out, ffn2_norm, ffn2_wi, ffn2_wo, final_norm):
    bsz, s, d = x.shape
    depth = ffn1_wi.shape[0]
    for win, dil in A_GROUPS:
        assert win // dil == Q_BLOCK and s % (dil * Q_BLOCK) == 0
    assert CMP_LEN == 2 * CMP_STRIDE and all(w & (w - 1) == 0 for w in B_WINDOWS)

    bf = lambda w: w.astype(BF16)
    row3 = lambda g: g.reshape(g.shape[0], 1, g.shape[-1])
    w_main, w_gate = _reorder_w_in(w_in)
    ffn1_wi, ffn1_wo, ffn2_wi, ffn2_wo = bf(ffn1_wi), bf(ffn1_wo), bf(ffn2_wi), bf(ffn2_wo)
    pool_w, cmp_w1_k, cmp_w2_k, cmp_w1_v, cmp_w2_v = bf(pool_w), bf(cmp_w1_k), bf(cmp_w2_k), bf(cmp_w1_v), bf(cmp_w2_v)
    proj_a, proj_b, proj_c, w_out = bf(proj_a), bf(proj_b), bf(proj_c), bf(w_out)
    ffn1_norm, mix_norm, ffn2_norm, pool_scale = row3(ffn1_norm), row3(mix_norm), row3(ffn2_norm), row3(pool_scale)

    x = x.reshape(bsz * s, d)
    for layer in range(depth):
        x = _ffn_down(_ffn_up(x, ffn1_norm, ffn1_wi, layer), ffn1_wo, x, layer)
        x = _mixing(x, layer, mix_norm, w_main, w_gate, pool_w, pool_scale, cmp_pe_k, cmp_w1_k, cmp_w2_k,
                    cmp_pe_v, cmp_w1_v, cmp_w2_v, proj_a, proj_b, proj_c, w_out, bsz, s)
        x = _ffn_down(_ffn_up(x, ffn2_norm, ffn2_wi, layer), ffn2_wo, x, layer)
    return _final_norm(x, final_norm.reshape(1, d)).reshape(bsz, s, d)
```

```python
import functools

import numpy as np
import jax
import jax.numpy as jnp
from jax import lax
from jax.experimental import pallas as pl
from jax.experimental.pallas import tpu as pltpu

F32 = jnp.float32
BF16 = jnp.bfloat16

EPS = 1e-6
NEG = -1e30
Q_BLOCK = 128
LANES = 128

A_GROUPS = ((128, 1), (512, 4), (2048, 16))
A_HEADS_PER_GROUP = 4
A_HEAD_DIM = 64
A_GROUP_WIDTH = A_HEADS_PER_GROUP * A_HEAD_DIM
A_WIDTH = A_GROUP_WIDTH * len(A_GROUPS)

B_WINDOWS = (2, 4, 8, 16)
B_GROUP_DIM = 128
B_WIDTH = B_GROUP_DIM * len(B_WINDOWS)

C_KV_GROUPS = 2
C_HEADS_PER_GROUP = 3
C_HEADS = C_KV_GROUPS * C_HEADS_PER_GROUP
C_HEAD_DIM = 128
C_GROUP_WIDTH = C_HEADS_PER_GROUP * C_HEAD_DIM
C_WIDTH = C_HEADS * C_HEAD_DIM
C_KV_WIDTH = C_KV_GROUPS * C_HEAD_DIM
CMP_LEN = 32
CMP_STRIDE = 16
SEL_BLOCK = 64
SEL_TOPN = 8
FORCE_BONUS = 100.0
WIN = 512
N_GATES = 3 * C_HEADS
SEL_CHUNK = 512

N_BRANCH = 3
D_GATE = 2048

FOFF_GM = 0
FOFF_XB = N_BRANCH * D_GATE
FOFF_KCMP = FOFF_XB + B_WIDTH
FOFF_VCMP = FOFF_KCMP + C_KV_WIDTH
N_F32_COLS = FOFF_VCMP + C_KV_WIDTH
BOFF_QA = 0
BOFF_KA = BOFF_QA + A_WIDTH
BOFF_VA = BOFF_KA + A_WIDTH
BOFF_QC = BOFF_VA + A_WIDTH
BOFF_KSLC = BOFF_QC + C_WIDTH
BOFF_VSLC = BOFF_KSLC + C_KV_WIDTH
BOFF_KWIN = BOFF_VSLC + C_KV_WIDTH
BOFF_VWIN = BOFF_KWIN + C_KV_WIDTH
N_BF16_COLS = BOFF_VWIN + C_KV_WIDTH

VMEM_LIMIT = 56 * 1024 * 1024
ROW_TILE = 1024
COL_TILE = 512


def _params(*sem):
    return pltpu.CompilerParams(dimension_semantics=sem, vmem_limit_bytes=VMEM_LIMIT)


def _dot(a, b):
    return jnp.dot(a, b, preferred_element_type=F32)


def _dot_nt(a, b):
    return lax.dot_general(a, b, (((1,), (1,)), ((), ())), preferred_element_type=F32)


def _sigmoid(x):
    return 1.0 / (1.0 + jnp.exp(-x))


def _rms_scaled(x, g):
    return x * lax.rsqrt(jnp.mean(x * x, axis=-1, keepdims=True) + EPS) * g


def _ffn_up_kernel(x_ref, g_ref, wa_ref, wb_ref, h_ref, xn_ref):
    @pl.when(pl.program_id(1) == 0)
    def _():
        xn_ref[...] = _rms_scaled(x_ref[...], g_ref[...]).astype(BF16)

    xn = xn_ref[...]
    a = _dot(xn, wa_ref[...])
    b = _dot(xn, wb_ref[...])
    h_ref[...] = (a * _sigmoid(a) * b).astype(BF16)


def _ffn_up(x, g, wi, layer, *, tm=ROW_TILE, tn=COL_TILE):
    m, d = x.shape
    d_ff = wi.shape[-1] // 2
    nj = d_ff // tn
    return pl.pallas_call(
        _ffn_up_kernel,
        out_shape=jax.ShapeDtypeStruct((m, d_ff), BF16),
        grid=(m // tm, nj),
        in_specs=[
            pl.BlockSpec((tm, d), lambda i, j: (i, 0)),
            pl.BlockSpec((None, 1, d), lambda i, j: (layer, 0, 0)),
            pl.BlockSpec((None, d, tn), lambda i, j: (layer, 0, j)),
            pl.BlockSpec((None, d, tn), lambda i, j: (layer, 0, j + nj)),
        ],
        out_specs=pl.BlockSpec((tm, tn), lambda i, j: (i, j)),
        scratch_shapes=[pltpu.VMEM((tm, d), BF16)],
        compiler_params=_params("parallel", "arbitrary"),
        name="ffn_up",
    )(x, g, wi, wi)


def _ffn_down_kernel(h_ref, w_ref, r_ref, o_ref):
    o_ref[...] = r_ref[...] + 0.5 * _dot(h_ref[...], w_ref[...])


def _ffn_down(h, wo, res, layer, *, tm=ROW_TILE, tn=COL_TILE):
    m, d_ff = h.shape
    d = res.shape[-1]
    return pl.pallas_call(
        _ffn_down_kernel,
        out_shape=jax.ShapeDtypeStruct((m, d), F32),
        grid=(m // tm, d // tn),
        in_specs=[
            pl.BlockSpec((tm, d_ff), lambda i, j: (i, 0)),
            pl.BlockSpec((None, d_ff, tn), lambda i, j: (layer, 0, j)),
            pl.BlockSpec((tm, tn), lambda i, j: (i, j)),
        ],
        out_specs=pl.BlockSpec((tm, tn), lambda i, j: (i, j)),
        compiler_params=_params("parallel", "arbitrary"),
        name="ffn_down",
    )(h, wo, res)


def _in_proj_kernel(x_ref, g_ref, w_ref, wg_ref, sc_ref, of_ref, ob_ref, og_ref, xn_ref, *, n_f32_tiles):
    j = pl.program_id(1)

    @pl.when(j == 0)
    def _():
        xn = _rms_scaled(x_ref[...], g_ref[...]).astype(BF16)
        xn_ref[...] = xn
        og_ref[...] = _dot(xn, wg_ref[...])

    acc = _dot(xn_ref[...], w_ref[...])

    @pl.when(j < n_f32_tiles)
    def _():
        of_ref[...] = acc

    @pl.when(j >= n_f32_tiles)
    def _():
        ob_ref[...] = (acc * sc_ref[...]).astype(BF16)


def _in_proj(x, g, w_main, w_gate, col_scale, layer, *, tm=ROW_TILE, tn=COL_TILE):
    m, d = x.shape
    ng = w_gate.shape[-1]
    nf = N_F32_COLS // tn
    nb = N_BF16_COLS // tn
    return pl.pallas_call(
        functools.partial(_in_proj_kernel, n_f32_tiles=nf),
        out_shape=(jax.ShapeDtypeStruct((m, N_F32_COLS), F32), jax.ShapeDtypeStruct((m, N_BF16_COLS), BF16),
                   jax.ShapeDtypeStruct((m, ng), F32)),
        grid=(m // tm, nf + nb),
        in_specs=[
            pl.BlockSpec((tm, d), lambda i, j: (i, 0)),
            pl.BlockSpec((None, 1, d), lambda i, j: (layer, 0, 0)),
            pl.BlockSpec((None, d, tn), lambda i, j: (layer, 0, j)),
            pl.BlockSpec((None, d, ng), lambda i, j: (layer, 0, 0)),
            pl.BlockSpec((1, tn), lambda i, j: (0, j)),
        ],
        out_specs=(
            pl.BlockSpec((tm, tn), lambda i, j: (i, jnp.minimum(j, nf - 1))),
            pl.BlockSpec((tm, tn), lambda i, j: (i, jnp.maximum(j - nf, 0))),
            pl.BlockSpec((tm, ng), lambda i, j: (i, 0)),
        ),
        scratch_shapes=[pltpu.VMEM((tm, d), BF16)],
        compiler_params=_params("parallel", "arbitrary"),
        name="in_proj",
    )(x, g, w_main, w_gate, col_scale)


def _dil_attn_kernel(q_ref, kc_ref, kp_ref, vc_ref, vp_ref, o_ref, l_ref, *, has_prev):
    r = lax.broadcasted_iota(jnp.int32, (Q_BLOCK, Q_BLOCK), 0)
    c = lax.broadcasted_iota(jnp.int32, (Q_BLOCK, Q_BLOCK), 1)
    bias_cur = jnp.where(c <= r, 0.0, NEG)
    if has_prev:
        first = pl.program_id(2) == 0
        bias_prev = jnp.where(c >= r + jnp.where(first, Q_BLOCK, 0), 0.0, NEG)
    head = lax.shift_right_logical(lax.broadcasted_iota(jnp.int32, (Q_BLOCK, A_GROUP_WIDTH), 1),
                                   int(np.log2(A_HEAD_DIM)))
    q = q_ref[...]
    kc, vc = kc_ref[...], vc_ref[...]
    o_acc = jnp.zeros((Q_BLOCK, A_GROUP_WIDTH), F32)
    l_acc = jnp.zeros((Q_BLOCK, A_GROUP_WIDTH), F32)
    for h in range(A_HEADS_PER_GROUP):
        mine = head == h
        qh = q * jnp.where(mine, 1.0, 0.0).astype(BF16)
        s_c = _dot_nt(qh, kc) + bias_cur
        m = jnp.max(s_c, axis=-1, keepdims=True)
        if has_prev:
            s_p = _dot_nt(qh, kp_ref[...]) + bias_prev
            m = jnp.maximum(m, jnp.max(s_p, axis=-1, keepdims=True))
        p_c = jnp.exp(s_c - m)
        l = jnp.sum(p_c, axis=-1, keepdims=True)
        pv = _dot(p_c.astype(BF16), vc)
        if has_prev:
            p_p = jnp.exp(s_p - m)
            l = l + jnp.sum(p_p, axis=-1, keepdims=True)
            pv = pv + _dot(p_p.astype(BF16), vp_ref[...])
        o_acc = jnp.where(mine, pv / l, o_acc)
        l_acc = jnp.where(mine, m + jnp.log(l), l_acc)
    o_ref[...] = o_acc
    l_ref[...] = l_acc


def _dil_attn(cb3, gi):
    bsz, s, nc = cb3.shape
    dil = A_GROUPS[gi][1]
    w = A_GROUP_WIDTH
    rows = s // dil
    tiles = rows // Q_BLOCK
    view = cb3.reshape(bsz, rows, dil * nc)

    def col(off, prev):
        base = (off + gi * w) // w

        def index(b, r, i):
            return (b, jnp.maximum(i - 1, 0) if prev else i, r * (nc // w) + base)

        return pl.BlockSpec((None, Q_BLOCK, w), index)

    out = jax.ShapeDtypeStruct((bsz, rows, dil * w), F32)
    ospec = pl.BlockSpec((None, Q_BLOCK, w), lambda b, r, i: (b, i, r))
    o, lse = pl.pallas_call(
        functools.partial(_dil_attn_kernel, has_prev=tiles > 1),
        out_shape=(out, out),
        grid=(bsz, dil, tiles),
        in_specs=[col(BOFF_QA, False), col(BOFF_KA, False), col(BOFF_KA, True), col(BOFF_VA, False), col(BOFF_VA, True)],
        out_specs=(ospec, ospec),
        compiler_params=_params("parallel", "parallel", "arbitrary"),
        name=f"dilated_attention_g{gi}",
    )(view, view, view, view, view)
    return o.reshape(bsz * s, w), lse.reshape(bsz * s, w)


def _pool_kernel(x_ref, w_ref, sc_ref, o_ref):
    s = x_ref.shape[0]
    t = lax.broadcasted_iota(jnp.int32, (s, B_GROUP_DIM), 0)
    for gi, win in enumerate(B_WINDOWS):
        cs = slice(gi * B_GROUP_DIM, (gi + 1) * B_GROUP_DIM)
        x = x_ref[:, cs]
        acc = x
        k = 1
        while k < win:
            acc = acc + jnp.where(t >= k, pltpu.roll(acc, k, axis=0), 0.0)
            k *= 2
        cnt = jnp.minimum(t + 1, win).astype(F32)
        z = acc / cnt - x
        o_ref[:, cs] = _dot(z.astype(BF16), w_ref[gi]) * sc_ref[:, cs]


def _pool_mixer(cf3, pool_w, pool_scale, layer):
    bsz, s, _ = cf3.shape
    ng = len(B_WINDOWS)
    return pl.pallas_call(
        _pool_kernel,
        out_shape=jax.ShapeDtypeStruct((bsz, s, B_WIDTH), F32),
        grid=(bsz,),
        in_specs=[
            pl.BlockSpec((None, s, B_WIDTH), lambda b: (b, 0, FOFF_XB // B_WIDTH)),
            pl.BlockSpec((None, ng, B_GROUP_DIM, B_GROUP_DIM), lambda b: (layer, 0, 0, 0)),
            pl.BlockSpec((None, 1, B_WIDTH), lambda b: (layer, 0, 0)),
        ],
        out_specs=pl.BlockSpec((None, s, B_WIDTH), lambda b: (b, 0, 0)),
        compiler_params=_params("parallel"),
        name="pool_mixer",
    )(cf3, pool_w, pool_scale)


def _gelu_tanh(x):
    return 0.5 * x * (1.0 + jnp.tanh(np.float32(np.sqrt(2.0 / np.pi)) * (x + 0.044715 * (x * x * x))))


def _compress_one(z_ref, pe_ref, w1_ref, w2_ref, o_ref):
    nch = z_ref.shape[0] // CMP_STRIDE
    dh = C_HEAD_DIM
    first = jnp.zeros((nch, dh), F32)
    second = jnp.zeros((nch, dh), F32)
    for p in range(CMP_STRIDE):
        zp = z_ref[pl.ds(p, nch, stride=CMP_STRIDE), :]
        a = (zp + pe_ref[p:p + 1, :]).astype(BF16)
        b = (zp + pe_ref[CMP_STRIDE + p:CMP_STRIDE + p + 1, :]).astype(BF16)
        first = first + _dot(a, w1_ref[p * dh:(p + 1) * dh, :])
        second = second + _dot(b, w1_ref[(CMP_STRIDE + p) * dh:(CMP_STRIDE + p + 1) * dh, :])
    pre = first + pltpu.roll(second, nch - 1, axis=0)
    o_ref[...] = _dot(_gelu_tanh(pre).astype(BF16), w2_ref[...]).astype(BF16)


def _compress_kernel(zk_ref, zv_ref, pek_ref, w1k_ref, w2k_ref, pev_ref, w1v_ref, w2v_ref, kc_ref, vc_ref):
    _compress_one(zk_ref, pek_ref, w1k_ref, w2k_ref, kc_ref)
    _compress_one(zv_ref, pev_ref, w1v_ref, w2v_ref, vc_ref)


def _compress(cf3, pe_k, w1_k, w2_k, pe_v, w1_v, w2_v, layer):
    bsz, s, _ = cf3.shape
    nch = s // CMP_STRIDE
    dh = C_HEAD_DIM
    cdim = CMP_LEN * dh

    def col(off):
        return pl.BlockSpec((None, s, dh), lambda b, g: (b, 0, off // dh + g))

    def wspec(shape):
        return pl.BlockSpec((None,) + shape, lambda b, g: (layer,) + (0,) * len(shape))

    out = jax.ShapeDtypeStruct((bsz, C_KV_GROUPS, nch, dh), BF16)
    ospec = pl.BlockSpec((None, None, nch, dh), lambda b, g: (b, g, 0, 0))
    return pl.pallas_call(
        _compress_kernel,
        out_shape=(out, out),
        grid=(bsz, C_KV_GROUPS),
        in_specs=[col(FOFF_KCMP), col(FOFF_VCMP),
                  wspec((CMP_LEN, dh)), wspec((cdim, dh)), wspec((dh, dh)),
                  wspec((CMP_LEN, dh)), wspec((cdim, dh)), wspec((dh, dh))],
        out_specs=(ospec, ospec),
        compiler_params=_params("parallel", "parallel"),
        name="nsa_compress",
    )(cf3, cf3, pe_k, w1_k, w2_k, pe_v, w1_v, w2_v)


def _split3(x):
    hi = x.astype(BF16)
    r1 = x - hi.astype(F32)
    mid = r1.astype(BF16)
    lo = (r1 - mid.astype(F32)).astype(BF16)
    return hi, mid, lo


def _softmax_pv(s, v):
    m = jnp.max(s, axis=-1, keepdims=True)
    p = jnp.exp(s - m)
    return _dot(p.astype(BF16), v) / jnp.sum(p, axis=-1, keepdims=True)


def _nsa_kernel(q_ref, kc_ref, vc_ref, ks_ref, vs_ref, kw_ref, vw_ref, gt_ref, ovt_ref, ex_ref, o_ref, *, s_len):
    n = pl.program_id(2)
    nb = s_len // SEL_BLOCK
    hg = C_HEADS_PER_GROUP
    dh = C_HEAD_DIM
    t0 = n * Q_BLOCK
    row = t0 + lax.broadcasted_iota(jnp.int32, (Q_BLOCK, LANES), 0)
    lane = lax.broadcasted_iota(jnp.int32, (Q_BLOCK, LANES), 1)

    q3 = jnp.concatenate([q_ref[:, h * dh:(h + 1) * dh] for h in range(hg)], axis=0)
    rows = [slice(h * Q_BLOCK, (h + 1) * Q_BLOCK) for h in range(hg)]
    gt = _sigmoid(gt_ref[...])

    cvalid = lane * CMP_STRIDE + (CMP_LEN - 1) <= row
    s3 = _dot_nt(q3, kc_ref[...])
    vc = vc_ref[...]
    out = []
    psum = jnp.zeros((Q_BLOCK, LANES), F32)
    for h in range(hg):
        s = jnp.where(cvalid, s3[rows[h]], NEG)
        p = jnp.exp(s - jnp.max(s, axis=-1, keepdims=True))
        p = jnp.where(cvalid, p / jnp.sum(p, axis=-1, keepdims=True), 0.0)
        out.append(gt[:, 3 * h:3 * h + 1] * _dot(p.astype(BF16), vc))
        psum = psum + p

    ovt = ovt_ref[...]
    imp_t = sum(_dot_nt(ovt, part) for part in _split3(psum))
    blk = lax.broadcasted_iota(jnp.int32, (nb, Q_BLOCK), 0)
    tb = lax.shift_right_logical(t0 + lax.broadcasted_iota(jnp.int32, (nb, Q_BLOCK), 1), int(np.log2(SEL_BLOCK)))
    forced = (blk == 0) | (blk == tb) | (blk == tb - 1)
    score = jnp.where(blk > tb, -1.0, imp_t + jnp.where(forced, FORCE_BONUS, 0.0))
    blk_f = blk.astype(F32)
    sel_t = jnp.zeros((nb, Q_BLOCK), F32)
    for _ in range(min(SEL_TOPN, nb)):
        mx = jnp.max(score, axis=0, keepdims=True)
        idx = jnp.min(jnp.where(score == mx, blk_f, float(nb)), axis=0, keepdims=True)
        hit = blk_f == idx
        sel_t = jnp.where(hit, 1.0, sel_t)
        score = jnp.where(hit, -3.0, score)
    sel = jnp.concatenate([sel_t, jnp.zeros((LANES - nb, Q_BLOCK), F32)], axis=0).T.astype(BF16)

    wlen = WIN + Q_BLOCK
    w0 = pl.multiple_of(jnp.maximum(t0 - WIN, 0), Q_BLOCK)
    kpos = w0 + lax.broadcasted_iota(jnp.int32, (Q_BLOCK, wlen), 1)
    qpos = t0 + lax.broadcasted_iota(jnp.int32, (Q_BLOCK, wlen), 0)
    bias = jnp.where((kpos <= qpos) & (kpos > qpos - WIN), 0.0, NEG)
    s3 = _dot_nt(q3, kw_ref[pl.ds(w0, wlen), :])
    vw = vw_ref[pl.ds(w0, wlen), :]
    for h in range(hg):
        out[h] = out[h] + gt[:, 3 * h + 2:3 * h + 3] * _softmax_pv(s3[rows[h]] + bias, vw)

    for cls in range(s_len // SEL_CHUNK):
        klen = (cls + 1) * SEL_CHUNK

        @pl.when(n // (SEL_CHUNK // Q_BLOCK) == cls)
        def _(klen=klen):
            kpos = lax.broadcasted_iota(jnp.int32, (Q_BLOCK, klen), 1)
            qpos = t0 + lax.broadcasted_iota(jnp.int32, (Q_BLOCK, klen), 0)
            chosen = _dot(sel, ex_ref[:, :klen])
            bias = jnp.where((chosen > 0.5) & (kpos <= qpos), 0.0, NEG)
            s3 = _dot_nt(q3, ks_ref[:klen, :])
            vs = vs_ref[:klen, :]
            for h in range(hg):
                o_ref[:, h * dh:(h + 1) * dh] = out[h] + gt[:, 3 * h + 1:3 * h + 2] * _softmax_pv(s3[rows[h]] + bias, vs)


def _nsa_constants(s):
    nch = s // CMP_STRIDE
    nb = s // SEL_BLOCK
    n_cmp = (s - CMP_LEN) // CMP_STRIDE + 1
    ci = np.arange(nch)[None, :] * CMP_STRIDE
    bj = np.arange(nb)[:, None] * SEL_BLOCK
    overlap_t = (ci < bj + SEL_BLOCK) & (ci + CMP_LEN > bj) & (np.arange(nch)[None, :] < n_cmp)
    expand = np.arange(LANES)[:, None] == (np.arange(s)[None, :] // SEL_BLOCK)
    return jnp.asarray(overlap_t, BF16), jnp.asarray(expand, BF16)


def _nsa(cb3, kc, vc, gates3):
    bsz, s, _ = cb3.shape
    nq = s // Q_BLOCK
    nch = s // CMP_STRIDE
    nb = s // SEL_BLOCK
    dh = C_HEAD_DIM
    assert nch == LANES and nb <= LANES and nb % 8 == 0 and s % SEL_CHUNK == 0 and s >= WIN + Q_BLOCK
    overlap_t, expand = _nsa_constants(s)

    def kv(off):
        return pl.BlockSpec((None, s, dh), lambda b, g, n: (b, 0, off // dh + g))

    cmp_spec = pl.BlockSpec((None, None, nch, dh), lambda b, g, n: (b, g, 0, 0))
    return pl.pallas_call(
        functools.partial(_nsa_kernel, s_len=s),
        out_shape=jax.ShapeDtypeStruct((bsz, s, C_WIDTH), F32),
        grid=(bsz, C_KV_GROUPS, nq),
        in_specs=[
            pl.BlockSpec((None, Q_BLOCK, C_GROUP_WIDTH), lambda b, g, n: (b, n, BOFF_QC // C_GROUP_WIDTH + g)),
            cmp_spec, cmp_spec,
            kv(BOFF_KSLC), kv(BOFF_VSLC), kv(BOFF_KWIN), kv(BOFF_VWIN),
            pl.BlockSpec((None, Q_BLOCK, LANES), lambda b, g, n: (b, n, g)),
            pl.BlockSpec((nb, nch), lambda b, g, n: (0, 0)),
            pl.BlockSpec((LANES, s), lambda b, g, n: (0, 0)),
        ],
        out_specs=pl.BlockSpec((None, Q_BLOCK, C_GROUP_WIDTH), lambda b, g, n: (b, n, g)),
        compiler_params=_params("parallel", "parallel", "arbitrary"),
        name="nsa_attention",
    )(cb3, kc, vc, cb3, cb3, cb3, cb3, gates3, overlap_t, expand)


def _merge_kernel(*refs):
    ng = len(A_GROUPS)
    oa_refs, la_refs = refs[:ng], refs[ng:2 * ng]
    zb_ref, oc_ref, g0_ref, g1_ref, g2_ref, x_ref, pa_ref, pb_ref, pc_ref, wo_ref, o_ref = refs[2 * ng:]
    lse = [r[...] for r in la_refs]
    m = functools.reduce(jnp.maximum, lse)
    e = [jnp.exp(l - m) for l in lse]
    oa = sum(w * r[...] for w, r in zip(e, oa_refs)) / sum(e)
    ya = _dot(oa.astype(BF16), pa_ref[...])
    yb = _dot(zb_ref[...].astype(BF16), pb_ref[...])
    yc = _dot(oc_ref[...].astype(BF16), pc_ref[...])
    mix = _sigmoid(g0_ref[...]) * ya + _sigmoid(g1_ref[...]) * yb + _sigmoid(g2_ref[...]) * yc
    o_ref[...] = x_ref[...] + _dot(mix.astype(BF16), wo_ref[...])


def _merge(oa, la, zb, oc, cf, x, proj_a, proj_b, proj_c, w_out, layer, *, tm=256):
    m, d = x.shape
    assert d == D_GATE

    def rows(w):
        return pl.BlockSpec((tm, w), lambda i: (i, 0))

    def gate(k):
        return pl.BlockSpec((tm, d), lambda i: (i, FOFF_GM // d + k))

    def weight(k):
        return pl.BlockSpec((None, k, d), lambda i: (layer, 0, 0), pipeline_mode=pl.Buffered(1))

    return pl.pallas_call(
        _merge_kernel,
        out_shape=jax.ShapeDtypeStruct((m, d), F32),
        grid=(m // tm,),
        in_specs=[rows(A_GROUP_WIDTH)] * (2 * len(A_GROUPS))
        + [rows(B_WIDTH), rows(C_WIDTH), gate(0), gate(1), gate(2), rows(d),
           weight(A_GROUP_WIDTH), weight(B_WIDTH), weight(C_WIDTH), weight(d)],
        out_specs=rows(d),
        compiler_params=_params("parallel"),
        name="merge",
    )(*oa, *la, zb, oc, cf, cf, cf, x, proj_a, proj_b, proj_c, w_out)


def _final_norm_kernel(x_ref, g_ref, o_ref):
    o_ref[...] = _rms_scaled(x_ref[...], g_ref[...])


def _final_norm(x, g, *, tm=512):
    m, d = x.shape
    return pl.pallas_call(
        _final_norm_kernel,
        out_shape=jax.ShapeDtypeStruct((m, d), F32),
        grid=(m // tm,),
        in_specs=[pl.BlockSpec((tm, d), lambda i: (i, 0)), pl.BlockSpec((1, d), lambda i: (0, 0))],
        out_specs=pl.BlockSpec((tm, d), lambda i: (i, 0)),
        compiler_params=_params("parallel"),
        name="final_norm",
    )(x, g)


def _reorder_w_in(w_in):
    a3 = 3 * A_WIDTH
    xb_end = a3 + B_WIDTH
    qc_end = xb_end + C_WIDTH
    cmp_end = qc_end + 2 * C_KV_WIDTH
    kv_end = qc_end + 6 * C_KV_WIDTH
    gm_start = kv_end + N_GATES
    w_main = jnp.concatenate(
        [w_in[..., gm_start:], w_in[..., a3:xb_end], w_in[..., qc_end:cmp_end],
         w_in[..., :a3], w_in[..., xb_end:qc_end], w_in[..., cmp_end:kv_end]],
        axis=-1).astype(BF16)
    per_group = 3 * C_HEADS_PER_GROUP
    pad = jnp.zeros(w_in.shape[:-1] + (LANES - per_group,), w_in.dtype)
    w_gate = jnp.concatenate(
        [piece for g in range(C_KV_GROUPS)
         for piece in (w_in[..., kv_end + g * per_group:kv_end + (g + 1) * per_group], pad)], axis=-1).astype(BF16)
    col_scale = np.ones((1, N_F32_COLS + N_BF16_COLS), np.float32)
    col_scale[:, N_F32_COLS + BOFF_QA:N_F32_COLS + BOFF_QA + A_WIDTH] = A_HEAD_DIM ** -0.5
    col_scale[:, N_F32_COLS + BOFF_QC:N_F32_COLS + BOFF_QC + C_WIDTH] = C_HEAD_DIM ** -0.5
    return w_main, w_gate, jnp.asarray(col_scale)


def _mixing(x, layer, g_mix, w_main, w_gate, col_scale, pool_w, pool_scale, pe_k, w1_k, w2_k, pe_v, w1_v, w2_v,
            proj_a, proj_b, proj_c, w_out, bsz, s):
    m, d = x.shape
    cf, cb, gates = _in_proj(x, g_mix, w_main, w_gate, col_scale, layer)
    cf3 = cf.reshape(bsz, s, N_F32_COLS)
    cb3 = cb.reshape(bsz, s, N_BF16_COLS)
    gates3 = gates.reshape(bsz, s, C_KV_GROUPS * LANES)

    oa, la = zip(*[_dil_attn(cb3, gi) for gi in range(len(A_GROUPS))])
    zb = _pool_mixer(cf3, pool_w, pool_scale, layer).reshape(m, B_WIDTH)
    kc, vc = _compress(cf3, pe_k, w1_k, w2_k, pe_v, w1_v, w2_v, layer)
    oc = _nsa(cb3, kc, vc, gates3).reshape(m, C_WIDTH)
    return _merge(oa, la, zb, oc, cf, x, proj_a, proj_b, proj_c, w_out, layer)


def kernel(x, ffn1_norm, ffn1_wi, ffn1_wo, mix_norm, w_in, pool_w, pool_scale, cmp_pe_k, cmp_w1_k, cmp_w2_k,
           cmp_pe_v, cmp_w1_v, cmp_w2_v, proj_a, proj_b, proj_c, w_out, ffn2_norm, ffn2_wi, ffn2_wo, final_norm):
    bsz, s, d = x.shape
    depth = ffn1_wi.shape[0]
    for win, dil in A_GROUPS:
        assert win // dil == Q_BLOCK and s % (dil * Q_BLOCK) == 0
    assert CMP_LEN == 2 * CMP_STRIDE and all(w & (w - 1) == 0 for w in B_WINDOWS)

    bf = lambda w: w.astype(BF16)
    row3 = lambda g: g.reshape(g.shape[0], 1, g.shape[-1])
    w_main, w_gate, col_scale = _reorder_w_in(w_in)
    ffn1_wi, ffn1_wo, ffn2_wi, ffn2_wo = bf(ffn1_wi), bf(ffn1_wo), bf(ffn2_wi), bf(ffn2_wo)
    pool_w, cmp_w1_k, cmp_w2_k, cmp_w1_v, cmp_w2_v = bf(pool_w), bf(cmp_w1_k), bf(cmp_w2_k), bf(cmp_w1_v), bf(cmp_w2_v)
    proj_a, proj_b, proj_c, w_out = bf(proj_a), bf(proj_b), bf(proj_c), bf(w_out)
    ffn1_norm, mix_norm, ffn2_norm, pool_scale = row3(ffn1_norm), row3(mix_norm), row3(ffn2_norm), row3(pool_scale)

    x = x.reshape(bsz * s, d)
    for layer in range(depth):
        x = _ffn_down(_ffn_up(x, ffn1_norm, ffn1_wi, layer), ffn1_wo, x, layer)
        x = _mixing(x, layer, mix_norm, w_main, w_gate, col_scale, pool_w, pool_scale, cmp_pe_k, cmp_w1_k, cmp_w2_k,
                    cmp_pe_v, cmp_w1_v, cmp_w2_v, proj_a, proj_b, proj_c, w_out, bsz, s)
        x = _ffn_down(_ffn_up(x, ffn2_norm, ffn2_wi, layer), ffn2_wo, x, layer)
    return _final_norm(x, final_norm.reshape(1, d)).reshape(bsz, s, d)
```

```python
import functools

import numpy as np
import jax
import jax.numpy as jnp
from jax import lax
from jax.experimental import pallas as pl
from jax.experimental.pallas import tpu as pltpu

F32 = jnp.float32
BF16 = jnp.bfloat16

EPS = 1e-6
NEG = -1e30
Q_BLOCK = 128
LANES = 128

A_GROUPS = ((128, 1), (512, 4), (2048, 16))
A_HEADS_PER_GROUP = 4
A_HEAD_DIM = 64
A_GROUP_WIDTH = A_HEADS_PER_GROUP * A_HEAD_DIM
A_WIDTH = A_GROUP_WIDTH * len(A_GROUPS)

B_WINDOWS = (2, 4, 8, 16)
B_GROUP_DIM = 128
B_WIDTH = B_GROUP_DIM * len(B_WINDOWS)

C_KV_GROUPS = 2
C_HEADS_PER_GROUP = 3
C_HEADS = C_KV_GROUPS * C_HEADS_PER_GROUP
C_HEAD_DIM = 128
C_GROUP_WIDTH = C_HEADS_PER_GROUP * C_HEAD_DIM
C_WIDTH = C_HEADS * C_HEAD_DIM
C_KV_WIDTH = C_KV_GROUPS * C_HEAD_DIM
CMP_LEN = 32
CMP_STRIDE = 16
SEL_BLOCK = 64
SEL_TOPN = 8
FORCE_BONUS = 100.0
WIN = 512
N_GATES = 3 * C_HEADS
SEL_CHUNK = 512

N_BRANCH = 3
D_GATE = 2048

FOFF_GM = 0
FOFF_XB = N_BRANCH * D_GATE
FOFF_KCMP = FOFF_XB + B_WIDTH
FOFF_VCMP = FOFF_KCMP + C_KV_WIDTH
N_F32_COLS = FOFF_VCMP + C_KV_WIDTH
BOFF_QA = 0
BOFF_KA = BOFF_QA + A_WIDTH
BOFF_VA = BOFF_KA + A_WIDTH
BOFF_QC = BOFF_VA + A_WIDTH
BOFF_KSLC = BOFF_QC + C_WIDTH
BOFF_VSLC = BOFF_KSLC + C_KV_WIDTH
BOFF_KWIN = BOFF_VSLC + C_KV_WIDTH
BOFF_VWIN = BOFF_KWIN + C_KV_WIDTH
N_BF16_COLS = BOFF_VWIN + C_KV_WIDTH

VMEM_LIMIT = 56 * 1024 * 1024
ROW_TILE = 1024
COL_TILE = 512


def _params(*sem):
    return pltpu.CompilerParams(dimension_semantics=sem, vmem_limit_bytes=VMEM_LIMIT)


def _dot(a, b):
    return jnp.dot(a, b, preferred_element_type=F32)


def _dot_nt(a, b):
    return lax.dot_general(a, b, (((1,), (1,)), ((), ())), preferred_element_type=F32)


def _sigmoid(x):
    return 1.0 / (1.0 + jnp.exp(-x))


def _rms_scaled(x, g):
    return x * lax.rsqrt(jnp.mean(x * x, axis=-1, keepdims=True) + EPS) * g


def _ffn_up_kernel(x_ref, g_ref, wa_ref, wb_ref, h_ref, xn_ref):
    @pl.when(pl.program_id(1) == 0)
    def _():
        xn_ref[...] = _rms_scaled(x_ref[...], g_ref[...]).astype(BF16)

    xn = xn_ref[...]
    a = _dot(xn, wa_ref[...])
    b = _dot(xn, wb_ref[...])
    h_ref[...] = (a * _sigmoid(a) * b).astype(BF16)


def _ffn_up(x, g, wi, layer, *, tm=ROW_TILE, tn=COL_TILE):
    m, d = x.shape
    d_ff = wi.shape[-1] // 2
    nj = d_ff // tn
    return pl.pallas_call(
        _ffn_up_kernel,
        out_shape=jax.ShapeDtypeStruct((m, d_ff), BF16),
        grid=(m // tm, nj),
        in_specs=[
            pl.BlockSpec((tm, d), lambda i, j: (i, 0)),
            pl.BlockSpec((None, 1, d), lambda i, j: (layer, 0, 0)),
            pl.BlockSpec((None, d, tn), lambda i, j: (layer, 0, j)),
            pl.BlockSpec((None, d, tn), lambda i, j: (layer, 0, j + nj)),
        ],
        out_specs=pl.BlockSpec((tm, tn), lambda i, j: (i, j)),
        scratch_shapes=[pltpu.VMEM((tm, d), BF16)],
        compiler_params=_params("parallel", "arbitrary"),
        name="ffn_up",
    )(x, g, wi, wi)


def _ffn_down_kernel(h_ref, w_ref, r_ref, o_ref):
    o_ref[...] = r_ref[...] + 0.5 * _dot(h_ref[...], w_ref[...])


def _ffn_down(h, wo, res, layer, *, tm=ROW_TILE, tn=COL_TILE):
    m, d_ff = h.shape
    d = res.shape[-1]
    return pl.pallas_call(
        _ffn_down_kernel,
        out_shape=jax.ShapeDtypeStruct((m, d), F32),
        grid=(m // tm, d // tn),
        in_specs=[
            pl.BlockSpec((tm, d_ff), lambda i, j: (i, 0)),
            pl.BlockSpec((None, d_ff, tn), lambda i, j: (layer, 0, j)),
            pl.BlockSpec((tm, tn), lambda i, j: (i, j)),
        ],
        out_specs=pl.BlockSpec((tm, tn), lambda i, j: (i, j)),
        compiler_params=_params("parallel", "arbitrary"),
        name="ffn_down",
    )(h, wo, res)


def _in_proj_kernel(x_ref, g_ref, w_ref, wg_ref, sc_ref, of_ref, ob_ref, og_ref, xn_ref, *, n_f32_tiles):
    j = pl.program_id(1)

    @pl.when(j == 0)
    def _():
        xn = _rms_scaled(x_ref[...], g_ref[...]).astype(BF16)
        xn_ref[...] = xn
        og_ref[...] = _dot(xn, wg_ref[...])

    acc = _dot(xn_ref[...], w_ref[...])

    @pl.when(j < n_f32_tiles)
    def _():
        of_ref[...] = acc

    @pl.when(j >= n_f32_tiles)
    def _():
        ob_ref[...] = (acc * sc_ref[...]).astype(BF16)


def _in_proj(x, g, w_main, w_gate, col_scale, layer, *, tm=ROW_TILE, tn=2 * COL_TILE):
    m, d = x.shape
    ng = w_gate.shape[-1]
    nf = N_F32_COLS // tn
    nb = N_BF16_COLS // tn
    return pl.pallas_call(
        functools.partial(_in_proj_kernel, n_f32_tiles=nf),
        out_shape=(jax.ShapeDtypeStruct((m, N_F32_COLS), F32), jax.ShapeDtypeStruct((m, N_BF16_COLS), BF16),
                   jax.ShapeDtypeStruct((m, ng), F32)),
        grid=(m // tm, nf + nb),
        in_specs=[
            pl.BlockSpec((tm, d), lambda i, j: (i, 0)),
            pl.BlockSpec((None, 1, d), lambda i, j: (layer, 0, 0)),
            pl.BlockSpec((None, d, tn), lambda i, j: (layer, 0, j)),
            pl.BlockSpec((None, d, ng), lambda i, j: (layer, 0, 0)),
            pl.BlockSpec((1, tn), lambda i, j: (0, j)),
        ],
        out_specs=(
            pl.BlockSpec((tm, tn), lambda i, j: (i, jnp.minimum(j, nf - 1))),
            pl.BlockSpec((tm, tn), lambda i, j: (i, jnp.maximum(j - nf, 0))),
            pl.BlockSpec((tm, ng), lambda i, j: (i, 0)),
        ),
        scratch_shapes=[pltpu.VMEM((tm, d), BF16)],
        compiler_params=_params("parallel", "arbitrary"),
        name="in_proj",
    )(x, g, w_main, w_gate, col_scale)


def _dil_heads(q, kc, vc, kp, vp, first):
    r = lax.broadcasted_iota(jnp.int32, (Q_BLOCK, Q_BLOCK), 0)
    c = lax.broadcasted_iota(jnp.int32, (Q_BLOCK, Q_BLOCK), 1)
    bias_cur = jnp.where(c <= r, 0.0, NEG)
    if kp is not None:
        bias_prev = jnp.where(c >= r + jnp.where(first, Q_BLOCK, 0), 0.0, NEG)
    head = lax.shift_right_logical(lax.broadcasted_iota(jnp.int32, (Q_BLOCK, A_GROUP_WIDTH), 1),
                                   int(np.log2(A_HEAD_DIM)))
    o_acc = jnp.zeros((Q_BLOCK, A_GROUP_WIDTH), F32)
    l_acc = jnp.zeros((Q_BLOCK, A_GROUP_WIDTH), F32)
    for h in range(A_HEADS_PER_GROUP):
        mine = head == h
        qh = q * jnp.where(mine, 1.0, 0.0).astype(BF16)
        s_c = _dot_nt(qh, kc) + bias_cur
        m = jnp.max(s_c, axis=-1, keepdims=True)
        if kp is not None:
            s_p = _dot_nt(qh, kp) + bias_prev
            m = jnp.maximum(m, jnp.max(s_p, axis=-1, keepdims=True))
        p_c = jnp.exp(s_c - m)
        l = jnp.sum(p_c, axis=-1, keepdims=True)
        pv = _dot(p_c.astype(BF16), vc)
        if kp is not None:
            p_p = jnp.exp(s_p - m)
            l = l + jnp.sum(p_p, axis=-1, keepdims=True)
            pv = pv + _dot(p_p.astype(BF16), vp)
        o_acc = jnp.where(mine, pv / l, o_acc)
        l_acc = jnp.where(mine, m + jnp.log(l), l_acc)
    return o_acc, l_acc


def _dil_attn_dense_kernel(q_ref, kc_ref, kp_ref, vc_ref, vp_ref, o_ref, l_ref):
    o, lse = _dil_heads(q_ref[...], kc_ref[...], vc_ref[...], kp_ref[...], vp_ref[...], pl.program_id(1) == 0)
    for half in range(A_GROUP_WIDTH // LANES):
        o_ref[half] = o[:, half * LANES:(half + 1) * LANES]
        l_ref[half] = lse[:, half * LANES:(half + 1) * LANES]


def _dil_attn_strided_kernel(q_ref, k_ref, v_ref, o_ref, l_ref, qf_ref, kf_ref, vf_ref, *, dil, has_prev):
    i = pl.program_id(1)
    r = pl.program_id(2)
    halves = A_GROUP_WIDTH // LANES
    slot = lax.rem(i, 2) if has_prev else 0

    @pl.when(r == 0)
    def _():
        for half in range(halves):
            cs = slice(half * LANES, (half + 1) * LANES)
            qf_ref[half] = q_ref[:, cs].astype(F32)
            kf_ref[slot, half] = k_ref[:, cs].astype(F32)
            vf_ref[slot, half] = v_ref[:, cs].astype(F32)

    if has_prev:
        @pl.when((r == 0) & (i == 0))
        def _():
            kf_ref[1] = jnp.zeros(kf_ref.shape[1:], F32)
            vf_ref[1] = jnp.zeros(vf_ref.shape[1:], F32)

    def rows_of_class(ref, *lead):
        return jnp.concatenate([ref[(*lead, half, pl.ds(r, Q_BLOCK, stride=dil), slice(None))]
                                for half in range(halves)], axis=1).astype(BF16)

    kp = rows_of_class(kf_ref, 1 - slot) if has_prev else None
    vp = rows_of_class(vf_ref, 1 - slot) if has_prev else None
    o, lse = _dil_heads(rows_of_class(qf_ref), rows_of_class(kf_ref, slot), rows_of_class(vf_ref, slot), kp, vp, i == 0)
    for half in range(halves):
        o_ref[half, pl.ds(r, Q_BLOCK, stride=dil), :] = o[:, half * LANES:(half + 1) * LANES]
        l_ref[half, pl.ds(r, Q_BLOCK, stride=dil), :] = lse[:, half * LANES:(half + 1) * LANES]


def _dil_attn(cb3, gi):
    bsz, s, _ = cb3.shape
    dil = A_GROUPS[gi][1]
    w = A_GROUP_WIDTH
    halves = w // LANES
    out = jax.ShapeDtypeStruct((bsz, halves, s, LANES), F32)
    name = f"dilated_attention_g{gi}"
    if dil == 1:
        tiles = s // Q_BLOCK

        def col(off, prev):
            return pl.BlockSpec((None, Q_BLOCK, w),
                                lambda b, i: (b, jnp.maximum(i - 1, 0) if prev else i, (off + gi * w) // w))

        ospec = pl.BlockSpec((None, halves, Q_BLOCK, LANES), lambda b, i: (b, 0, i, 0))
        return pl.pallas_call(
            _dil_attn_dense_kernel,
            out_shape=(out, out),
            grid=(bsz, tiles),
            in_specs=[col(BOFF_QA, False), col(BOFF_KA, False), col(BOFF_KA, True), col(BOFF_VA, False), col(BOFF_VA, True)],
            out_specs=(ospec, ospec),
            compiler_params=_params("parallel", "arbitrary"),
            name=name,
        )(cb3, cb3, cb3, cb3, cb3)

    rows = Q_BLOCK * dil
    tiles = s // rows
    has_prev = tiles > 1

    def col(off):
        return pl.BlockSpec((None, rows, w), lambda b, i, r: (b, i, (off + gi * w) // w))

    ospec = pl.BlockSpec((None, halves, rows, LANES), lambda b, i, r: (b, 0, i, 0))
    return pl.pallas_call(
        functools.partial(_dil_attn_strided_kernel, dil=dil, has_prev=has_prev),
        out_shape=(out, out),
        grid=(bsz, tiles, dil),
        in_specs=[col(BOFF_QA), col(BOFF_KA), col(BOFF_VA)],
        out_specs=(ospec, ospec),
        scratch_shapes=[pltpu.VMEM((halves, rows, LANES), F32),
                        pltpu.VMEM((2 if has_prev else 1, halves, rows, LANES), F32),
                        pltpu.VMEM((2 if has_prev else 1, halves, rows, LANES), F32)],
        compiler_params=_params("parallel", "arbitrary", "arbitrary"),
        name=name,
    )(cb3, cb3, cb3)


def _pool_kernel(x_ref, w_ref, sc_ref, o_ref):
    s = x_ref.shape[0]
    t = lax.broadcasted_iota(jnp.int32, (s, B_GROUP_DIM), 0)
    for gi, win in enumerate(B_WINDOWS):
        cs = slice(gi * B_GROUP_DIM, (gi + 1) * B_GROUP_DIM)
        x = x_ref[:, cs]
        acc = x
        k = 1
        while k < win:
            acc = acc + jnp.where(t >= k, pltpu.roll(acc, k, axis=0), 0.0)
            k *= 2
        cnt = jnp.minimum(t + 1, win).astype(F32)
        z = acc / cnt - x
        o_ref[:, cs] = _dot(z.astype(BF16), w_ref[gi]) * sc_ref[:, cs]


def _pool_mixer(cf3, pool_w, pool_scale, layer):
    bsz, s, _ = cf3.shape
    ng = len(B_WINDOWS)
    return pl.pallas_call(
        _pool_kernel,
        out_shape=jax.ShapeDtypeStruct((bsz, s, B_WIDTH), F32),
        grid=(bsz,),
        in_specs=[
            pl.BlockSpec((None, s, B_WIDTH), lambda b: (b, 0, FOFF_XB // B_WIDTH)),
            pl.BlockSpec((None, ng, B_GROUP_DIM, B_GROUP_DIM), lambda b: (layer, 0, 0, 0)),
            pl.BlockSpec((None, 1, B_WIDTH), lambda b: (layer, 0, 0)),
        ],
        out_specs=pl.BlockSpec((None, s, B_WIDTH), lambda b: (b, 0, 0)),
        compiler_params=_params("parallel"),
        name="pool_mixer",
    )(cf3, pool_w, pool_scale)


def _gelu_tanh(x):
    return 0.5 * x * (1.0 + jnp.tanh(np.float32(np.sqrt(2.0 / np.pi)) * (x + 0.044715 * (x * x * x))))


def _compress_one(z_ref, pe_ref, w1_ref, w2_ref, o_ref):
    nch = z_ref.shape[0] // CMP_STRIDE
    dh = C_HEAD_DIM
    first = jnp.zeros((nch, dh), F32)
    second = jnp.zeros((nch, dh), F32)
    for p in range(CMP_STRIDE):
        zp = z_ref[pl.ds(p, nch, stride=CMP_STRIDE), :]
        a = (zp + pe_ref[p:p + 1, :]).astype(BF16)
        b = (zp + pe_ref[CMP_STRIDE + p:CMP_STRIDE + p + 1, :]).astype(BF16)
        first = first + _dot(a, w1_ref[p * dh:(p + 1) * dh, :])
        second = second + _dot(b, w1_ref[(CMP_STRIDE + p) * dh:(CMP_STRIDE + p + 1) * dh, :])
    pre = first + pltpu.roll(second, nch - 1, axis=0)
    o_ref[...] = _dot(_gelu_tanh(pre).astype(BF16), w2_ref[...]).astype(BF16)


def _compress_kernel(zk_ref, zv_ref, pek_ref, w1k_ref, w2k_ref, pev_ref, w1v_ref, w2v_ref, kc_ref, vc_ref):
    _compress_one(zk_ref, pek_ref, w1k_ref, w2k_ref, kc_ref)
    _compress_one(zv_ref, pev_ref, w1v_ref, w2v_ref, vc_ref)


def _compress(cf3, pe_k, w1_k, w2_k, pe_v, w1_v, w2_v, layer):
    bsz, s, _ = cf3.shape
    nch = s // CMP_STRIDE
    dh = C_HEAD_DIM
    cdim = CMP_LEN * dh

    def col(off):
        return pl.BlockSpec((None, s, dh), lambda b, g: (b, 0, off // dh + g))

    def wspec(shape):
        return pl.BlockSpec((None,) + shape, lambda b, g: (layer,) + (0,) * len(shape))

    out = jax.ShapeDtypeStruct((bsz, C_KV_GROUPS, nch, dh), BF16)
    ospec = pl.BlockSpec((None, None, nch, dh), lambda b, g: (b, g, 0, 0))
    return pl.pallas_call(
        _compress_kernel,
        out_shape=(out, out),
        grid=(bsz, C_KV_GROUPS),
        in_specs=[col(FOFF_KCMP), col(FOFF_VCMP),
                  wspec((CMP_LEN, dh)), wspec((cdim, dh)), wspec((dh, dh)),
                  wspec((CMP_LEN, dh)), wspec((cdim, dh)), wspec((dh, dh))],
        out_specs=(ospec, ospec),
        compiler_params=_params("parallel", "parallel"),
        name="nsa_compress",
    )(cf3, cf3, pe_k, w1_k, w2_k, pe_v, w1_v, w2_v)


def _split3(x):
    hi = x.astype(BF16)
    r1 = x - hi.astype(F32)
    mid = r1.astype(BF16)
    lo = (r1 - mid.astype(F32)).astype(BF16)
    return hi, mid, lo


def _softmax_pv(s, v):
    m = jnp.max(s, axis=-1, keepdims=True)
    p = jnp.exp(s - m)
    return _dot(p.astype(BF16), v) / jnp.sum(p, axis=-1, keepdims=True)


def _nsa_kernel(q_ref, kc_ref, vc_ref, ks_ref, vs_ref, kw_ref, vw_ref, gt_ref, ovt_ref, ex_ref, o_ref, *, s_len):
    n = pl.program_id(2)
    nb = s_len // SEL_BLOCK
    hg = C_HEADS_PER_GROUP
    dh = C_HEAD_DIM
    t0 = n * Q_BLOCK
    row = t0 + lax.broadcasted_iota(jnp.int32, (Q_BLOCK, LANES), 0)
    lane = lax.broadcasted_iota(jnp.int32, (Q_BLOCK, LANES), 1)

    q3 = jnp.concatenate([q_ref[:, h * dh:(h + 1) * dh] for h in range(hg)], axis=0)
    rows = [slice(h * Q_BLOCK, (h + 1) * Q_BLOCK) for h in range(hg)]
    gt = _sigmoid(gt_ref[...])

    cvalid = lane * CMP_STRIDE + (CMP_LEN - 1) <= row
    s3 = _dot_nt(q3, kc_ref[...])
    vc = vc_ref[...]
    out = []
    psum = jnp.zeros((Q_BLOCK, LANES), F32)
    for h in range(hg):
        s = jnp.where(cvalid, s3[rows[h]], NEG)
        p = jnp.exp(s - jnp.max(s, axis=-1, keepdims=True))
        p = jnp.where(cvalid, p / jnp.sum(p, axis=-1, keepdims=True), 0.0)
        out.append(gt[:, 3 * h:3 * h + 1] * _dot(p.astype(BF16), vc))
        psum = psum + p

    ovt = ovt_ref[...]
    imp_t = sum(_dot_nt(ovt, part) for part in _split3(psum))
    blk = lax.broadcasted_iota(jnp.int32, (nb, Q_BLOCK), 0)
    tb = lax.shift_right_logical(t0 + lax.broadcasted_iota(jnp.int32, (nb, Q_BLOCK), 1), int(np.log2(SEL_BLOCK)))
    forced = (blk == 0) | (blk == tb) | (blk == tb - 1)
    score = jnp.where(blk > tb, -1.0, imp_t + jnp.where(forced, FORCE_BONUS, 0.0))
    blk_f = blk.astype(F32)
    sel_t = jnp.zeros((nb, Q_BLOCK), F32)
    for _ in range(min(SEL_TOPN, nb)):
        mx = jnp.max(score, axis=0, keepdims=True)
        idx = jnp.min(jnp.where(score == mx, blk_f, float(nb)), axis=0, keepdims=True)
        hit = blk_f == idx
        sel_t = jnp.where(hit, 1.0, sel_t)
        score = jnp.where(hit, -3.0, score)
    sel = jnp.concatenate([sel_t, jnp.zeros((LANES - nb, Q_BLOCK), F32)], axis=0).T.astype(BF16)

    wlen = WIN + Q_BLOCK
    w0 = pl.multiple_of(jnp.maximum(t0 - WIN, 0), Q_BLOCK)
    kpos = w0 + lax.broadcasted_iota(jnp.int32, (Q_BLOCK, wlen), 1)
    qpos = t0 + lax.broadcasted_iota(jnp.int32, (Q_BLOCK, wlen), 0)
    bias = jnp.where((kpos <= qpos) & (kpos > qpos - WIN), 0.0, NEG)
    s3 = _dot_nt(q3, kw_ref[pl.ds(w0, wlen), :])
    vw = vw_ref[pl.ds(w0, wlen), :]
    for h in range(hg):
        out[h] = out[h] + gt[:, 3 * h + 2:3 * h + 3] * _softmax_pv(s3[rows[h]] + bias, vw)

    for cls in range(s_len // SEL_CHUNK):
        klen = (cls + 1) * SEL_CHUNK

        @pl.when(n // (SEL_CHUNK // Q_BLOCK) == cls)
        def _(klen=klen):
            kpos = lax.broadcasted_iota(jnp.int32, (Q_BLOCK, klen), 1)
            qpos = t0 + lax.broadcasted_iota(jnp.int32, (Q_BLOCK, klen), 0)
            chosen = _dot(sel, ex_ref[:, :klen])
            bias = jnp.where((chosen > 0.5) & (kpos <= qpos), 0.0, NEG)
            s3 = _dot_nt(q3, ks_ref[:klen, :])
            vs = vs_ref[:klen, :]
            for h in range(hg):
                o_ref[:, h * dh:(h + 1) * dh] = out[h] + gt[:, 3 * h + 1:3 * h + 2] * _softmax_pv(s3[rows[h]] + bias, vs)


def _nsa_constants(s):
    nch = s // CMP_STRIDE
    nb = s // SEL_BLOCK
    n_cmp = (s - CMP_LEN) // CMP_STRIDE + 1
    ci = np.arange(nch)[None, :] * CMP_STRIDE
    bj = np.arange(nb)[:, None] * SEL_BLOCK
    overlap_t = (ci < bj + SEL_BLOCK) & (ci + CMP_LEN > bj) & (np.arange(nch)[None, :] < n_cmp)
    expand = np.arange(LANES)[:, None] == (np.arange(s)[None, :] // SEL_BLOCK)
    return jnp.asarray(overlap_t, BF16), jnp.asarray(expand, BF16)


def _nsa(cb3, kc, vc, gates3):
    bsz, s, _ = cb3.shape
    nq = s // Q_BLOCK
    nch = s // CMP_STRIDE
    nb = s // SEL_BLOCK
    dh = C_HEAD_DIM
    assert nch == LANES and nb <= LANES and nb % 8 == 0 and s % SEL_CHUNK == 0 and s >= WIN + Q_BLOCK
    overlap_t, expand = _nsa_constants(s)

    def kv(off):
        return pl.BlockSpec((None, s, dh), lambda b, g, n: (b, 0, off // dh + g))

    cmp_spec = pl.BlockSpec((None, None, nch, dh), lambda b, g, n: (b, g, 0, 0))
    return pl.pallas_call(
        functools.partial(_nsa_kernel, s_len=s),
        out_shape=jax.ShapeDtypeStruct((bsz, s, C_WIDTH), F32),
        grid=(bsz, C_KV_GROUPS, nq),
        in_specs=[
            pl.BlockSpec((None, Q_BLOCK, C_GROUP_WIDTH), lambda b, g, n: (b, n, BOFF_QC // C_GROUP_WIDTH + g)),
            cmp_spec, cmp_spec,
            kv(BOFF_KSLC), kv(BOFF_VSLC), kv(BOFF_KWIN), kv(BOFF_VWIN),
            pl.BlockSpec((None, Q_BLOCK, LANES), lambda b, g, n: (b, n, g)),
            pl.BlockSpec((nb, nch), lambda b, g, n: (0, 0)),
            pl.BlockSpec((LANES, s), lambda b, g, n: (0, 0)),
        ],
        out_specs=pl.BlockSpec((None, Q_BLOCK, C_GROUP_WIDTH), lambda b, g, n: (b, n, g)),
        compiler_params=_params("parallel", "parallel", "arbitrary"),
        name="nsa_attention",
    )(cb3, kc, vc, cb3, cb3, cb3, cb3, gates3, overlap_t, expand)


def _merge_kernel(*refs):
    ng = len(A_GROUPS)
    oa_refs, la_refs = refs[:ng], refs[ng:2 * ng]
    zb_ref, oc_ref, g0_ref, g1_ref, g2_ref, x_ref, pa_ref, pb_ref, pc_ref, wo_ref, o_ref = refs[2 * ng:]
    def lanes(ref):
        return jnp.concatenate([ref[half] for half in range(ref.shape[0])], axis=1)

    lse = [lanes(r) for r in la_refs]
    m = functools.reduce(jnp.maximum, lse)
    e = [jnp.exp(l - m) for l in lse]
    oa = sum(w * lanes(r) for w, r in zip(e, oa_refs)) / sum(e)
    ya = _dot(oa.astype(BF16), pa_ref[...])
    yb = _dot(zb_ref[...].astype(BF16), pb_ref[...])
    yc = _dot(oc_ref[...].astype(BF16), pc_ref[...])
    mix = _sigmoid(g0_ref[...]) * ya + _sigmoid(g1_ref[...]) * yb + _sigmoid(g2_ref[...]) * yc
    o_ref[...] = x_ref[...] + _dot(mix.astype(BF16), wo_ref[...])


def _merge(oa, la, zb, oc, cf, x, proj_a, proj_b, proj_c, w_out, layer, *, tm=256):
    m, d = x.shape
    assert d == D_GATE

    def rows(w):
        return pl.BlockSpec((tm, w), lambda i: (i, 0))

    def gate(k):
        return pl.BlockSpec((tm, d), lambda i: (i, FOFF_GM // d + k))

    def weight(k):
        return pl.BlockSpec((None, k, d), lambda i: (layer, 0, 0), pipeline_mode=pl.Buffered(1))

    _, halves, s, _ = oa[0].shape
    per_seq = s // tm
    group = pl.BlockSpec((None, halves, tm, LANES), lambda i: (i // per_seq, 0, i % per_seq, 0))

    return pl.pallas_call(
        _merge_kernel,
        out_shape=jax.ShapeDtypeStruct((m, d), F32),
        grid=(m // tm,),
        in_specs=[group] * (2 * len(A_GROUPS))
        + [rows(B_WIDTH), rows(C_WIDTH), gate(0), gate(1), gate(2), rows(d),
           weight(A_GROUP_WIDTH), weight(B_WIDTH), weight(C_WIDTH), weight(d)],
        out_specs=rows(d),
        compiler_params=_params("parallel"),
        name="merge",
    )(*oa, *la, zb, oc, cf, cf, cf, x, proj_a, proj_b, proj_c, w_out)


def _final_norm_kernel(x_ref, g_ref, o_ref):
    o_ref[...] = _rms_scaled(x_ref[...], g_ref[...])


def _final_norm(x, g, *, tm=512):
    m, d = x.shape
    return pl.pallas_call(
        _final_norm_kernel,
        out_shape=jax.ShapeDtypeStruct((m, d), F32),
        grid=(m // tm,),
        in_specs=[pl.BlockSpec((tm, d), lambda i: (i, 0)), pl.BlockSpec((1, d), lambda i: (0, 0))],
        out_specs=pl.BlockSpec((tm, d), lambda i: (i, 0)),
        compiler_params=_params("parallel"),
        name="final_norm",
    )(x, g)


def _reorder_w_in(w_in):
    a3 = 3 * A_WIDTH
    xb_end = a3 + B_WIDTH
    qc_end = xb_end + C_WIDTH
    cmp_end = qc_end + 2 * C_KV_WIDTH
    kv_end = qc_end + 6 * C_KV_WIDTH
    gm_start = kv_end + N_GATES
    w_main = jnp.concatenate(
        [w_in[..., gm_start:], w_in[..., a3:xb_end], w_in[..., qc_end:cmp_end],
         w_in[..., :a3], w_in[..., xb_end:qc_end], w_in[..., cmp_end:kv_end]],
        axis=-1).astype(BF16)
    per_group = 3 * C_HEADS_PER_GROUP
    pad = jnp.zeros(w_in.shape[:-1] + (LANES - per_group,), w_in.dtype)
    w_gate = jnp.concatenate(
        [piece for g in range(C_KV_GROUPS)
         for piece in (w_in[..., kv_end + g * per_group:kv_end + (g + 1) * per_group], pad)], axis=-1).astype(BF16)
    col_scale = np.ones((1, N_F32_COLS + N_BF16_COLS), np.float32)
    col_scale[:, N_F32_COLS + BOFF_QA:N_F32_COLS + BOFF_QA + A_WIDTH] = A_HEAD_DIM ** -0.5
    col_scale[:, N_F32_COLS + BOFF_QC:N_F32_COLS + BOFF_QC + C_WIDTH] = C_HEAD_DIM ** -0.5
    return w_main, w_gate, jnp.asarray(col_scale)


def _mixing(x, layer, g_mix, w_main, w_gate, col_scale, pool_w, pool_scale, pe_k, w1_k, w2_k, pe_v, w1_v, w2_v,
            proj_a, proj_b, proj_c, w_out, bsz, s):
    m, d = x.shape
    cf, cb, gates = _in_proj(x, g_mix, w_main, w_gate, col_scale, layer)
    cf3 = cf.reshape(bsz, s, N_F32_COLS)
    cb3 = cb.reshape(bsz, s, N_BF16_COLS)
    gates3 = gates.reshape(bsz, s, C_KV_GROUPS * LANES)

    oa, la = zip(*[_dil_attn(cb3, gi) for gi in range(len(A_GROUPS))])
    zb = _pool_mixer(cf3, pool_w, pool_scale, layer).reshape(m, B_WIDTH)
    kc, vc = _compress(cf3, pe_k, w1_k, w2_k, pe_v, w1_v, w2_v, layer)
    oc = _nsa(cb3, kc, vc, gates3).reshape(m, C_WIDTH)
    return _merge(oa, la, zb, oc, cf, x, proj_a, proj_b, proj_c, w_out, layer)


def kernel(x, ffn1_norm, ffn1_wi, ffn1_wo, mix_norm, w_in, pool_w, pool_scale, cmp_pe_k, cmp_w1_k, cmp_w2_k,
           cmp_pe_v, cmp_w1_v, cmp_w2_v, proj_a, proj_b, proj_c, w_out, ffn2_norm, ffn2_wi, ffn2_wo, final_norm):
    bsz, s, d = x.shape
    depth = ffn1_wi.shape[0]
    for win, dil in A_GROUPS:
        assert win // dil == Q_BLOCK and s % (dil * Q_BLOCK) == 0
    assert CMP_LEN == 2 * CMP_STRIDE and all(w & (w - 1) == 0 for w in B_WINDOWS)

    bf = lambda w: w.astype(BF16)
    row3 = lambda g: g.reshape(g.shape[0], 1, g.shape[-1])
    w_main, w_gate, col_scale = _reorder_w_in(w_in)
    ffn1_wi, ffn1_wo, ffn2_wi, ffn2_wo = bf(ffn1_wi), bf(ffn1_wo), bf(ffn2_wi), bf(ffn2_wo)
    pool_w, cmp_w1_k, cmp_w2_k, cmp_w1_v, cmp_w2_v = bf(pool_w), bf(cmp_w1_k), bf(cmp_w2_k), bf(cmp_w1_v), bf(cmp_w2_v)
    proj_a, proj_b, proj_c, w_out = bf(proj_a), bf(proj_b), bf(proj_c), bf(w_out)
    ffn1_norm, mix_norm, ffn2_norm, pool_scale = row3(ffn1_norm), row3(mix_norm), row3(ffn2_norm), row3(pool_scale)

    x = x.reshape(bsz * s, d)
    for layer in range(depth):
        x = _ffn_down(_ffn_up(x, ffn1_norm, ffn1_wi, layer), ffn1_wo, x, layer)
        x = _mixing(x, layer, mix_norm, w_main, w_gate, col_scale, pool_w, pool_scale, cmp_pe_k, cmp_w1_k, cmp_w2_k,
                    cmp_pe_v, cmp_w1_v, cmp_w2_v, proj_a, proj_b, proj_c, w_out, bsz, s)
        x = _ffn_down(_ffn_up(x, ffn2_norm, ffn2_wi, layer), ffn2_wo, x, layer)
    return _final_norm(x, final_norm.reshape(1, d)).reshape(bsz, s, d)
```

```python
import functools

import numpy as np
import jax
import jax.numpy as jnp
from jax import lax
from jax.experimental import pallas as pl
from jax.experimental.pallas import tpu as pltpu

F32 = jnp.float32
BF16 = jnp.bfloat16

EPS = 1e-6
NEG = -1e30
Q_BLOCK = 128
LANES = 128

A_GROUPS = ((128, 1), (512, 4), (2048, 16))
A_HEADS_PER_GROUP = 4
A_HEAD_DIM = 64
A_GROUP_WIDTH = A_HEADS_PER_GROUP * A_HEAD_DIM
A_WIDTH = A_GROUP_WIDTH * len(A_GROUPS)

B_WINDOWS = (2, 4, 8, 16)
B_GROUP_DIM = 128
B_WIDTH = B_GROUP_DIM * len(B_WINDOWS)

C_KV_GROUPS = 2
C_HEADS_PER_GROUP = 3
C_HEADS = C_KV_GROUPS * C_HEADS_PER_GROUP
C_HEAD_DIM = 128
C_GROUP_WIDTH = C_HEADS_PER_GROUP * C_HEAD_DIM
C_WIDTH = C_HEADS * C_HEAD_DIM
C_KV_WIDTH = C_KV_GROUPS * C_HEAD_DIM
CMP_LEN = 32
CMP_STRIDE = 16
SEL_BLOCK = 64
SEL_TOPN = 8
FORCE_BONUS = 100.0
WIN = 512
N_GATES = 3 * C_HEADS
SEL_CHUNK = 512

N_BRANCH = 3
D_GATE = 2048

FOFF_GM = 0
FOFF_XB = N_BRANCH * D_GATE
FOFF_KCMP = FOFF_XB + B_WIDTH
FOFF_VCMP = FOFF_KCMP + C_KV_WIDTH
N_F32_COLS = FOFF_VCMP + C_KV_WIDTH
BOFF_QA = 0
BOFF_KA = BOFF_QA + A_WIDTH
BOFF_VA = BOFF_KA + A_WIDTH
BOFF_QC = BOFF_VA + A_WIDTH
BOFF_KSLC = BOFF_QC + C_WIDTH
BOFF_VSLC = BOFF_KSLC + C_KV_WIDTH
BOFF_KWIN = BOFF_VSLC + C_KV_WIDTH
BOFF_VWIN = BOFF_KWIN + C_KV_WIDTH
N_BF16_COLS = BOFF_VWIN + C_KV_WIDTH

VMEM_LIMIT = 56 * 1024 * 1024
ROW_TILE = 1024
COL_TILE = 512
FFN_COL_TILE = 1408


def _params(*sem):
    return pltpu.CompilerParams(dimension_semantics=sem, vmem_limit_bytes=VMEM_LIMIT)


def _dot(a, b):
    return jnp.dot(a, b, preferred_element_type=F32)


def _dot_nt(a, b):
    return lax.dot_general(a, b, (((1,), (1,)), ((), ())), preferred_element_type=F32)


def _sigmoid(x):
    return 1.0 / (1.0 + jnp.exp(-x))


def _rms_scaled(x, g):
    return x * lax.rsqrt(jnp.mean(x * x, axis=-1, keepdims=True) + EPS) * g


def _row_norm_kernel(x_ref, g_ref, o_ref):
    o_ref[...] = _rms_scaled(x_ref[...], g_ref[...]).astype(o_ref.dtype)


def _row_norm(x, g, layer, *, tm=512):
    m, d = x.shape
    return pl.pallas_call(
        _row_norm_kernel,
        out_shape=jax.ShapeDtypeStruct((m, d), BF16),
        grid=(m // tm,),
        in_specs=[pl.BlockSpec((tm, d), lambda i: (i, 0)), pl.BlockSpec((None, 1, d), lambda i: (layer, 0, 0))],
        out_specs=pl.BlockSpec((tm, d), lambda i: (i, 0)),
        compiler_params=_params("parallel"),
        name="row_norm",
    )(x, g)


def _pair_ffn_tiles(wi, tn=FFN_COL_TILE):
    lead, two_dff = wi.shape[:-1], wi.shape[-1]
    nj = two_dff // (2 * tn)
    return wi.reshape(*lead, 2, nj, tn).swapaxes(-3, -2).reshape(*lead, two_dff)


def _ffn_up_kernel(xn_ref, w_ref, h_ref):
    tn = h_ref.shape[-1]
    ab = _dot(xn_ref[...], w_ref[...])
    a, b = ab[:, :tn], ab[:, tn:]
    h_ref[...] = (a * _sigmoid(a) * b).astype(BF16)


def _ffn_up(xn, wi_paired, layer, *, tm=ROW_TILE, tn=FFN_COL_TILE):
    m, d = xn.shape
    d_ff = wi_paired.shape[-1] // 2
    return pl.pallas_call(
        _ffn_up_kernel,
        out_shape=jax.ShapeDtypeStruct((m, d_ff), BF16),
        grid=(m // tm, d_ff // tn),
        in_specs=[
            pl.BlockSpec((tm, d), lambda i, j: (i, 0)),
            pl.BlockSpec((None, d, 2 * tn), lambda i, j: (layer, 0, j)),
        ],
        out_specs=pl.BlockSpec((tm, tn), lambda i, j: (i, j)),
        compiler_params=_params("parallel", "arbitrary"),
        name="ffn_up",
    )(xn, wi_paired)


def _ffn_down_kernel(h_ref, w_ref, r_ref, g_ref, *o_refs, last):
    y = r_ref[...] + 0.5 * _dot(h_ref[...], w_ref[...])
    if last:
        o_refs[0][...] = _rms_scaled(y, g_ref[...])
    else:
        o_refs[0][...] = y
        o_refs[1][...] = _rms_scaled(y, g_ref[...]).astype(BF16)


def _ffn_down(h, wo, res, g_next, layer, g_layer, *, last=False, tm=256):
    m, d_ff = h.shape
    d = res.shape[-1]
    rows = pl.BlockSpec((tm, d), lambda i: (i, 0))
    x_out = jax.ShapeDtypeStruct((m, d), F32)
    return pl.pallas_call(
        functools.partial(_ffn_down_kernel, last=last),
        out_shape=x_out if last else (x_out, jax.ShapeDtypeStruct((m, d), BF16)),
        grid=(m // tm,),
        in_specs=[
            pl.BlockSpec((tm, d_ff), lambda i: (i, 0)),
            pl.BlockSpec((None, d_ff, d), lambda i: (layer, 0, 0), pipeline_mode=pl.Buffered(1)),
            rows,
            pl.BlockSpec((None, 1, d), lambda i: (g_layer, 0, 0)),
        ],
        out_specs=rows if last else (rows, rows),
        compiler_params=_params("parallel"),
        name="ffn_down",
    )(h, wo, res, g_next)


def _in_proj_kernel(xn_ref, w_ref, wg_ref, sc_ref, of_ref, ob_ref, og_ref, *, n_f32_tiles):
    j = pl.program_id(1)

    @pl.when(j == 0)
    def _():
        og_ref[...] = _dot(xn_ref[...], wg_ref[...])

    acc = _dot(xn_ref[...], w_ref[...])

    @pl.when(j < n_f32_tiles)
    def _():
        of_ref[...] = acc

    @pl.when(j >= n_f32_tiles)
    def _():
        ob_ref[...] = (acc * sc_ref[...]).astype(BF16)


def _in_proj(xn, w_main, w_gate, col_scale, layer, *, tm=ROW_TILE, tn=2 * COL_TILE):
    m, d = xn.shape
    ng = w_gate.shape[-1]
    nf = N_F32_COLS // tn
    nb = N_BF16_COLS // tn
    return pl.pallas_call(
        functools.partial(_in_proj_kernel, n_f32_tiles=nf),
        out_shape=(jax.ShapeDtypeStruct((m, N_F32_COLS), F32), jax.ShapeDtypeStruct((m, N_BF16_COLS), BF16),
                   jax.ShapeDtypeStruct((m, ng), F32)),
        grid=(m // tm, nf + nb),
        in_specs=[
            pl.BlockSpec((tm, d), lambda i, j: (i, 0)),
            pl.BlockSpec((None, d, tn), lambda i, j: (layer, 0, j)),
            pl.BlockSpec((None, d, ng), lambda i, j: (layer, 0, 0)),
            pl.BlockSpec((1, tn), lambda i, j: (0, j)),
        ],
        out_specs=(
            pl.BlockSpec((tm, tn), lambda i, j: (i, jnp.minimum(j, nf - 1))),
            pl.BlockSpec((tm, tn), lambda i, j: (i, jnp.maximum(j - nf, 0))),
            pl.BlockSpec((tm, ng), lambda i, j: (i, 0)),
        ),
        compiler_params=_params("parallel", "arbitrary"),
        name="in_proj",
    )(xn, w_main, w_gate, col_scale)


def _dil_heads(q, kc, vc, kp, vp, first):
    r = lax.broadcasted_iota(jnp.int32, (Q_BLOCK, Q_BLOCK), 0)
    c = lax.broadcasted_iota(jnp.int32, (Q_BLOCK, Q_BLOCK), 1)
    bias_cur = jnp.where(c <= r, 0.0, NEG)
    if kp is not None:
        bias_prev = jnp.where(c >= r + jnp.where(first, Q_BLOCK, 0), 0.0, NEG)
    head = lax.shift_right_logical(lax.broadcasted_iota(jnp.int32, (Q_BLOCK, A_GROUP_WIDTH), 1),
                                   int(np.log2(A_HEAD_DIM)))
    o_acc = jnp.zeros((Q_BLOCK, A_GROUP_WIDTH), F32)
    l_acc = jnp.zeros((Q_BLOCK, A_GROUP_WIDTH), F32)
    for h in range(A_HEADS_PER_GROUP):
        mine = head == h
        qh = q * jnp.where(mine, 1.0, 0.0).astype(BF16)
        s_c = _dot_nt(qh, kc) + bias_cur
        m = jnp.max(s_c, axis=-1, keepdims=True)
        if kp is not None:
            s_p = _dot_nt(qh, kp) + bias_prev
            m = jnp.maximum(m, jnp.max(s_p, axis=-1, keepdims=True))
        p_c = jnp.exp(s_c - m)
        l = jnp.sum(p_c, axis=-1, keepdims=True)
        pv = _dot(p_c.astype(BF16), vc)
        if kp is not None:
            p_p = jnp.exp(s_p - m)
            l = l + jnp.sum(p_p, axis=-1, keepdims=True)
            pv = pv + _dot(p_p.astype(BF16), vp)
        o_acc = jnp.where(mine, pv / l, o_acc)
        l_acc = jnp.where(mine, m + jnp.log(l), l_acc)
    return o_acc, l_acc


def _dil_attn_dense_kernel(q_ref, kc_ref, kp_ref, vc_ref, vp_ref, o_ref, l_ref):
    o, lse = _dil_heads(q_ref[...], kc_ref[...], vc_ref[...], kp_ref[...], vp_ref[...], pl.program_id(1) == 0)
    for half in range(A_GROUP_WIDTH // LANES):
        o_ref[half] = o[:, half * LANES:(half + 1) * LANES]
        l_ref[half] = lse[:, half * LANES:(half + 1) * LANES]


def _dil_attn_strided_kernel(q_ref, k_ref, v_ref, o_ref, l_ref, qf_ref, kf_ref, vf_ref, *, dil, has_prev):
    i = pl.program_id(1)
    r = pl.program_id(2)
    halves = A_GROUP_WIDTH // LANES
    slot = lax.rem(i, 2) if has_prev else 0

    @pl.when(r == 0)
    def _():
        for half in range(halves):
            cs = slice(half * LANES, (half + 1) * LANES)
            qf_ref[half] = q_ref[:, cs].astype(F32)
            kf_ref[slot, half] = k_ref[:, cs].astype(F32)
            vf_ref[slot, half] = v_ref[:, cs].astype(F32)

    if has_prev:
        @pl.when((r == 0) & (i == 0))
        def _():
            kf_ref[1] = jnp.zeros(kf_ref.shape[1:], F32)
            vf_ref[1] = jnp.zeros(vf_ref.shape[1:], F32)

    def rows_of_class(ref, *lead):
        return jnp.concatenate([ref[(*lead, half, pl.ds(r, Q_BLOCK, stride=dil), slice(None))]
                                for half in range(halves)], axis=1).astype(BF16)

    kp = rows_of_class(kf_ref, 1 - slot) if has_prev else None
    vp = rows_of_class(vf_ref, 1 - slot) if has_prev else None
    o, lse = _dil_heads(rows_of_class(qf_ref), rows_of_class(kf_ref, slot), rows_of_class(vf_ref, slot), kp, vp, i == 0)
    for half in range(halves):
        o_ref[half, pl.ds(r, Q_BLOCK, stride=dil), :] = o[:, half * LANES:(half + 1) * LANES]
        l_ref[half, pl.ds(r, Q_BLOCK, stride=dil), :] = lse[:, half * LANES:(half + 1) * LANES]


def _dil_attn(cb3, gi):
    bsz, s, _ = cb3.shape
    dil = A_GROUPS[gi][1]
    w = A_GROUP_WIDTH
    halves = w // LANES
    out = jax.ShapeDtypeStruct((bsz, halves, s, LANES), F32)
    name = f"dilated_attention_g{gi}"
    if dil == 1:
        tiles = s // Q_BLOCK

        def col(off, prev):
            return pl.BlockSpec((None, Q_BLOCK, w),
                                lambda b, i: (b, jnp.maximum(i - 1, 0) if prev else i, (off + gi * w) // w))

        ospec = pl.BlockSpec((None, halves, Q_BLOCK, LANES), lambda b, i: (b, 0, i, 0))
        return pl.pallas_call(
            _dil_attn_dense_kernel,
            out_shape=(out, out),
            grid=(bsz, tiles),
            in_specs=[col(BOFF_QA, False), col(BOFF_KA, False), col(BOFF_KA, True), col(BOFF_VA, False), col(BOFF_VA, True)],
            out_specs=(ospec, ospec),
            compiler_params=_params("parallel", "arbitrary"),
            name=name,
        )(cb3, cb3, cb3, cb3, cb3)

    rows = Q_BLOCK * dil
    tiles = s // rows
    has_prev = tiles > 1

    def col(off):
        return pl.BlockSpec((None, rows, w), lambda b, i, r: (b, i, (off + gi * w) // w))

    ospec = pl.BlockSpec((None, halves, rows, LANES), lambda b, i, r: (b, 0, i, 0))
    return pl.pallas_call(
        functools.partial(_dil_attn_strided_kernel, dil=dil, has_prev=has_prev),
        out_shape=(out, out),
        grid=(bsz, tiles, dil),
        in_specs=[col(BOFF_QA), col(BOFF_KA), col(BOFF_VA)],
        out_specs=(ospec, ospec),
        scratch_shapes=[pltpu.VMEM((halves, rows, LANES), F32),
                        pltpu.VMEM((2 if has_prev else 1, halves, rows, LANES), F32),
                        pltpu.VMEM((2 if has_prev else 1, halves, rows, LANES), F32)],
        compiler_params=_params("parallel", "arbitrary", "arbitrary"),
        name=name,
    )(cb3, cb3, cb3)


def _pool_kernel(x_ref, w_ref, sc_ref, o_ref):
    s = x_ref.shape[0]
    t = lax.broadcasted_iota(jnp.int32, (s, B_GROUP_DIM), 0)
    for gi, win in enumerate(B_WINDOWS):
        cs = slice(gi * B_GROUP_DIM, (gi + 1) * B_GROUP_DIM)
        x = x_ref[:, cs]
        acc = x
        k = 1
        while k < win:
            acc = acc + jnp.where(t >= k, pltpu.roll(acc, k, axis=0), 0.0)
            k *= 2
        cnt = jnp.minimum(t + 1, win).astype(F32)
        z = acc / cnt - x
        o_ref[:, cs] = _dot(z.astype(BF16), w_ref[gi]) * sc_ref[:, cs]


def _pool_mixer(cf3, pool_w, pool_scale, layer):
    bsz, s, _ = cf3.shape
    ng = len(B_WINDOWS)
    return pl.pallas_call(
        _pool_kernel,
        out_shape=jax.ShapeDtypeStruct((bsz, s, B_WIDTH), F32),
        grid=(bsz,),
        in_specs=[
            pl.BlockSpec((None, s, B_WIDTH), lambda b: (b, 0, FOFF_XB // B_WIDTH)),
            pl.BlockSpec((None, ng, B_GROUP_DIM, B_GROUP_DIM), lambda b: (layer, 0, 0, 0)),
            pl.BlockSpec((None, 1, B_WIDTH), lambda b: (layer, 0, 0)),
        ],
        out_specs=pl.BlockSpec((None, s, B_WIDTH), lambda b: (b, 0, 0)),
        compiler_params=_params("parallel"),
        name="pool_mixer",
    )(cf3, pool_w, pool_scale)


def _gelu_tanh(x):
    return 0.5 * x * (1.0 + jnp.tanh(np.float32(np.sqrt(2.0 / np.pi)) * (x + 0.044715 * (x * x * x))))


def _compress_one(z_ref, pe_ref, w1_ref, w2_ref, o_ref):
    nch = z_ref.shape[0] // CMP_STRIDE
    dh = C_HEAD_DIM
    first = jnp.zeros((nch, dh), F32)
    second = jnp.zeros((nch, dh), F32)
    for p in range(CMP_STRIDE):
        zp = z_ref[pl.ds(p, nch, stride=CMP_STRIDE), :]
        a = (zp + pe_ref[p:p + 1, :]).astype(BF16)
        b = (zp + pe_ref[CMP_STRIDE + p:CMP_STRIDE + p + 1, :]).astype(BF16)
        first = first + _dot(a, w1_ref[p * dh:(p + 1) * dh, :])
        second = second + _dot(b, w1_ref[(CMP_STRIDE + p) * dh:(CMP_STRIDE + p + 1) * dh, :])
    pre = first + pltpu.roll(second, nch - 1, axis=0)
    o_ref[...] = _dot(_gelu_tanh(pre).astype(BF16), w2_ref[...]).astype(BF16)


def _compress_kernel(zk_ref, zv_ref, pek_ref, w1k_ref, w2k_ref, pev_ref, w1v_ref, w2v_ref, kc_ref, vc_ref):
    _compress_one(zk_ref, pek_ref, w1k_ref, w2k_ref, kc_ref)
    _compress_one(zv_ref, pev_ref, w1v_ref, w2v_ref, vc_ref)


def _compress(cf3, pe_k, w1_k, w2_k, pe_v, w1_v, w2_v, layer):
    bsz, s, _ = cf3.shape
    nch = s // CMP_STRIDE
    dh = C_HEAD_DIM
    cdim = CMP_LEN * dh

    def col(off):
        return pl.BlockSpec((None, s, dh), lambda b, g: (b, 0, off // dh + g))

    def wspec(shape):
        return pl.BlockSpec((None,) + shape, lambda b, g: (layer,) + (0,) * len(shape))

    out = jax.ShapeDtypeStruct((bsz, C_KV_GROUPS, nch, dh), BF16)
    ospec = pl.BlockSpec((None, None, nch, dh), lambda b, g: (b, g, 0, 0))
    return pl.pallas_call(
        _compress_kernel,
        out_shape=(out, out),
        grid=(bsz, C_KV_GROUPS),
        in_specs=[col(FOFF_KCMP), col(FOFF_VCMP),
                  wspec((CMP_LEN, dh)), wspec((cdim, dh)), wspec((dh, dh)),
                  wspec((CMP_LEN, dh)), wspec((cdim, dh)), wspec((dh, dh))],
        out_specs=(ospec, ospec),
        compiler_params=_params("parallel", "parallel"),
        name="nsa_compress",
    )(cf3, cf3, pe_k, w1_k, w2_k, pe_v, w1_v, w2_v)


def _split3(x):
    hi = x.astype(BF16)
    r1 = x - hi.astype(F32)
    mid = r1.astype(BF16)
    lo = (r1 - mid.astype(F32)).astype(BF16)
    return hi, mid, lo


def _softmax_pv(s, v):
    m = jnp.max(s, axis=-1, keepdims=True)
    p = jnp.exp(s - m)
    return _dot(p.astype(BF16), v) / jnp.sum(p, axis=-1, keepdims=True)


def _nsa_kernel(q_ref, kc_ref, vc_ref, ks_ref, vs_ref, kw_ref, vw_ref, gt_ref, ovt_ref, ex_ref, o_ref, *, s_len):
    n = pl.program_id(2)
    nb = s_len // SEL_BLOCK
    hg = C_HEADS_PER_GROUP
    dh = C_HEAD_DIM
    t0 = n * Q_BLOCK
    row = t0 + lax.broadcasted_iota(jnp.int32, (Q_BLOCK, LANES), 0)
    lane = lax.broadcasted_iota(jnp.int32, (Q_BLOCK, LANES), 1)

    q3 = jnp.concatenate([q_ref[:, h * dh:(h + 1) * dh] for h in range(hg)], axis=0)
    rows = [slice(h * Q_BLOCK, (h + 1) * Q_BLOCK) for h in range(hg)]
    gt = _sigmoid(gt_ref[...])

    cvalid = lane * CMP_STRIDE + (CMP_LEN - 1) <= row
    s3 = _dot_nt(q3, kc_ref[...])
    vc = vc_ref[...]
    out = []
    psum = jnp.zeros((Q_BLOCK, LANES), F32)
    for h in range(hg):
        s = jnp.where(cvalid, s3[rows[h]], NEG)
        p = jnp.exp(s - jnp.max(s, axis=-1, keepdims=True))
        p = jnp.where(cvalid, p / jnp.sum(p, axis=-1, keepdims=True), 0.0)
        out.append(gt[:, 3 * h:3 * h + 1] * _dot(p.astype(BF16), vc))
        psum = psum + p

    ovt = ovt_ref[...]
    imp_t = sum(_dot_nt(ovt, part) for part in _split3(psum))
    blk = lax.broadcasted_iota(jnp.int32, (nb, Q_BLOCK), 0)
    tb = lax.shift_right_logical(t0 + lax.broadcasted_iota(jnp.int32, (nb, Q_BLOCK), 1), int(np.log2(SEL_BLOCK)))
    forced = (blk == 0) | (blk == tb) | (blk == tb - 1)
    score = jnp.where(blk > tb, -1.0, imp_t + jnp.where(forced, FORCE_BONUS, 0.0))
    blk_f = blk.astype(F32)
    sel_t = jnp.zeros((nb, Q_BLOCK), F32)
    for _ in range(min(SEL_TOPN, nb)):
        mx = jnp.max(score, axis=0, keepdims=True)
        idx = jnp.min(jnp.where(score == mx, blk_f, float(nb)), axis=0, keepdims=True)
        hit = blk_f == idx
        sel_t = jnp.where(hit, 1.0, sel_t)
        score = jnp.where(hit, -3.0, score)
    sel = jnp.concatenate([sel_t, jnp.zeros((LANES - nb, Q_BLOCK), F32)], axis=0).T.astype(BF16)

    wlen = WIN + Q_BLOCK
    w0 = pl.multiple_of(jnp.maximum(t0 - WIN, 0), Q_BLOCK)
    kpos = w0 + lax.broadcasted_iota(jnp.int32, (Q_BLOCK, wlen), 1)
    qpos = t0 + lax.broadcasted_iota(jnp.int32, (Q_BLOCK, wlen), 0)
    bias = jnp.where((kpos <= qpos) & (kpos > qpos - WIN), 0.0, NEG)
    s3 = _dot_nt(q3, kw_ref[pl.ds(w0, wlen), :])
    vw = vw_ref[pl.ds(w0, wlen), :]
    for h in range(hg):
        out[h] = out[h] + gt[:, 3 * h + 2:3 * h + 3] * _softmax_pv(s3[rows[h]] + bias, vw)

    for cls in range(s_len // SEL_CHUNK):
        klen = (cls + 1) * SEL_CHUNK

        @pl.when(n // (SEL_CHUNK // Q_BLOCK) == cls)
        def _(klen=klen):
            kpos = lax.broadcasted_iota(jnp.int32, (Q_BLOCK, klen), 1)
            qpos = t0 + lax.broadcasted_iota(jnp.int32, (Q_BLOCK, klen), 0)
            chosen = _dot(sel, ex_ref[:, :klen])
            bias = jnp.where((chosen > 0.5) & (kpos <= qpos), 0.0, NEG)
            s3 = _dot_nt(q3, ks_ref[:klen, :])
            vs = vs_ref[:klen, :]
            for h in range(hg):
                o_ref[:, h * dh:(h + 1) * dh] = out[h] + gt[:, 3 * h + 1:3 * h + 2] * _softmax_pv(s3[rows[h]] + bias, vs)


def _nsa_constants(s):
    nch = s // CMP_STRIDE
    nb = s // SEL_BLOCK
    n_cmp = (s - CMP_LEN) // CMP_STRIDE + 1
    ci = np.arange(nch)[None, :] * CMP_STRIDE
    bj = np.arange(nb)[:, None] * SEL_BLOCK
    overlap_t = (ci < bj + SEL_BLOCK) & (ci + CMP_LEN > bj) & (np.arange(nch)[None, :] < n_cmp)
    expand = np.arange(LANES)[:, None] == (np.arange(s)[None, :] // SEL_BLOCK)
    return jnp.asarray(overlap_t, BF16), jnp.asarray(expand, BF16)


def _nsa(cb3, kc, vc, gates3):
    bsz, s, _ = cb3.shape
    nq = s // Q_BLOCK
    nch = s // CMP_STRIDE
    nb = s // SEL_BLOCK
    dh = C_HEAD_DIM
    assert nch == LANES and nb <= LANES and nb % 8 == 0 and s % SEL_CHUNK == 0 and s >= WIN + Q_BLOCK
    overlap_t, expand = _nsa_constants(s)

    def kv(off):
        return pl.BlockSpec((None, s, dh), lambda b, g, n: (b, 0, off // dh + g))

    cmp_spec = pl.BlockSpec((None, None, nch, dh), lambda b, g, n: (b, g, 0, 0))
    return pl.pallas_call(
        functools.partial(_nsa_kernel, s_len=s),
        out_shape=jax.ShapeDtypeStruct((bsz, s, C_WIDTH), F32),
        grid=(bsz, C_KV_GROUPS, nq),
        in_specs=[
            pl.BlockSpec((None, Q_BLOCK, C_GROUP_WIDTH), lambda b, g, n: (b, n, BOFF_QC // C_GROUP_WIDTH + g)),
            cmp_spec, cmp_spec,
            kv(BOFF_KSLC), kv(BOFF_VSLC), kv(BOFF_KWIN), kv(BOFF_VWIN),
            pl.BlockSpec((None, Q_BLOCK, LANES), lambda b, g, n: (b, n, g)),
            pl.BlockSpec((nb, nch), lambda b, g, n: (0, 0)),
            pl.BlockSpec((LANES, s), lambda b, g, n: (0, 0)),
        ],
        out_specs=pl.BlockSpec((None, Q_BLOCK, C_GROUP_WIDTH), lambda b, g, n: (b, n, g)),
        compiler_params=_params("parallel", "parallel", "arbitrary"),
        name="nsa_attention",
    )(cb3, kc, vc, cb3, cb3, cb3, cb3, gates3, overlap_t, expand)


def _merge_kernel(*refs):
    ng = len(A_GROUPS)
    oa_refs, la_refs = refs[:ng], refs[ng:2 * ng]
    zb_ref, oc_ref, g0_ref, g1_ref, g2_ref, x_ref, pa_ref, pb_ref, pc_ref, wo_ref, gn_ref, o_ref, xn_ref = refs[2 * ng:]

    def lanes(ref):
        return jnp.concatenate([ref[half] for half in range(ref.shape[0])], axis=1)

    lse = [lanes(r) for r in la_refs]
    m = functools.reduce(jnp.maximum, lse)
    e = [jnp.exp(l - m) for l in lse]
    oa = sum(w * lanes(r) for w, r in zip(e, oa_refs)) / sum(e)
    ya = _dot(oa.astype(BF16), pa_ref[...])
    yb = _dot(zb_ref[...].astype(BF16), pb_ref[...])
    yc = _dot(oc_ref[...].astype(BF16), pc_ref[...])
    mix = _sigmoid(g0_ref[...]) * ya + _sigmoid(g1_ref[...]) * yb + _sigmoid(g2_ref[...]) * yc
    y = x_ref[...] + _dot(mix.astype(BF16), wo_ref[...])
    o_ref[...] = y
    xn_ref[...] = _rms_scaled(y, gn_ref[...]).astype(BF16)


def _merge(oa, la, zb, oc, cf, x, proj_a, proj_b, proj_c, w_out, g_next, layer, *, tm=256):
    m, d = x.shape
    assert d == D_GATE

    def rows(w):
        return pl.BlockSpec((tm, w), lambda i: (i, 0))

    def gate(k):
        return pl.BlockSpec((tm, d), lambda i: (i, FOFF_GM // d + k))

    def weight(k):
        return pl.BlockSpec((None, k, d), lambda i: (layer, 0, 0), pipeline_mode=pl.Buffered(1))

    _, halves, s, _ = oa[0].shape
    per_seq = s // tm
    group = pl.BlockSpec((None, halves, tm, LANES), lambda i: (i // per_seq, 0, i % per_seq, 0))

    return pl.pallas_call(
        _merge_kernel,
        out_shape=(jax.ShapeDtypeStruct((m, d), F32), jax.ShapeDtypeStruct((m, d), BF16)),
        grid=(m // tm,),
        in_specs=[group] * (2 * len(A_GROUPS))
        + [rows(B_WIDTH), rows(C_WIDTH), gate(0), gate(1), gate(2), rows(d),
           weight(A_GROUP_WIDTH), weight(B_WIDTH), weight(C_WIDTH), weight(d),
           pl.BlockSpec((None, 1, d), lambda i: (layer, 0, 0))],
        out_specs=(rows(d), rows(d)),
        compiler_params=_params("parallel"),
        name="merge",
    )(*oa, *la, zb, oc, cf, cf, cf, x, proj_a, proj_b, proj_c, w_out, g_next)


def _reorder_w_in(w_in):
    a3 = 3 * A_WIDTH
    xb_end = a3 + B_WIDTH
    qc_end = xb_end + C_WIDTH
    cmp_end = qc_end + 2 * C_KV_WIDTH
    kv_end = qc_end + 6 * C_KV_WIDTH
    gm_start = kv_end + N_GATES
    w_main = jnp.concatenate(
        [w_in[..., gm_start:], w_in[..., a3:xb_end], w_in[..., qc_end:cmp_end],
         w_in[..., :a3], w_in[..., xb_end:qc_end], w_in[..., cmp_end:kv_end]],
        axis=-1).astype(BF16)
    per_group = 3 * C_HEADS_PER_GROUP
    pad = jnp.zeros(w_in.shape[:-1] + (LANES - per_group,), w_in.dtype)
    w_gate = jnp.concatenate(
        [piece for g in range(C_KV_GROUPS)
         for piece in (w_in[..., kv_end + g * per_group:kv_end + (g + 1) * per_group], pad)], axis=-1).astype(BF16)
    col_scale = np.ones((1, N_F32_COLS + N_BF16_COLS), np.float32)
    col_scale[:, N_F32_COLS + BOFF_QA:N_F32_COLS + BOFF_QA + A_WIDTH] = A_HEAD_DIM ** -0.5
    col_scale[:, N_F32_COLS + BOFF_QC:N_F32_COLS + BOFF_QC + C_WIDTH] = C_HEAD_DIM ** -0.5
    return w_main, w_gate, jnp.asarray(col_scale)


def _mixing(x, xn, layer, w_main, w_gate, col_scale, pool_w, pool_scale, pe_k, w1_k, w2_k, pe_v, w1_v, w2_v,
            proj_a, proj_b, proj_c, w_out, g_next, bsz, s):
    m, d = x.shape
    cf, cb, gates = _in_proj(xn, w_main, w_gate, col_scale, layer)
    cf3 = cf.reshape(bsz, s, N_F32_COLS)
    cb3 = cb.reshape(bsz, s, N_BF16_COLS)
    gates3 = gates.reshape(bsz, s, C_KV_GROUPS * LANES)

    oa, la = zip(*[_dil_attn(cb3, gi) for gi in range(len(A_GROUPS))])
    zb = _pool_mixer(cf3, pool_w, pool_scale, layer).reshape(m, B_WIDTH)
    kc, vc = _compress(cf3, pe_k, w1_k, w2_k, pe_v, w1_v, w2_v, layer)
    oc = _nsa(cb3, kc, vc, gates3).reshape(m, C_WIDTH)
    return _merge(oa, la, zb, oc, cf, x, proj_a, proj_b, proj_c, w_out, g_next, layer)


def kernel(x, ffn1_norm, ffn1_wi, ffn1_wo, mix_norm, w_in, pool_w, pool_scale, cmp_pe_k, cmp_w1_k, cmp_w2_k,
           cmp_pe_v, cmp_w1_v, cmp_w2_v, proj_a, proj_b, proj_c, w_out, ffn2_norm, ffn2_wi, ffn2_wo, final_norm):
    bsz, s, d = x.shape
    depth = ffn1_wi.shape[0]
    for win, dil in A_GROUPS:
        assert win // dil == Q_BLOCK and s % (dil * Q_BLOCK) == 0
    assert CMP_LEN == 2 * CMP_STRIDE and all(w & (w - 1) == 0 for w in B_WINDOWS)

    bf = lambda w: w.astype(BF16)
    row3 = lambda g: g.reshape(g.shape[0], 1, g.shape[-1])
    w_main, w_gate, col_scale = _reorder_w_in(w_in)
    ffn1_wi, ffn2_wi = _pair_ffn_tiles(bf(ffn1_wi)), _pair_ffn_tiles(bf(ffn2_wi))
    ffn1_wo, ffn2_wo = bf(ffn1_wo), bf(ffn2_wo)
    pool_w, cmp_w1_k, cmp_w2_k, cmp_w1_v, cmp_w2_v = bf(pool_w), bf(cmp_w1_k), bf(cmp_w2_k), bf(cmp_w1_v), bf(cmp_w2_v)
    proj_a, proj_b, proj_c, w_out = bf(proj_a), bf(proj_b), bf(proj_c), bf(w_out)
    ffn1_norm, mix_norm, ffn2_norm, pool_scale = row3(ffn1_norm), row3(mix_norm), row3(ffn2_norm), row3(pool_scale)

    final_norm = final_norm.reshape(1, 1, d)
    x = x.reshape(bsz * s, d)
    xn = _row_norm(x, ffn1_norm, 0)
    for layer in range(depth):
        x, xn = _ffn_down(_ffn_up(xn, ffn1_wi, layer), ffn1_wo, x, mix_norm, layer, layer)
        x, xn = _mixing(x, xn, layer, w_main, w_gate, col_scale, pool_w, pool_scale, cmp_pe_k, cmp_w1_k, cmp_w2_k,
                        cmp_pe_v, cmp_w1_v, cmp_w2_v, proj_a, proj_b, proj_c, w_out, ffn2_norm, bsz, s)
        h = _ffn_up(xn, ffn2_wi, layer)
        if layer + 1 < depth:
            x, xn = _ffn_down(h, ffn2_wo, x, ffn1_norm, layer, layer + 1)
        else:
            x = _ffn_down(h, ffn2_wo, x, final_norm, layer, 0, last=True)
    return x.reshape(bsz, s, d)
```

```python
import functools

import numpy as np
import jax
import jax.numpy as jnp
from jax import lax
from jax.experimental import pallas as pl
from jax.experimental.pallas import tpu as pltpu

F32 = jnp.float32
BF16 = jnp.bfloat16

EPS = 1e-6
NEG = -1e30
Q_BLOCK = 128
LANES = 128

A_GROUPS = ((128, 1), (512, 4), (2048, 16))
A_HEADS_PER_GROUP = 4
A_HEAD_DIM = 64
A_GROUP_WIDTH = A_HEADS_PER_GROUP * A_HEAD_DIM
A_WIDTH = A_GROUP_WIDTH * len(A_GROUPS)

B_WINDOWS = (2, 4, 8, 16)
B_GROUP_DIM = 128
B_WIDTH = B_GROUP_DIM * len(B_WINDOWS)

C_KV_GROUPS = 2
C_HEADS_PER_GROUP = 3
C_HEADS = C_KV_GROUPS * C_HEADS_PER_GROUP
C_HEAD_DIM = 128
C_GROUP_WIDTH = C_HEADS_PER_GROUP * C_HEAD_DIM
C_WIDTH = C_HEADS * C_HEAD_DIM
C_KV_WIDTH = C_KV_GROUPS * C_HEAD_DIM
CMP_LEN = 32
CMP_STRIDE = 16
SEL_BLOCK = 64
SEL_TOPN = 8
FORCE_BONUS = 100.0
WIN = 512
N_GATES = 3 * C_HEADS
SEL_CHUNK = 512

N_BRANCH = 3
D_GATE = 2048

FOFF_GM = 0
FOFF_XB = N_BRANCH * D_GATE
FOFF_KCMP = FOFF_XB + B_WIDTH
FOFF_VCMP = FOFF_KCMP + C_KV_WIDTH
N_F32_COLS = FOFF_VCMP + C_KV_WIDTH
BOFF_QA = 0
BOFF_KA = BOFF_QA + A_WIDTH
BOFF_VA = BOFF_KA + A_WIDTH
BOFF_QC = BOFF_VA + A_WIDTH
BOFF_KSLC = BOFF_QC + C_WIDTH
BOFF_VSLC = BOFF_KSLC + C_KV_WIDTH
BOFF_KWIN = BOFF_VSLC + C_KV_WIDTH
BOFF_VWIN = BOFF_KWIN + C_KV_WIDTH
N_BF16_COLS = BOFF_VWIN + C_KV_WIDTH

VMEM_LIMIT = 56 * 1024 * 1024
ROW_TILE = 1024
COL_TILE = 512
FFN_COL_TILE = 1408
MXU_COLS = 256


def _params(*sem):
    return pltpu.CompilerParams(dimension_semantics=sem, vmem_limit_bytes=VMEM_LIMIT)


def _dot(a, b):
    return jnp.dot(a, b, preferred_element_type=F32)


def _dot_nt(a, b):
    return lax.dot_general(a, b, (((1,), (1,)), ((), ())), preferred_element_type=F32)


def _sigmoid(x):
    return 1.0 / (1.0 + jnp.exp(-x))


def _rms_scaled(x, g):
    return x * lax.rsqrt(jnp.mean(x * x, axis=-1, keepdims=True) + EPS) * g


def _row_norm_kernel(x_ref, g_ref, o_ref):
    o_ref[...] = _rms_scaled(x_ref[...], g_ref[...]).astype(o_ref.dtype)


def _row_norm(x, g, layer, *, tm=512):
    m, d = x.shape
    return pl.pallas_call(
        _row_norm_kernel,
        out_shape=jax.ShapeDtypeStruct((m, d), BF16),
        grid=(m // tm,),
        in_specs=[pl.BlockSpec((tm, d), lambda i: (i, 0)), pl.BlockSpec((None, 1, d), lambda i: (layer, 0, 0))],
        out_specs=pl.BlockSpec((tm, d), lambda i: (i, 0)),
        compiler_params=_params("parallel"),
        name="row_norm",
    )(x, g)


def _ffn_up_kernel(xn_ref, wa_ref, wb_ref, h_ref):
    tn = h_ref.shape[-1]
    whole = tn // MXU_COLS * MXU_COLS
    xn = xn_ref[...]

    def swiglu(a, b):
        return (a * _sigmoid(a) * b).astype(BF16)

    h_ref[:, :whole] = swiglu(_dot(xn, wa_ref[:, :whole]), _dot(xn, wb_ref[:, :whole]))
    if whole < tn:
        rem = tn - whole
        r = _dot(xn, jnp.concatenate([wa_ref[:, whole:], wb_ref[:, whole:]], axis=1))
        h_ref[:, whole:] = swiglu(r[:, :rem], r[:, rem:])


def _ffn_up(xn, wi, layer, *, tm=ROW_TILE, tn=FFN_COL_TILE):
    m, d = xn.shape
    d_ff = wi.shape[-1] // 2
    nj = d_ff // tn
    return pl.pallas_call(
        _ffn_up_kernel,
        out_shape=jax.ShapeDtypeStruct((m, d_ff), BF16),
        grid=(m // tm, nj),
        in_specs=[
            pl.BlockSpec((tm, d), lambda i, j: (i, 0)),
            pl.BlockSpec((None, d, tn), lambda i, j: (layer, 0, j)),
            pl.BlockSpec((None, d, tn), lambda i, j: (layer, 0, j + nj)),
        ],
        out_specs=pl.BlockSpec((tm, tn), lambda i, j: (i, j)),
        compiler_params=_params("parallel", "arbitrary"),
        name="ffn_up",
    )(xn, wi, wi)


def _ffn_down_kernel(h_ref, w_ref, r_ref, g_ref, *o_refs, last):
    y = r_ref[...] + 0.5 * _dot(h_ref[...], w_ref[...])
    if last:
        o_refs[0][...] = _rms_scaled(y, g_ref[...])
    else:
        o_refs[0][...] = y
        o_refs[1][...] = _rms_scaled(y, g_ref[...]).astype(BF16)


def _ffn_down(h, wo, res, g_next, layer, g_layer, *, last=False, tm=256):
    m, d_ff = h.shape
    d = res.shape[-1]
    rows = pl.BlockSpec((tm, d), lambda i: (i, 0))
    x_out = jax.ShapeDtypeStruct((m, d), F32)
    return pl.pallas_call(
        functools.partial(_ffn_down_kernel, last=last),
        out_shape=x_out if last else (x_out, jax.ShapeDtypeStruct((m, d), BF16)),
        grid=(m // tm,),
        in_specs=[
            pl.BlockSpec((tm, d_ff), lambda i: (i, 0)),
            pl.BlockSpec((None, d_ff, d), lambda i: (layer, 0, 0), pipeline_mode=pl.Buffered(1)),
            rows,
            pl.BlockSpec((None, 1, d), lambda i: (g_layer, 0, 0)),
        ],
        out_specs=rows if last else (rows, rows),
        compiler_params=_params("parallel"),
        name="ffn_down",
    )(h, wo, res, g_next)


def _in_proj_kernel(xn_ref, w_ref, wg_ref, sc_ref, of_ref, ob_ref, og_ref, *, n_f32_tiles):
    j = pl.program_id(1)

    @pl.when(j == 0)
    def _():
        og_ref[...] = _dot(xn_ref[...], wg_ref[...])

    acc = _dot(xn_ref[...], w_ref[...])

    @pl.when(j < n_f32_tiles)
    def _():
        of_ref[...] = acc

    @pl.when(j >= n_f32_tiles)
    def _():
        ob_ref[...] = (acc * sc_ref[...]).astype(BF16)


def _in_proj(xn, w_main, w_gate, col_scale, layer, *, tm=ROW_TILE, tn=2 * COL_TILE):
    m, d = xn.shape
    ng = w_gate.shape[-1]
    nf = N_F32_COLS // tn
    nb = N_BF16_COLS // tn
    return pl.pallas_call(
        functools.partial(_in_proj_kernel, n_f32_tiles=nf),
        out_shape=(jax.ShapeDtypeStruct((m, N_F32_COLS), F32), jax.ShapeDtypeStruct((m, N_BF16_COLS), BF16),
                   jax.ShapeDtypeStruct((m, ng), F32)),
        grid=(m // tm, nf + nb),
        in_specs=[
            pl.BlockSpec((tm, d), lambda i, j: (i, 0)),
            pl.BlockSpec((None, d, tn), lambda i, j: (layer, 0, j)),
            pl.BlockSpec((None, d, ng), lambda i, j: (layer, 0, 0)),
            pl.BlockSpec((1, tn), lambda i, j: (0, j)),
        ],
        out_specs=(
            pl.BlockSpec((tm, tn), lambda i, j: (i, jnp.minimum(j, nf - 1))),
            pl.BlockSpec((tm, tn), lambda i, j: (i, jnp.maximum(j - nf, 0))),
            pl.BlockSpec((tm, ng), lambda i, j: (i, 0)),
        ),
        compiler_params=_params("parallel", "arbitrary"),
        name="in_proj",
    )(xn, w_main, w_gate, col_scale)


def _dil_heads(q, kc, vc, kp, vp, first):
    r = lax.broadcasted_iota(jnp.int32, (Q_BLOCK, Q_BLOCK), 0)
    c = lax.broadcasted_iota(jnp.int32, (Q_BLOCK, Q_BLOCK), 1)
    bias_cur = jnp.where(c <= r, 0.0, NEG)
    if kp is not None:
        bias_prev = jnp.where(c >= r + jnp.where(first, Q_BLOCK, 0), 0.0, NEG)
    head = lax.shift_right_logical(lax.broadcasted_iota(jnp.int32, (Q_BLOCK, A_GROUP_WIDTH), 1),
                                   int(np.log2(A_HEAD_DIM)))
    o_acc = jnp.zeros((Q_BLOCK, A_GROUP_WIDTH), F32)
    l_acc = jnp.zeros((Q_BLOCK, A_GROUP_WIDTH), F32)
    for h in range(A_HEADS_PER_GROUP):
        mine = head == h
        qh = q * jnp.where(mine, 1.0, 0.0).astype(BF16)
        s_c = _dot_nt(qh, kc) + bias_cur
        m = jnp.max(s_c, axis=-1, keepdims=True)
        if kp is not None:
            s_p = _dot_nt(qh, kp) + bias_prev
            m = jnp.maximum(m, jnp.max(s_p, axis=-1, keepdims=True))
        p_c = jnp.exp(s_c - m)
        l = jnp.sum(p_c, axis=-1, keepdims=True)
        pv = _dot(p_c.astype(BF16), vc)
        if kp is not None:
            p_p = jnp.exp(s_p - m)
            l = l + jnp.sum(p_p, axis=-1, keepdims=True)
            pv = pv + _dot(p_p.astype(BF16), vp)
        o_acc = jnp.where(mine, pv / l, o_acc)
        l_acc = jnp.where(mine, m + jnp.log(l), l_acc)
    return o_acc, l_acc


def _dil_attn_dense_kernel(q_ref, kc_ref, kp_ref, vc_ref, vp_ref, o_ref, l_ref):
    o, lse = _dil_heads(q_ref[...], kc_ref[...], vc_ref[...], kp_ref[...], vp_ref[...], pl.program_id(1) == 0)
    for half in range(A_GROUP_WIDTH // LANES):
        o_ref[half] = o[:, half * LANES:(half + 1) * LANES]
        l_ref[half] = lse[:, half * LANES:(half + 1) * LANES]


def _dil_attn_strided_kernel(q_ref, k_ref, v_ref, o_ref, l_ref, qf_ref, kf_ref, vf_ref, *, dil, has_prev):
    i = pl.program_id(1)
    r = pl.program_id(2)
    halves = A_GROUP_WIDTH // LANES
    slot = lax.rem(i, 2) if has_prev else 0

    @pl.when(r == 0)
    def _():
        for half in range(halves):
            cs = slice(half * LANES, (half + 1) * LANES)
            qf_ref[half] = q_ref[:, cs].astype(F32)
            kf_ref[slot, half] = k_ref[:, cs].astype(F32)
            vf_ref[slot, half] = v_ref[:, cs].astype(F32)

    if has_prev:
        @pl.when((r == 0) & (i == 0))
        def _():
            kf_ref[1] = jnp.zeros(kf_ref.shape[1:], F32)
            vf_ref[1] = jnp.zeros(vf_ref.shape[1:], F32)

    def rows_of_class(ref, *lead):
        return jnp.concatenate([ref[(*lead, half, pl.ds(r, Q_BLOCK, stride=dil), slice(None))]
                                for half in range(halves)], axis=1).astype(BF16)

    kp = rows_of_class(kf_ref, 1 - slot) if has_prev else None
    vp = rows_of_class(vf_ref, 1 - slot) if has_prev else None
    o, lse = _dil_heads(rows_of_class(qf_ref), rows_of_class(kf_ref, slot), rows_of_class(vf_ref, slot), kp, vp, i == 0)
    for half in range(halves):
        o_ref[half, pl.ds(r, Q_BLOCK, stride=dil), :] = o[:, half * LANES:(half + 1) * LANES]
        l_ref[half, pl.ds(r, Q_BLOCK, stride=dil), :] = lse[:, half * LANES:(half + 1) * LANES]


def _dil_attn(cb3, gi):
    bsz, s, _ = cb3.shape
    dil = A_GROUPS[gi][1]
    w = A_GROUP_WIDTH
    halves = w // LANES
    out = jax.ShapeDtypeStruct((bsz, halves, s, LANES), F32)
    name = f"dilated_attention_g{gi}"
    if dil == 1:
        tiles = s // Q_BLOCK

        def col(off, prev):
            return pl.BlockSpec((None, Q_BLOCK, w),
                                lambda b, i: (b, jnp.maximum(i - 1, 0) if prev else i, (off + gi * w) // w))

        ospec = pl.BlockSpec((None, halves, Q_BLOCK, LANES), lambda b, i: (b, 0, i, 0))
        return pl.pallas_call(
            _dil_attn_dense_kernel,
            out_shape=(out, out),
            grid=(bsz, tiles),
            in_specs=[col(BOFF_QA, False), col(BOFF_KA, False), col(BOFF_KA, True), col(BOFF_VA, False), col(BOFF_VA, True)],
            out_specs=(ospec, ospec),
            compiler_params=_params("parallel", "arbitrary"),
            name=name,
        )(cb3, cb3, cb3, cb3, cb3)

    rows = Q_BLOCK * dil
    tiles = s // rows
    has_prev = tiles > 1

    def col(off):
        return pl.BlockSpec((None, rows, w), lambda b, i, r: (b, i, (off + gi * w) // w))

    ospec = pl.BlockSpec((None, halves, rows, LANES), lambda b, i, r: (b, 0, i, 0))
    return pl.pallas_call(
        functools.partial(_dil_attn_strided_kernel, dil=dil, has_prev=has_prev),
        out_shape=(out, out),
        grid=(bsz, tiles, dil),
        in_specs=[col(BOFF_QA), col(BOFF_KA), col(BOFF_VA)],
        out_specs=(ospec, ospec),
        scratch_shapes=[pltpu.VMEM((halves, rows, LANES), F32),
                        pltpu.VMEM((2 if has_prev else 1, halves, rows, LANES), F32),
                        pltpu.VMEM((2 if has_prev else 1, halves, rows, LANES), F32)],
        compiler_params=_params("parallel", "arbitrary", "arbitrary"),
        name=name,
    )(cb3, cb3, cb3)


def _pool_kernel(x_ref, w_ref, sc_ref, o_ref):
    s = x_ref.shape[0]
    t = lax.broadcasted_iota(jnp.int32, (s, B_GROUP_DIM), 0)
    for gi, win in enumerate(B_WINDOWS):
        cs = slice(gi * B_GROUP_DIM, (gi + 1) * B_GROUP_DIM)
        x = x_ref[:, cs]
        acc = x
        k = 1
        while k < win:
            acc = acc + jnp.where(t >= k, pltpu.roll(acc, k, axis=0), 0.0)
            k *= 2
        cnt = jnp.minimum(t + 1, win).astype(F32)
        z = acc / cnt - x
        o_ref[:, cs] = _dot(z.astype(BF16), w_ref[gi]) * sc_ref[:, cs]


def _pool_mixer(cf3, pool_w, pool_scale, layer):
    bsz, s, _ = cf3.shape
    ng = len(B_WINDOWS)
    return pl.pallas_call(
        _pool_kernel,
        out_shape=jax.ShapeDtypeStruct((bsz, s, B_WIDTH), F32),
        grid=(bsz,),
        in_specs=[
            pl.BlockSpec((None, s, B_WIDTH), lambda b: (b, 0, FOFF_XB // B_WIDTH)),
            pl.BlockSpec((None, ng, B_GROUP_DIM, B_GROUP_DIM), lambda b: (layer, 0, 0, 0)),
            pl.BlockSpec((None, 1, B_WIDTH), lambda b: (layer, 0, 0)),
        ],
        out_specs=pl.BlockSpec((None, s, B_WIDTH), lambda b: (b, 0, 0)),
        compiler_params=_params("parallel"),
        name="pool_mixer",
    )(cf3, pool_w, pool_scale)


def _gelu_tanh(x):
    return 0.5 * x * (1.0 + jnp.tanh(np.float32(np.sqrt(2.0 / np.pi)) * (x + 0.044715 * (x * x * x))))


def _compress_one(z_ref, pe_ref, w1_ref, w2_ref, o_ref):
    nch = z_ref.shape[0] // CMP_STRIDE
    dh = C_HEAD_DIM
    first = jnp.zeros((nch, dh), F32)
    second = jnp.zeros((nch, dh), F32)
    for p in range(CMP_STRIDE):
        zp = z_ref[pl.ds(p, nch, stride=CMP_STRIDE), :]
        a = (zp + pe_ref[p:p + 1, :]).astype(BF16)
        b = (zp + pe_ref[CMP_STRIDE + p:CMP_STRIDE + p + 1, :]).astype(BF16)
        first = first + _dot(a, w1_ref[p * dh:(p + 1) * dh, :])
        second = second + _dot(b, w1_ref[(CMP_STRIDE + p) * dh:(CMP_STRIDE + p + 1) * dh, :])
    pre = first + pltpu.roll(second, nch - 1, axis=0)
    o_ref[...] = _dot(_gelu_tanh(pre).astype(BF16), w2_ref[...]).astype(BF16)


def _compress_kernel(zk_ref, zv_ref, pek_ref, w1k_ref, w2k_ref, pev_ref, w1v_ref, w2v_ref, kc_ref, vc_ref):
    _compress_one(zk_ref, pek_ref, w1k_ref, w2k_ref, kc_ref)
    _compress_one(zv_ref, pev_ref, w1v_ref, w2v_ref, vc_ref)


def _compress(cf3, pe_k, w1_k, w2_k, pe_v, w1_v, w2_v, layer):
    bsz, s, _ = cf3.shape
    nch = s // CMP_STRIDE
    dh = C_HEAD_DIM
    cdim = CMP_LEN * dh

    def col(off):
        return pl.BlockSpec((None, s, dh), lambda b, g: (b, 0, off // dh + g))

    def wspec(shape):
        return pl.BlockSpec((None,) + shape, lambda b, g: (layer,) + (0,) * len(shape))

    out = jax.ShapeDtypeStruct((bsz, C_KV_GROUPS, nch, dh), BF16)
    ospec = pl.BlockSpec((None, None, nch, dh), lambda b, g: (b, g, 0, 0))
    return pl.pallas_call(
        _compress_kernel,
        out_shape=(out, out),
        grid=(bsz, C_KV_GROUPS),
        in_specs=[col(FOFF_KCMP), col(FOFF_VCMP),
                  wspec((CMP_LEN, dh)), wspec((cdim, dh)), wspec((dh, dh)),
                  wspec((CMP_LEN, dh)), wspec((cdim, dh)), wspec((dh, dh))],
        out_specs=(ospec, ospec),
        compiler_params=_params("parallel", "parallel"),
        name="nsa_compress",
    )(cf3, cf3, pe_k, w1_k, w2_k, pe_v, w1_v, w2_v)


def _split3(x):
    hi = x.astype(BF16)
    r1 = x - hi.astype(F32)
    mid = r1.astype(BF16)
    lo = (r1 - mid.astype(F32)).astype(BF16)
    return hi, mid, lo


def _softmax_pv(s, v):
    m = jnp.max(s, axis=-1, keepdims=True)
    p = jnp.exp(s - m)
    return _dot(p.astype(BF16), v) / jnp.sum(p, axis=-1, keepdims=True)


def _nsa_kernel(q_ref, kc_ref, vc_ref, ks_ref, vs_ref, kw_ref, vw_ref, gt_ref, ovt_ref, ex_ref, o_ref, *, s_len):
    n = pl.program_id(2)
    nb = s_len // SEL_BLOCK
    hg = C_HEADS_PER_GROUP
    dh = C_HEAD_DIM
    t0 = n * Q_BLOCK
    row = t0 + lax.broadcasted_iota(jnp.int32, (Q_BLOCK, LANES), 0)
    lane = lax.broadcasted_iota(jnp.int32, (Q_BLOCK, LANES), 1)

    q3 = jnp.concatenate([q_ref[:, h * dh:(h + 1) * dh] for h in range(hg)], axis=0)
    rows = [slice(h * Q_BLOCK, (h + 1) * Q_BLOCK) for h in range(hg)]
    gt = _sigmoid(gt_ref[...])

    cvalid = lane * CMP_STRIDE + (CMP_LEN - 1) <= row
    s3 = _dot_nt(q3, kc_ref[...])
    vc = vc_ref[...]
    out = []
    psum = jnp.zeros((Q_BLOCK, LANES), F32)
    for h in range(hg):
        s = jnp.where(cvalid, s3[rows[h]], NEG)
        p = jnp.exp(s - jnp.max(s, axis=-1, keepdims=True))
        p = jnp.where(cvalid, p / jnp.sum(p, axis=-1, keepdims=True), 0.0)
        out.append(gt[:, 3 * h:3 * h + 1] * _dot(p.astype(BF16), vc))
        psum = psum + p

    ovt = ovt_ref[...]
    imp_t = sum(_dot_nt(ovt, part) for part in _split3(psum))
    blk = lax.broadcasted_iota(jnp.int32, (nb, Q_BLOCK), 0)
    tb = lax.shift_right_logical(t0 + lax.broadcasted_iota(jnp.int32, (nb, Q_BLOCK), 1), int(np.log2(SEL_BLOCK)))
    forced = (blk == 0) | (blk == tb) | (blk == tb - 1)
    score = jnp.where(blk > tb, -1.0, imp_t + jnp.where(forced, FORCE_BONUS, 0.0))
    blk_f = blk.astype(F32)
    sel_t = jnp.zeros((nb, Q_BLOCK), F32)
    for _ in range(min(SEL_TOPN, nb)):
        mx = jnp.max(score, axis=0, keepdims=True)
        idx = jnp.min(jnp.where(score == mx, blk_f, float(nb)), axis=0, keepdims=True)
        hit = blk_f == idx
        sel_t = jnp.where(hit, 1.0, sel_t)
        score = jnp.where(hit, -3.0, score)
    sel = jnp.concatenate([sel_t, jnp.zeros((LANES - nb, Q_BLOCK), F32)], axis=0).T.astype(BF16)

    wlen = WIN + Q_BLOCK
    w0 = pl.multiple_of(jnp.maximum(t0 - WIN, 0), Q_BLOCK)
    kpos = w0 + lax.broadcasted_iota(jnp.int32, (Q_BLOCK, wlen), 1)
    qpos = t0 + lax.broadcasted_iota(jnp.int32, (Q_BLOCK, wlen), 0)
    bias = jnp.where((kpos <= qpos) & (kpos > qpos - WIN), 0.0, NEG)
    s3 = _dot_nt(q3, kw_ref[pl.ds(w0, wlen), :])
    vw = vw_ref[pl.ds(w0, wlen), :]
    for h in range(hg):
        out[h] = out[h] + gt[:, 3 * h + 2:3 * h + 3] * _softmax_pv(s3[rows[h]] + bias, vw)

    for cls in range(s_len // SEL_CHUNK):
        klen = (cls + 1) * SEL_CHUNK

        @pl.when(n // (SEL_CHUNK // Q_BLOCK) == cls)
        def _(klen=klen):
            kpos = lax.broadcasted_iota(jnp.int32, (Q_BLOCK, klen), 1)
            qpos = t0 + lax.broadcasted_iota(jnp.int32, (Q_BLOCK, klen), 0)
            chosen = _dot(sel, ex_ref[:, :klen])
            bias = jnp.where((chosen > 0.5) & (kpos <= qpos), 0.0, NEG)
            s3 = _dot_nt(q3, ks_ref[:klen, :])
            vs = vs_ref[:klen, :]
            for h in range(hg):
                o_ref[:, h * dh:(h + 1) * dh] = out[h] + gt[:, 3 * h + 1:3 * h + 2] * _softmax_pv(s3[rows[h]] + bias, vs)


def _nsa_constants(s):
    nch = s // CMP_STRIDE
    nb = s // SEL_BLOCK
    n_cmp = (s - CMP_LEN) // CMP_STRIDE + 1
    ci = np.arange(nch)[None, :] * CMP_STRIDE
    bj = np.arange(nb)[:, None] * SEL_BLOCK
    overlap_t = (ci < bj + SEL_BLOCK) & (ci + CMP_LEN > bj) & (np.arange(nch)[None, :] < n_cmp)
    expand = np.arange(LANES)[:, None] == (np.arange(s)[None, :] // SEL_BLOCK)
    return jnp.asarray(overlap_t, BF16), jnp.asarray(expand, BF16)


def _nsa(cb3, kc, vc, gates3):
    bsz, s, _ = cb3.shape
    nq = s // Q_BLOCK
    nch = s // CMP_STRIDE
    nb = s // SEL_BLOCK
    dh = C_HEAD_DIM
    assert nch == LANES and nb <= LANES and nb % 8 == 0 and s % SEL_CHUNK == 0 and s >= WIN + Q_BLOCK
    overlap_t, expand = _nsa_constants(s)

    def kv(off):
        return pl.BlockSpec((None, s, dh), lambda b, g, n: (b, 0, off // dh + g))

    cmp_spec = pl.BlockSpec((None, None, nch, dh), lambda b, g, n: (b, g, 0, 0))
    return pl.pallas_call(
        functools.partial(_nsa_kernel, s_len=s),
        out_shape=jax.ShapeDtypeStruct((bsz, s, C_WIDTH), F32),
        grid=(bsz, C_KV_GROUPS, nq),
        in_specs=[
            pl.BlockSpec((None, Q_BLOCK, C_GROUP_WIDTH), lambda b, g, n: (b, n, BOFF_QC // C_GROUP_WIDTH + g)),
            cmp_spec, cmp_spec,
            kv(BOFF_KSLC), kv(BOFF_VSLC), kv(BOFF_KWIN), kv(BOFF_VWIN),
            pl.BlockSpec((None, Q_BLOCK, LANES), lambda b, g, n: (b, n, g)),
            pl.BlockSpec((nb, nch), lambda b, g, n: (0, 0)),
            pl.BlockSpec((LANES, s), lambda b, g, n: (0, 0)),
        ],
        out_specs=pl.BlockSpec((None, Q_BLOCK, C_GROUP_WIDTH), lambda b, g, n: (b, n, g)),
        compiler_params=_params("parallel", "parallel", "arbitrary"),
        name="nsa_attention",
    )(cb3, kc, vc, cb3, cb3, cb3, cb3, gates3, overlap_t, expand)


def _merge_kernel(*refs):
    ng = len(A_GROUPS)
    oa_refs, la_refs = refs[:ng], refs[ng:2 * ng]
    zb_ref, oc_ref, g0_ref, g1_ref, g2_ref, x_ref, pa_ref, pb_ref, pc_ref, wo_ref, gn_ref, o_ref, xn_ref = refs[2 * ng:]

    def lanes(ref):
        return jnp.concatenate([ref[half] for half in range(ref.shape[0])], axis=1)

    lse = [lanes(r) for r in la_refs]
    m = functools.reduce(jnp.maximum, lse)
    e = [jnp.exp(l - m) for l in lse]
    oa = sum(w * lanes(r) for w, r in zip(e, oa_refs)) / sum(e)
    ya = _dot(oa.astype(BF16), pa_ref[...])
    yb = _dot(zb_ref[...].astype(BF16), pb_ref[...])
    yc = _dot(oc_ref[...].astype(BF16), pc_ref[...])
    mix = _sigmoid(g0_ref[...]) * ya + _sigmoid(g1_ref[...]) * yb + _sigmoid(g2_ref[...]) * yc
    y = x_ref[...] + _dot(mix.astype(BF16), wo_ref[...])
    o_ref[...] = y
    xn_ref[...] = _rms_scaled(y, gn_ref[...]).astype(BF16)


def _merge(oa, la, zb, oc, cf, x, proj_a, proj_b, proj_c, w_out, g_next, layer, *, tm=256):
    m, d = x.shape
    assert d == D_GATE

    def rows(w):
        return pl.BlockSpec((tm, w), lambda i: (i, 0))

    def gate(k):
        return pl.BlockSpec((tm, d), lambda i: (i, FOFF_GM // d + k))

    def weight(k):
        return pl.BlockSpec((None, k, d), lambda i: (layer, 0, 0), pipeline_mode=pl.Buffered(1))

    _, halves, s, _ = oa[0].shape
    per_seq = s // tm
    group = pl.BlockSpec((None, halves, tm, LANES), lambda i: (i // per_seq, 0, i % per_seq, 0))

    return pl.pallas_call(
        _merge_kernel,
        out_shape=(jax.ShapeDtypeStruct((m, d), F32), jax.ShapeDtypeStruct((m, d), BF16)),
        grid=(m // tm,),
        in_specs=[group] * (2 * len(A_GROUPS))
        + [rows(B_WIDTH), rows(C_WIDTH), gate(0), gate(1), gate(2), rows(d),
           weight(A_GROUP_WIDTH), weight(B_WIDTH), weight(C_WIDTH), weight(d),
           pl.BlockSpec((None, 1, d), lambda i: (layer, 0, 0))],
        out_specs=(rows(d), rows(d)),
        compiler_params=_params("parallel"),
        name="merge",
    )(*oa, *la, zb, oc, cf, cf, cf, x, proj_a, proj_b, proj_c, w_out, g_next)


def _reorder_w_in(w_in):
    a3 = 3 * A_WIDTH
    xb_end = a3 + B_WIDTH
    qc_end = xb_end + C_WIDTH
    cmp_end = qc_end + 2 * C_KV_WIDTH
    kv_end = qc_end + 6 * C_KV_WIDTH
    gm_start = kv_end + N_GATES
    w_main = jnp.concatenate(
        [w_in[..., gm_start:], w_in[..., a3:xb_end], w_in[..., qc_end:cmp_end],
         w_in[..., :a3], w_in[..., xb_end:qc_end], w_in[..., cmp_end:kv_end]],
        axis=-1).astype(BF16)
    per_group = 3 * C_HEADS_PER_GROUP
    pad = jnp.zeros(w_in.shape[:-1] + (LANES - per_group,), w_in.dtype)
    w_gate = jnp.concatenate(
        [piece for g in range(C_KV_GROUPS)
         for piece in (w_in[..., kv_end + g * per_group:kv_end + (g + 1) * per_group], pad)], axis=-1).astype(BF16)
    col_scale = np.ones((1, N_F32_COLS + N_BF16_COLS), np.float32)
    col_scale[:, N_F32_COLS + BOFF_QA:N_F32_COLS + BOFF_QA + A_WIDTH] = A_HEAD_DIM ** -0.5
    col_scale[:, N_F32_COLS + BOFF_QC:N_F32_COLS + BOFF_QC + C_WIDTH] = C_HEAD_DIM ** -0.5
    return w_main, w_gate, jnp.asarray(col_scale)


def _mixing(x, xn, layer, w_main, w_gate, col_scale, pool_w, pool_scale, pe_k, w1_k, w2_k, pe_v, w1_v, w2_v,
            proj_a, proj_b, proj_c, w_out, g_next, bsz, s):
    m, d = x.shape
    cf, cb, gates = _in_proj(xn, w_main, w_gate, col_scale, layer)
    cf3 = cf.reshape(bsz, s, N_F32_COLS)
    cb3 = cb.reshape(bsz, s, N_BF16_COLS)
    gates3 = gates.reshape(bsz, s, C_KV_GROUPS * LANES)

    oa, la = zip(*[_dil_attn(cb3, gi) for gi in range(len(A_GROUPS))])
    zb = _pool_mixer(cf3, pool_w, pool_scale, layer).reshape(m, B_WIDTH)
    kc, vc = _compress(cf3, pe_k, w1_k, w2_k, pe_v, w1_v, w2_v, layer)
    oc = _nsa(cb3, kc, vc, gates3).reshape(m, C_WIDTH)
    return _merge(oa, la, zb, oc, cf, x, proj_a, proj_b, proj_c, w_out, g_next, layer)


def kernel(x, ffn1_norm, ffn1_wi, ffn1_wo, mix_norm, w_in, pool_w, pool_scale, cmp_pe_k, cmp_w1_k, cmp_w2_k,
           cmp_pe_v, cmp_w1_v, cmp_w2_v, proj_a, proj_b, proj_c, w_out, ffn2_norm, ffn2_wi, ffn2_wo, final_norm):
    bsz, s, d = x.shape
    depth = ffn1_wi.shape[0]
    for win, dil in A_GROUPS:
        assert win // dil == Q_BLOCK and s % (dil * Q_BLOCK) == 0
    assert CMP_LEN == 2 * CMP_STRIDE and all(w & (w - 1) == 0 for w in B_WINDOWS)

    bf = lambda w: w.astype(BF16)
    row3 = lambda g: g.reshape(g.shape[0], 1, g.shape[-1])
    w_main, w_gate, col_scale = _reorder_w_in(w_in)
    ffn1_wi, ffn1_wo, ffn2_wi, ffn2_wo = bf(ffn1_wi), bf(ffn1_wo), bf(ffn2_wi), bf(ffn2_wo)
    pool_w, cmp_w1_k, cmp_w2_k, cmp_w1_v, cmp_w2_v = bf(pool_w), bf(cmp_w1_k), bf(cmp_w2_k), bf(cmp_w1_v), bf(cmp_w2_v)
    proj_a, proj_b, proj_c, w_out = bf(proj_a), bf(proj_b), bf(proj_c), bf(w_out)
    ffn1_norm, mix_norm, ffn2_norm, pool_scale = row3(ffn1_norm), row3(mix_norm), row3(ffn2_norm), row3(pool_scale)

    final_norm = final_norm.reshape(1, 1, d)
    x = x.reshape(bsz * s, d)
    xn = _row_norm(x, ffn1_norm, 0)
    for layer in range(depth):
        x, xn = _ffn_down(_ffn_up(xn, ffn1_wi, layer), ffn1_wo, x, mix_norm, layer, layer)
        x, xn = _mixing(x, xn, layer, w_main, w_gate, col_scale, pool_w, pool_scale, cmp_pe_k, cmp_w1_k, cmp_w2_k,
                        cmp_pe_v, cmp_w1_v, cmp_w2_v, proj_a, proj_b, proj_c, w_out, ffn2_norm, bsz, s)
        h = _ffn_up(xn, ffn2_wi, layer)
        if layer + 1 < depth:
            x, xn = _ffn_down(h, ffn2_wo, x, ffn1_norm, layer, layer + 1)
        else:
            x = _ffn_down(h, ffn2_wo, x, final_norm, layer, 0, last=True)
    return x.reshape(bsz, s, d)
```

```python
import functools

import numpy as np
import jax
import jax.numpy as jnp
from jax import lax
from jax.experimental import pallas as pl
from jax.experimental.pallas import tpu as pltpu

F32 = jnp.float32
BF16 = jnp.bfloat16

EPS = 1e-6
NEG = -1e30
Q_BLOCK = 128
LANES = 128

A_GROUPS = ((128, 1), (512, 4), (2048, 16))
A_HEADS_PER_GROUP = 4
A_HEAD_DIM = 64
A_GROUP_WIDTH = A_HEADS_PER_GROUP * A_HEAD_DIM
A_WIDTH = A_GROUP_WIDTH * len(A_GROUPS)

B_WINDOWS = (2, 4, 8, 16)
B_GROUP_DIM = 128
B_WIDTH = B_GROUP_DIM * len(B_WINDOWS)

C_KV_GROUPS = 2
C_HEADS_PER_GROUP = 3
C_HEADS = C_KV_GROUPS * C_HEADS_PER_GROUP
C_HEAD_DIM = 128
C_GROUP_WIDTH = C_HEADS_PER_GROUP * C_HEAD_DIM
C_WIDTH = C_HEADS * C_HEAD_DIM
C_KV_WIDTH = C_KV_GROUPS * C_HEAD_DIM
CMP_LEN = 32
CMP_STRIDE = 16
SEL_BLOCK = 64
SEL_TOPN = 8
FORCE_BONUS = 100.0
WIN = 512
N_GATES = 3 * C_HEADS
SEL_CHUNK = 512

N_BRANCH = 3
D_GATE = 2048

FOFF_GM = 0
FOFF_XB = N_BRANCH * D_GATE
FOFF_KCMP = FOFF_XB + B_WIDTH
FOFF_VCMP = FOFF_KCMP + C_KV_WIDTH
N_F32_COLS = FOFF_VCMP + C_KV_WIDTH
BOFF_QA = 0
BOFF_KA = BOFF_QA + A_WIDTH
BOFF_VA = BOFF_KA + A_WIDTH
BOFF_QC = BOFF_VA + A_WIDTH
BOFF_KSLC = BOFF_QC + C_WIDTH
BOFF_VSLC = BOFF_KSLC + C_KV_WIDTH
BOFF_KWIN = BOFF_VSLC + C_KV_WIDTH
BOFF_VWIN = BOFF_KWIN + C_KV_WIDTH
N_BF16_COLS = BOFF_VWIN + C_KV_WIDTH

VMEM_LIMIT = 56 * 1024 * 1024
ROW_TILE = 1024
COL_TILE = 512
FFN_ROW_TILE = 1024


def _params(*sem):
    return pltpu.CompilerParams(dimension_semantics=sem, vmem_limit_bytes=VMEM_LIMIT)


def _dot(a, b):
    return jnp.dot(a, b, preferred_element_type=F32)


def _dot_nt(a, b):
    return lax.dot_general(a, b, (((1,), (1,)), ((), ())), preferred_element_type=F32)


def _sigmoid(x):
    return 1.0 / (1.0 + jnp.exp(-x))


def _rms_scaled(x, g):
    return x * lax.rsqrt(jnp.mean(x * x, axis=-1, keepdims=True) + EPS) * g


def _row_norm_kernel(x_ref, g_ref, o_ref):
    o_ref[...] = _rms_scaled(x_ref[...], g_ref[...]).astype(o_ref.dtype)


def _row_norm(x, g, layer, *, tm=512):
    m, d = x.shape
    return pl.pallas_call(
        _row_norm_kernel,
        out_shape=jax.ShapeDtypeStruct((m, d), BF16),
        grid=(m // tm,),
        in_specs=[pl.BlockSpec((tm, d), lambda i: (i, 0)), pl.BlockSpec((None, 1, d), lambda i: (layer, 0, 0))],
        out_specs=pl.BlockSpec((tm, d), lambda i: (i, 0)),
        compiler_params=_params("parallel"),
        name="row_norm",
    )(x, g)


def _ffn_up_kernel(xn_ref, wa_ref, wb_ref, h_ref, w_ref):
    tn = h_ref.shape[-1]

    @pl.when(pl.program_id(1) == 0)
    def _():
        w_ref[:, :tn] = wa_ref[...].astype(BF16)
        w_ref[:, tn:] = wb_ref[...].astype(BF16)

    xn = xn_ref[...]
    a = _dot(xn, w_ref[:, :tn])
    b = _dot(xn, w_ref[:, tn:])
    h_ref[...] = (a * _sigmoid(a) * b).astype(BF16)


def _ffn_up(xn, wi, layer, *, tm=FFN_ROW_TILE, tn=COL_TILE):
    m, d = xn.shape
    d_ff = wi.shape[-1] // 2
    nj = d_ff // tn
    return pl.pallas_call(
        _ffn_up_kernel,
        out_shape=jax.ShapeDtypeStruct((m, d_ff), BF16),
        grid=(nj, m // tm),
        in_specs=[
            pl.BlockSpec((tm, d), lambda j, i: (i, 0)),
            pl.BlockSpec((None, d, tn), lambda j, i: (layer, 0, j)),
            pl.BlockSpec((None, d, tn), lambda j, i: (layer, 0, j + nj)),
        ],
        out_specs=pl.BlockSpec((tm, tn), lambda j, i: (i, j)),
        scratch_shapes=[pltpu.VMEM((d, 2 * tn), BF16)],
        compiler_params=_params("parallel", "arbitrary"),
        name="ffn_up",
    )(xn, wi, wi)


def _ffn_down_kernel(h_ref, w_ref, r_ref, g_ref, *o_refs, last):
    y = r_ref[...] + 0.5 * _dot(h_ref[...], w_ref[...])
    if last:
        o_refs[0][...] = _rms_scaled(y, g_ref[...])
    else:
        o_refs[0][...] = y
        o_refs[1][...] = _rms_scaled(y, g_ref[...]).astype(BF16)


def _ffn_down(h, wo, res, g_next, layer, g_layer, *, last=False, tm=256):
    m, d_ff = h.shape
    d = res.shape[-1]
    rows = pl.BlockSpec((tm, d), lambda i: (i, 0))
    x_out = jax.ShapeDtypeStruct((m, d), F32)
    return pl.pallas_call(
        functools.partial(_ffn_down_kernel, last=last),
        out_shape=x_out if last else (x_out, jax.ShapeDtypeStruct((m, d), BF16)),
        grid=(m // tm,),
        in_specs=[
            pl.BlockSpec((tm, d_ff), lambda i: (i, 0)),
            pl.BlockSpec((None, d_ff, d), lambda i: (layer, 0, 0), pipeline_mode=pl.Buffered(1)),
            rows,
            pl.BlockSpec((None, 1, d), lambda i: (g_layer, 0, 0)),
        ],
        out_specs=rows if last else (rows, rows),
        compiler_params=_params("parallel"),
        name="ffn_down",
    )(h, wo, res, g_next)


def _in_proj_kernel(xn_ref, wgm_ref, wr_ref, wg_ref, sc_ref, of_ref, ob_ref, og_ref, *, n_gm_tiles, n_f32_tiles):
    j = pl.program_id(1)

    @pl.when(j == 0)
    def _():
        og_ref[...] = _dot(xn_ref[...], wg_ref[...])

    @pl.when(j < n_gm_tiles)
    def _():
        of_ref[...] = _dot(xn_ref[...], wgm_ref[...])

    @pl.when((j >= n_gm_tiles) & (j < n_f32_tiles))
    def _():
        of_ref[...] = _dot(xn_ref[...], wr_ref[...])

    @pl.when(j >= n_f32_tiles)
    def _():
        ob_ref[...] = (_dot(xn_ref[...], wr_ref[...]) * sc_ref[...]).astype(BF16)


def _in_proj(xn, w_gm, w_rest, w_gate, col_scale, layer, *, tm=ROW_TILE, tn=2 * COL_TILE):
    m, d = xn.shape
    ng = w_gate.shape[-1]
    ngm = w_gm.shape[-1] // tn
    nf = N_F32_COLS // tn
    nb = N_BF16_COLS // tn
    assert w_gm.shape[-1] + w_rest.shape[-1] == N_F32_COLS + N_BF16_COLS and ngm < nf
    return pl.pallas_call(
        functools.partial(_in_proj_kernel, n_gm_tiles=ngm, n_f32_tiles=nf),
        out_shape=(jax.ShapeDtypeStruct((m, N_F32_COLS), F32), jax.ShapeDtypeStruct((m, N_BF16_COLS), BF16),
                   jax.ShapeDtypeStruct((m, ng), F32)),
        grid=(m // tm, nf + nb),
        in_specs=[
            pl.BlockSpec((tm, d), lambda i, j: (i, 0)),
            pl.BlockSpec((None, d, tn), lambda i, j: (layer, 0, jnp.minimum(j, ngm - 1))),
            pl.BlockSpec((None, d, tn), lambda i, j: (layer, 0, jnp.maximum(j - ngm, 0))),
            pl.BlockSpec((None, d, ng), lambda i, j: (layer, 0, 0)),
            pl.BlockSpec((1, tn), lambda i, j: (0, j)),
        ],
        out_specs=(
            pl.BlockSpec((tm, tn), lambda i, j: (i, jnp.minimum(j, nf - 1))),
            pl.BlockSpec((tm, tn), lambda i, j: (i, jnp.maximum(j - nf, 0))),
            pl.BlockSpec((tm, ng), lambda i, j: (i, 0)),
        ),
        compiler_params=_params("parallel", "arbitrary"),
        name="in_proj",
    )(xn, w_gm, w_rest, w_gate, col_scale)


def _dil_heads(q, kc, vc, kp, vp, first):
    r = lax.broadcasted_iota(jnp.int32, (Q_BLOCK, Q_BLOCK), 0)
    c = lax.broadcasted_iota(jnp.int32, (Q_BLOCK, Q_BLOCK), 1)
    bias_cur = jnp.where(c <= r, 0.0, NEG)
    if kp is not None:
        bias_prev = jnp.where(c >= r + jnp.where(first, Q_BLOCK, 0), 0.0, NEG)
    head = lax.shift_right_logical(lax.broadcasted_iota(jnp.int32, (Q_BLOCK, A_GROUP_WIDTH), 1),
                                   int(np.log2(A_HEAD_DIM)))
    o_acc = jnp.zeros((Q_BLOCK, A_GROUP_WIDTH), F32)
    l_acc = jnp.zeros((Q_BLOCK, A_GROUP_WIDTH), F32)
    for h in range(A_HEADS_PER_GROUP):
        mine = head == h
        qh = q * jnp.where(mine, 1.0, 0.0).astype(BF16)
        s_c = _dot_nt(qh, kc) + bias_cur
        m = jnp.max(s_c, axis=-1, keepdims=True)
        if kp is not None:
            s_p = _dot_nt(qh, kp) + bias_prev
            m = jnp.maximum(m, jnp.max(s_p, axis=-1, keepdims=True))
        p_c = jnp.exp(s_c - m)
        l = jnp.sum(p_c, axis=-1, keepdims=True)
        pv = _dot(p_c.astype(BF16), vc)
        if kp is not None:
            p_p = jnp.exp(s_p - m)
            l = l + jnp.sum(p_p, axis=-1, keepdims=True)
            pv = pv + _dot(p_p.astype(BF16), vp)
        o_acc = jnp.where(mine, pv / l, o_acc)
        l_acc = jnp.where(mine, m + jnp.log(l), l_acc)
    return o_acc, l_acc


def _dil_attn_dense_kernel(q_ref, kc_ref, kp_ref, vc_ref, vp_ref, o_ref, l_ref):
    o, lse = _dil_heads(q_ref[...], kc_ref[...], vc_ref[...], kp_ref[...], vp_ref[...], pl.program_id(1) == 0)
    for half in range(A_GROUP_WIDTH // LANES):
        o_ref[half] = o[:, half * LANES:(half + 1) * LANES]
        l_ref[half] = lse[:, half * LANES:(half + 1) * LANES]


def _dil_attn_strided_kernel(q_ref, k_ref, v_ref, o_ref, l_ref, qf_ref, kf_ref, vf_ref, *, dil, has_prev):
    i = pl.program_id(1)
    r = pl.program_id(2)
    halves = A_GROUP_WIDTH // LANES
    slot = lax.rem(i, 2) if has_prev else 0

    @pl.when(r == 0)
    def _():
        for half in range(halves):
            cs = slice(half * LANES, (half + 1) * LANES)
            qf_ref[half] = q_ref[:, cs].astype(F32)
            kf_ref[slot, half] = k_ref[:, cs].astype(F32)
            vf_ref[slot, half] = v_ref[:, cs].astype(F32)

    if has_prev:
        @pl.when((r == 0) & (i == 0))
        def _():
            kf_ref[1] = jnp.zeros(kf_ref.shape[1:], F32)
            vf_ref[1] = jnp.zeros(vf_ref.shape[1:], F32)

    def rows_of_class(ref, *lead):
        return jnp.concatenate([ref[(*lead, half, pl.ds(r, Q_BLOCK, stride=dil), slice(None))]
                                for half in range(halves)], axis=1).astype(BF16)

    kp = rows_of_class(kf_ref, 1 - slot) if has_prev else None
    vp = rows_of_class(vf_ref, 1 - slot) if has_prev else None
    o, lse = _dil_heads(rows_of_class(qf_ref), rows_of_class(kf_ref, slot), rows_of_class(vf_ref, slot), kp, vp, i == 0)
    for half in range(halves):
        o_ref[half, pl.ds(r, Q_BLOCK, stride=dil), :] = o[:, half * LANES:(half + 1) * LANES]
        l_ref[half, pl.ds(r, Q_BLOCK, stride=dil), :] = lse[:, half * LANES:(half + 1) * LANES]


def _dil_attn(cb3, gi):
    bsz, s, _ = cb3.shape
    dil = A_GROUPS[gi][1]
    w = A_GROUP_WIDTH
    halves = w // LANES
    out = jax.ShapeDtypeStruct((bsz, halves, s, LANES), F32)
    name = f"dilated_attention_g{gi}"
    if dil == 1:
        tiles = s // Q_BLOCK

        def col(off, prev):
            return pl.BlockSpec((None, Q_BLOCK, w),
                                lambda b, i: (b, jnp.maximum(i - 1, 0) if prev else i, (off + gi * w) // w))

        ospec = pl.BlockSpec((None, halves, Q_BLOCK, LANES), lambda b, i: (b, 0, i, 0))
        return pl.pallas_call(
            _dil_attn_dense_kernel,
            out_shape=(out, out),
            grid=(bsz, tiles),
            in_specs=[col(BOFF_QA, False), col(BOFF_KA, False), col(BOFF_KA, True), col(BOFF_VA, False), col(BOFF_VA, True)],
            out_specs=(ospec, ospec),
            compiler_params=_params("parallel", "arbitrary"),
            name=name,
        )(cb3, cb3, cb3, cb3, cb3)

    rows = Q_BLOCK * dil
    tiles = s // rows
    has_prev = tiles > 1

    def col(off):
        return pl.BlockSpec((None, rows, w), lambda b, i, r: (b, i, (off + gi * w) // w))

    ospec = pl.BlockSpec((None, halves, rows, LANES), lambda b, i, r: (b, 0, i, 0))
    return pl.pallas_call(
        functools.partial(_dil_attn_strided_kernel, dil=dil, has_prev=has_prev),
        out_shape=(out, out),
        grid=(bsz, tiles, dil),
        in_specs=[col(BOFF_QA), col(BOFF_KA), col(BOFF_VA)],
        out_specs=(ospec, ospec),
        scratch_shapes=[pltpu.VMEM((halves, rows, LANES), F32),
                        pltpu.VMEM((2 if has_prev else 1, halves, rows, LANES), F32),
                        pltpu.VMEM((2 if has_prev else 1, halves, rows, LANES), F32)],
        compiler_params=_params("parallel", "arbitrary", "arbitrary"),
        name=name,
    )(cb3, cb3, cb3)


def _pool_kernel(x_ref, w_ref, sc_ref, o_ref):
    s = x_ref.shape[0]
    t = lax.broadcasted_iota(jnp.int32, (s, B_GROUP_DIM), 0)
    for gi, win in enumerate(B_WINDOWS):
        cs = slice(gi * B_GROUP_DIM, (gi + 1) * B_GROUP_DIM)
        x = x_ref[:, cs]
        acc = x
        k = 1
        while k < win:
            acc = acc + jnp.where(t >= k, pltpu.roll(acc, k, axis=0), 0.0)
            k *= 2
        cnt = jnp.minimum(t + 1, win).astype(F32)
        z = acc / cnt - x
        o_ref[:, cs] = _dot(z.astype(BF16), w_ref[gi]) * sc_ref[:, cs]


def _pool_mixer(cf3, pool_w, pool_scale, layer):
    bsz, s, _ = cf3.shape
    ng = len(B_WINDOWS)
    return pl.pallas_call(
        _pool_kernel,
        out_shape=jax.ShapeDtypeStruct((bsz, s, B_WIDTH), F32),
        grid=(bsz,),
        in_specs=[
            pl.BlockSpec((None, s, B_WIDTH), lambda b: (b, 0, FOFF_XB // B_WIDTH)),
            pl.BlockSpec((None, ng, B_GROUP_DIM, B_GROUP_DIM), lambda b: (layer, 0, 0, 0)),
            pl.BlockSpec((None, 1, B_WIDTH), lambda b: (layer, 0, 0)),
        ],
        out_specs=pl.BlockSpec((None, s, B_WIDTH), lambda b: (b, 0, 0)),
        compiler_params=_params("parallel"),
        name="pool_mixer",
    )(cf3, pool_w, pool_scale)


def _gelu_tanh(x):
    return 0.5 * x * (1.0 + jnp.tanh(np.float32(np.sqrt(2.0 / np.pi)) * (x + 0.044715 * (x * x * x))))


def _compress_one(z_ref, pe_ref, w1_ref, w2_ref, o_ref):
    nch = z_ref.shape[0] // CMP_STRIDE
    dh = C_HEAD_DIM
    first = jnp.zeros((nch, dh), F32)
    second = jnp.zeros((nch, dh), F32)
    for p in range(CMP_STRIDE):
        zp = z_ref[pl.ds(p, nch, stride=CMP_STRIDE), :]
        a = (zp + pe_ref[p:p + 1, :]).astype(BF16)
        b = (zp + pe_ref[CMP_STRIDE + p:CMP_STRIDE + p + 1, :]).astype(BF16)
        first = first + _dot(a, w1_ref[p * dh:(p + 1) * dh, :])
        second = second + _dot(b, w1_ref[(CMP_STRIDE + p) * dh:(CMP_STRIDE + p + 1) * dh, :])
    pre = first + pltpu.roll(second, nch - 1, axis=0)
    o_ref[...] = _dot(_gelu_tanh(pre).astype(BF16), w2_ref[...]).astype(BF16)


def _compress_kernel(zk_ref, zv_ref, pek_ref, w1k_ref, w2k_ref, pev_ref, w1v_ref, w2v_ref, kc_ref, vc_ref):
    _compress_one(zk_ref, pek_ref, w1k_ref, w2k_ref, kc_ref)
    _compress_one(zv_ref, pev_ref, w1v_ref, w2v_ref, vc_ref)


def _compress(cf3, pe_k, w1_k, w2_k, pe_v, w1_v, w2_v, layer):
    bsz, s, _ = cf3.shape
    nch = s // CMP_STRIDE
    dh = C_HEAD_DIM
    cdim = CMP_LEN * dh

    def col(off):
        return pl.BlockSpec((None, s, dh), lambda b, g: (b, 0, off // dh + g))

    def wspec(shape):
        return pl.BlockSpec((None,) + shape, lambda b, g: (layer,) + (0,) * len(shape))

    out = jax.ShapeDtypeStruct((bsz, C_KV_GROUPS, nch, dh), BF16)
    ospec = pl.BlockSpec((None, None, nch, dh), lambda b, g: (b, g, 0, 0))
    return pl.pallas_call(
        _compress_kernel,
        out_shape=(out, out),
        grid=(bsz, C_KV_GROUPS),
        in_specs=[col(FOFF_KCMP), col(FOFF_VCMP),
                  wspec((CMP_LEN, dh)), wspec((cdim, dh)), wspec((dh, dh)),
                  wspec((CMP_LEN, dh)), wspec((cdim, dh)), wspec((dh, dh))],
        out_specs=(ospec, ospec),
        compiler_params=_params("parallel", "parallel"),
        name="nsa_compress",
    )(cf3, cf3, pe_k, w1_k, w2_k, pe_v, w1_v, w2_v)


def _split3(x):
    hi = x.astype(BF16)
    r1 = x - hi.astype(F32)
    mid = r1.astype(BF16)
    lo = (r1 - mid.astype(F32)).astype(BF16)
    return hi, mid, lo


def _softmax_pv(s, v):
    m = jnp.max(s, axis=-1, keepdims=True)
    p = jnp.exp(s - m)
    return _dot(p.astype(BF16), v) / jnp.sum(p, axis=-1, keepdims=True)


def _nsa_kernel(q_ref, kc_ref, vc_ref, ks_ref, vs_ref, kw_ref, vw_ref, gt_ref, ovt_ref, ex_ref, o_ref, *, s_len):
    n = pl.program_id(2)
    nb = s_len // SEL_BLOCK
    hg = C_HEADS_PER_GROUP
    dh = C_HEAD_DIM
    t0 = n * Q_BLOCK
    row = t0 + lax.broadcasted_iota(jnp.int32, (Q_BLOCK, LANES), 0)
    lane = lax.broadcasted_iota(jnp.int32, (Q_BLOCK, LANES), 1)

    q3 = jnp.concatenate([q_ref[:, h * dh:(h + 1) * dh] for h in range(hg)], axis=0)
    rows = [slice(h * Q_BLOCK, (h + 1) * Q_BLOCK) for h in range(hg)]
    gt = _sigmoid(gt_ref[...])

    cvalid = lane * CMP_STRIDE + (CMP_LEN - 1) <= row
    s3 = _dot_nt(q3, kc_ref[...])
    vc = vc_ref[...]
    out = []
    psum = jnp.zeros((Q_BLOCK, LANES), F32)
    for h in range(hg):
        s = jnp.where(cvalid, s3[rows[h]], NEG)
        p = jnp.exp(s - jnp.max(s, axis=-1, keepdims=True))
        p = jnp.where(cvalid, p / jnp.sum(p, axis=-1, keepdims=True), 0.0)
        out.append(gt[:, 3 * h:3 * h + 1] * _dot(p.astype(BF16), vc))
        psum = psum + p

    ovt = ovt_ref[...]
    imp_t = sum(_dot_nt(ovt, part) for part in _split3(psum))
    blk = lax.broadcasted_iota(jnp.int32, (nb, Q_BLOCK), 0)
    tb = lax.shift_right_logical(t0 + lax.broadcasted_iota(jnp.int32, (nb, Q_BLOCK), 1), int(np.log2(SEL_BLOCK)))
    forced = (blk == 0) | (blk == tb) | (blk == tb - 1)
    score = jnp.where(blk > tb, -1.0, imp_t + jnp.where(forced, FORCE_BONUS, 0.0))
    blk_f = blk.astype(F32)
    sel_t = jnp.zeros((nb, Q_BLOCK), F32)
    for _ in range(min(SEL_TOPN, nb)):
        mx = jnp.max(score, axis=0, keepdims=True)
        idx = jnp.min(jnp.where(score == mx, blk_f, float(nb)), axis=0, keepdims=True)
        hit = blk_f == idx
        sel_t = jnp.where(hit, 1.0, sel_t)
        score = jnp.where(hit, -3.0, score)
    sel = jnp.concatenate([sel_t, jnp.zeros((LANES - nb, Q_BLOCK), F32)], axis=0).T.astype(BF16)

    wlen = WIN + Q_BLOCK
    w0 = pl.multiple_of(jnp.maximum(t0 - WIN, 0), Q_BLOCK)
    kpos = w0 + lax.broadcasted_iota(jnp.int32, (Q_BLOCK, wlen), 1)
    qpos = t0 + lax.broadcasted_iota(jnp.int32, (Q_BLOCK, wlen), 0)
    bias = jnp.where((kpos <= qpos) & (kpos > qpos - WIN), 0.0, NEG)
    s3 = _dot_nt(q3, kw_ref[pl.ds(w0, wlen), :])
    vw = vw_ref[pl.ds(w0, wlen), :]
    for h in range(hg):
        out[h] = out[h] + gt[:, 3 * h + 2:3 * h + 3] * _softmax_pv(s3[rows[h]] + bias, vw)

    for cls in range(s_len // SEL_CHUNK):
        klen = (cls + 1) * SEL_CHUNK

        @pl.when(n // (SEL_CHUNK // Q_BLOCK) == cls)
        def _(klen=klen):
            kpos = lax.broadcasted_iota(jnp.int32, (Q_BLOCK, klen), 1)
            qpos = t0 + lax.broadcasted_iota(jnp.int32, (Q_BLOCK, klen), 0)
            chosen = _dot(sel, ex_ref[:, :klen])
            bias = jnp.where((chosen > 0.5) & (kpos <= qpos), 0.0, NEG)
            s3 = _dot_nt(q3, ks_ref[:klen, :])
            vs = vs_ref[:klen, :]
            for h in range(hg):
                o_ref[:, h * dh:(h + 1) * dh] = out[h] + gt[:, 3 * h + 1:3 * h + 2] * _softmax_pv(s3[rows[h]] + bias, vs)


def _nsa_constants(s):
    nch = s // CMP_STRIDE
    nb = s // SEL_BLOCK
    n_cmp = (s - CMP_LEN) // CMP_STRIDE + 1
    ci = np.arange(nch)[None, :] * CMP_STRIDE
    bj = np.arange(nb)[:, None] * SEL_BLOCK
    overlap_t = (ci < bj + SEL_BLOCK) & (ci + CMP_LEN > bj) & (np.arange(nch)[None, :] < n_cmp)
    expand = np.arange(LANES)[:, None] == (np.arange(s)[None, :] // SEL_BLOCK)
    return jnp.asarray(overlap_t, BF16), jnp.asarray(expand, BF16)


def _nsa(cb3, kc, vc, gates3):
    bsz, s, _ = cb3.shape
    nq = s // Q_BLOCK
    nch = s // CMP_STRIDE
    nb = s // SEL_BLOCK
    dh = C_HEAD_DIM
    assert nch == LANES and nb <= LANES and nb % 8 == 0 and s % SEL_CHUNK == 0 and s >= WIN + Q_BLOCK
    overlap_t, expand = _nsa_constants(s)

    def kv(off):
        return pl.BlockSpec((None, s, dh), lambda b, g, n: (b, 0, off // dh + g))

    cmp_spec = pl.BlockSpec((None, None, nch, dh), lambda b, g, n: (b, g, 0, 0))
    return pl.pallas_call(
        functools.partial(_nsa_kernel, s_len=s),
        out_shape=jax.ShapeDtypeStruct((bsz, s, C_WIDTH), F32),
        grid=(bsz, C_KV_GROUPS, nq),
        in_specs=[
            pl.BlockSpec((None, Q_BLOCK, C_GROUP_WIDTH), lambda b, g, n: (b, n, BOFF_QC // C_GROUP_WIDTH + g)),
            cmp_spec, cmp_spec,
            kv(BOFF_KSLC), kv(BOFF_VSLC), kv(BOFF_KWIN), kv(BOFF_VWIN),
            pl.BlockSpec((None, Q_BLOCK, LANES), lambda b, g, n: (b, n, g)),
            pl.BlockSpec((nb, nch), lambda b, g, n: (0, 0)),
            pl.BlockSpec((LANES, s), lambda b, g, n: (0, 0)),
        ],
        out_specs=pl.BlockSpec((None, Q_BLOCK, C_GROUP_WIDTH), lambda b, g, n: (b, n, g)),
        compiler_params=_params("parallel", "parallel", "arbitrary"),
        name="nsa_attention",
    )(cb3, kc, vc, cb3, cb3, cb3, cb3, gates3, overlap_t, expand)


def _merge_kernel(*refs):
    ng = len(A_GROUPS)
    oa_refs, la_refs = refs[:ng], refs[ng:2 * ng]
    zb_ref, oc_ref, g0_ref, g1_ref, g2_ref, x_ref, pa_ref, pb_ref, pc_ref, wo_ref, gn_ref, o_ref, xn_ref = refs[2 * ng:]

    def lanes(ref):
        return jnp.concatenate([ref[half] for half in range(ref.shape[0])], axis=1)

    lse = [lanes(r) for r in la_refs]
    m = functools.reduce(jnp.maximum, lse)
    e = [jnp.exp(l - m) for l in lse]
    oa = sum(w * lanes(r) for w, r in zip(e, oa_refs)) / sum(e)
    ya = _dot(oa.astype(BF16), pa_ref[...])
    yb = _dot(zb_ref[...].astype(BF16), pb_ref[...])
    yc = _dot(oc_ref[...].astype(BF16), pc_ref[...])
    mix = _sigmoid(g0_ref[...]) * ya + _sigmoid(g1_ref[...]) * yb + _sigmoid(g2_ref[...]) * yc
    y = x_ref[...] + _dot(mix.astype(BF16), wo_ref[...])
    o_ref[...] = y
    xn_ref[...] = _rms_scaled(y, gn_ref[...]).astype(BF16)


def _merge(oa, la, zb, oc, cf, x, proj_a, proj_b, proj_c, w_out, g_next, layer, *, tm=256):
    m, d = x.shape
    assert d == D_GATE

    def rows(w):
        return pl.BlockSpec((tm, w), lambda i: (i, 0))

    def gate(k):
        return pl.BlockSpec((tm, d), lambda i: (i, FOFF_GM // d + k))

    def weight(k):
        return pl.BlockSpec((None, k, d), lambda i: (layer, 0, 0), pipeline_mode=pl.Buffered(1))

    _, halves, s, _ = oa[0].shape
    per_seq = s // tm
    group = pl.BlockSpec((None, halves, tm, LANES), lambda i: (i // per_seq, 0, i % per_seq, 0))

    return pl.pallas_call(
        _merge_kernel,
        out_shape=(jax.ShapeDtypeStruct((m, d), F32), jax.ShapeDtypeStruct((m, d), BF16)),
        grid=(m // tm,),
        in_specs=[group] * (2 * len(A_GROUPS))
        + [rows(B_WIDTH), rows(C_WIDTH), gate(0), gate(1), gate(2), rows(d),
           weight(A_GROUP_WIDTH), weight(B_WIDTH), weight(C_WIDTH), weight(d),
           pl.BlockSpec((None, 1, d), lambda i: (layer, 0, 0))],
        out_specs=(rows(d), rows(d)),
        compiler_params=_params("parallel"),
        name="merge",
    )(*oa, *la, zb, oc, cf, cf, cf, x, proj_a, proj_b, proj_c, w_out, g_next)


_A3 = 3 * A_WIDTH
_XB_END = _A3 + B_WIDTH
_QC_END = _XB_END + C_WIDTH
_CMP_END = _QC_END + 2 * C_KV_WIDTH
_KV_END = _QC_END + 6 * C_KV_WIDTH
_GM_START = _KV_END + N_GATES
_REST_RANGES = ((_A3, _XB_END), (_QC_END, _CMP_END), (0, _A3), (_XB_END, _QC_END), (_CMP_END, _KV_END))


def _w_in_prep_kernel(w_ref, gm_ref, rest_ref, gate_ref):
    gm_ref[...] = w_ref[:, _GM_START:].astype(BF16)
    off = 0
    for lo, hi in _REST_RANGES:
        rest_ref[:, off:off + hi - lo] = w_ref[:, lo:hi].astype(BF16)
        off += hi - lo
    per_group = 3 * C_HEADS_PER_GROUP
    gate_ref[...] = jnp.zeros(gate_ref.shape, BF16)
    for g in range(C_KV_GROUPS):
        gate_ref[:, g * LANES:g * LANES + per_group] = (
            w_ref[:, _KV_END + g * per_group:_KV_END + (g + 1) * per_group].astype(BF16))


def _reorder_w_in(w_in, *, tr=256):
    depth, d, n_in = w_in.shape
    n_rest = sum(hi - lo for lo, hi in _REST_RANGES)
    w_gm, w_rest, w_gate = pl.pallas_call(
        _w_in_prep_kernel,
        out_shape=(jax.ShapeDtypeStruct((depth, d, n_in - _GM_START), BF16),
                   jax.ShapeDtypeStruct((depth, d, n_rest), BF16),
                   jax.ShapeDtypeStruct((depth, d, C_KV_GROUPS * LANES), BF16)),
        grid=(depth, d // tr),
        in_specs=[pl.BlockSpec((None, tr, n_in), lambda l, i: (l, i, 0))],
        out_specs=(pl.BlockSpec((None, tr, n_in - _GM_START), lambda l, i: (l, i, 0)),
                   pl.BlockSpec((None, tr, n_rest), lambda l, i: (l, i, 0)),
                   pl.BlockSpec((None, tr, C_KV_GROUPS * LANES), lambda l, i: (l, i, 0))),
        compiler_params=_params("parallel", "parallel"),
        name="w_in_prep",
    )(w_in)
    col_scale = np.ones((1, N_F32_COLS + N_BF16_COLS), np.float32)
    col_scale[:, N_F32_COLS + BOFF_QA:N_F32_COLS + BOFF_QA + A_WIDTH] = A_HEAD_DIM ** -0.5
    col_scale[:, N_F32_COLS + BOFF_QC:N_F32_COLS + BOFF_QC + C_WIDTH] = C_HEAD_DIM ** -0.5
    return w_gm, w_rest, w_gate, jnp.asarray(col_scale)


def _mixing(x, xn, layer, w_gm, w_rest, w_gate, col_scale, pool_w, pool_scale, pe_k, w1_k, w2_k, pe_v, w1_v, w2_v,
            proj_a, proj_b, proj_c, w_out, g_next, bsz, s):
    m, d = x.shape
    cf, cb, gates = _in_proj(xn, w_gm, w_rest, w_gate, col_scale, layer)
    cf3 = cf.reshape(bsz, s, N_F32_COLS)
    cb3 = cb.reshape(bsz, s, N_BF16_COLS)
    gates3 = gates.reshape(bsz, s, C_KV_GROUPS * LANES)

    oa, la = zip(*[_dil_attn(cb3, gi) for gi in range(len(A_GROUPS))])
    zb = _pool_mixer(cf3, pool_w, pool_scale, layer).reshape(m, B_WIDTH)
    kc, vc = _compress(cf3, pe_k, w1_k, w2_k, pe_v, w1_v, w2_v, layer)
    oc = _nsa(cb3, kc, vc, gates3).reshape(m, C_WIDTH)
    return _merge(oa, la, zb, oc, cf, x, proj_a, proj_b, proj_c, w_out, g_next, layer)


def kernel(x, ffn1_norm, ffn1_wi, ffn1_wo, mix_norm, w_in, pool_w, pool_scale, cmp_pe_k, cmp_w1_k, cmp_w2_k,
           cmp_pe_v, cmp_w1_v, cmp_w2_v, proj_a, proj_b, proj_c, w_out, ffn2_norm, ffn2_wi, ffn2_wo, final_norm):
    bsz, s, d = x.shape
    depth = ffn1_wi.shape[0]
    for win, dil in A_GROUPS:
        assert win // dil == Q_BLOCK and s % (dil * Q_BLOCK) == 0
    assert CMP_LEN == 2 * CMP_STRIDE and all(w & (w - 1) == 0 for w in B_WINDOWS)

    bf = lambda w: w.astype(BF16)
    row3 = lambda g: g.reshape(g.shape[0], 1, g.shape[-1])
    w_gm, w_rest, w_gate, col_scale = _reorder_w_in(w_in)
    ffn1_wo, ffn2_wo = bf(ffn1_wo), bf(ffn2_wo)
    pool_w, cmp_w1_k, cmp_w2_k, cmp_w1_v, cmp_w2_v = bf(pool_w), bf(cmp_w1_k), bf(cmp_w2_k), bf(cmp_w1_v), bf(cmp_w2_v)
    proj_a, proj_b, proj_c, w_out = bf(proj_a), bf(proj_b), bf(proj_c), bf(w_out)
    ffn1_norm, mix_norm, ffn2_norm, pool_scale = row3(ffn1_norm), row3(mix_norm), row3(ffn2_norm), row3(pool_scale)

    final_norm = final_norm.reshape(1, 1, d)
    x = x.reshape(bsz * s, d)
    xn = _row_norm(x, ffn1_norm, 0)
    for layer in range(depth):
        x, xn = _ffn_down(_ffn_up(xn, ffn1_wi, layer), ffn1_wo, x, mix_norm, layer, layer)
        x, xn = _mixing(x, xn, layer, w_gm, w_rest, w_gate, col_scale, pool_w, pool_scale, cmp_pe_k, cmp_w1_k, cmp_w2_k,
                        cmp_pe_v, cmp_w1_v, cmp_w2_v, proj_a, proj_b, proj_c, w_out, ffn2_norm, bsz, s)
        h = _ffn_up(xn, ffn2_wi, layer)
        if layer + 1 < depth:
            x, xn = _ffn_down(h, ffn2_wo, x, ffn1_norm, layer, layer + 1)
        else:
            x = _ffn_down(h, ffn2_wo, x, final_norm, layer, 0, last=True)
    return x.reshape(bsz, s, d)
```

```python
import functools

import numpy as np
import jax
import jax.numpy as jnp
from jax import lax
from jax.experimental import pallas as pl
from jax.experimental.pallas import tpu as pltpu

F32 = jnp.float32
BF16 = jnp.bfloat16

EPS = 1e-6
NEG = -1e30
Q_BLOCK = 128
LANES = 128

A_GROUPS = ((128, 1), (512, 4), (2048, 16))
A_HEADS_PER_GROUP = 4
A_HEAD_DIM = 64
A_GROUP_WIDTH = A_HEADS_PER_GROUP * A_HEAD_DIM
A_WIDTH = A_GROUP_WIDTH * len(A_GROUPS)

B_WINDOWS = (2, 4, 8, 16)
B_GROUP_DIM = 128
B_WIDTH = B_GROUP_DIM * len(B_WINDOWS)

C_KV_GROUPS = 2
C_HEADS_PER_GROUP = 3
C_HEADS = C_KV_GROUPS * C_HEADS_PER_GROUP
C_HEAD_DIM = 128
C_GROUP_WIDTH = C_HEADS_PER_GROUP * C_HEAD_DIM
C_WIDTH = C_HEADS * C_HEAD_DIM
C_KV_WIDTH = C_KV_GROUPS * C_HEAD_DIM
CMP_LEN = 32
CMP_STRIDE = 16
SEL_BLOCK = 64
SEL_TOPN = 8
FORCE_BONUS = 100.0
WIN = 512
N_GATES = 3 * C_HEADS
SEL_CHUNK = 512
NSA_TILES = 4
DIL_TILES = 4

N_BRANCH = 3
D_GATE = 2048

FOFF_GM = 0
FOFF_XB = N_BRANCH * D_GATE
FOFF_KCMP = FOFF_XB + B_WIDTH
FOFF_VCMP = FOFF_KCMP + C_KV_WIDTH
N_F32_COLS = FOFF_VCMP + C_KV_WIDTH
BOFF_QA = 0
BOFF_KA = BOFF_QA + A_WIDTH
BOFF_VA = BOFF_KA + A_WIDTH
BOFF_QC = BOFF_VA + A_WIDTH
BOFF_KSLC = BOFF_QC + C_WIDTH
BOFF_VSLC = BOFF_KSLC + C_KV_WIDTH
BOFF_KWIN = BOFF_VSLC + C_KV_WIDTH
BOFF_VWIN = BOFF_KWIN + C_KV_WIDTH
N_BF16_COLS = BOFF_VWIN + C_KV_WIDTH

VMEM_LIMIT = 56 * 1024 * 1024
ROW_TILE = 1024
COL_TILE = 512
FFN_ROW_TILE = 1024


def _params(*sem):
    return pltpu.CompilerParams(dimension_semantics=sem, vmem_limit_bytes=VMEM_LIMIT)


def _dot(a, b):
    return jnp.dot(a, b, preferred_element_type=F32)


def _dot_nt(a, b):
    return lax.dot_general(a, b, (((1,), (1,)), ((), ())), preferred_element_type=F32)


def _sigmoid(x):
    return 1.0 / (1.0 + jnp.exp(-x))


def _rms_scaled(x, g):
    return x * lax.rsqrt(jnp.mean(x * x, axis=-1, keepdims=True) + EPS) * g


def _row_norm_kernel(x_ref, g_ref, o_ref):
    o_ref[...] = _rms_scaled(x_ref[...], g_ref[...]).astype(o_ref.dtype)


def _row_norm(x, g, layer, *, tm=512):
    m, d = x.shape
    return pl.pallas_call(
        _row_norm_kernel,
        out_shape=jax.ShapeDtypeStruct((m, d), BF16),
        grid=(m // tm,),
        in_specs=[pl.BlockSpec((tm, d), lambda i: (i, 0)), pl.BlockSpec((None, 1, d), lambda i: (layer, 0, 0))],
        out_specs=pl.BlockSpec((tm, d), lambda i: (i, 0)),
        compiler_params=_params("parallel"),
        name="row_norm",
    )(x, g)


def _ffn_up_kernel(xn_ref, wa_ref, wb_ref, h_ref, w_ref):
    tn = h_ref.shape[-1]

    @pl.when(pl.program_id(1) == 0)
    def _():
        w_ref[:, :tn] = wa_ref[...].astype(BF16)
        w_ref[:, tn:] = wb_ref[...].astype(BF16)

    xn = xn_ref[...]
    a = _dot(xn, w_ref[:, :tn])
    b = _dot(xn, w_ref[:, tn:])
    h_ref[...] = (a * _sigmoid(a) * b).astype(BF16)


def _ffn_up(xn, wi, layer, *, tm=FFN_ROW_TILE, tn=COL_TILE):
    m, d = xn.shape
    d_ff = wi.shape[-1] // 2
    nj = d_ff // tn
    return pl.pallas_call(
        _ffn_up_kernel,
        out_shape=jax.ShapeDtypeStruct((m, d_ff), BF16),
        grid=(nj, m // tm),
        in_specs=[
            pl.BlockSpec((tm, d), lambda j, i: (i, 0)),
            pl.BlockSpec((None, d, tn), lambda j, i: (layer, 0, j)),
            pl.BlockSpec((None, d, tn), lambda j, i: (layer, 0, j + nj)),
        ],
        out_specs=pl.BlockSpec((tm, tn), lambda j, i: (i, j)),
        scratch_shapes=[pltpu.VMEM((d, 2 * tn), BF16)],
        compiler_params=_params("parallel", "arbitrary"),
        name="ffn_up",
    )(xn, wi, wi)


def _ffn_down_kernel(h_ref, w_ref, r_ref, g_ref, *o_refs, last):
    y = r_ref[...] + 0.5 * _dot(h_ref[...], w_ref[...])
    if last:
        o_refs[0][...] = _rms_scaled(y, g_ref[...])
    else:
        o_refs[0][...] = y
        o_refs[1][...] = _rms_scaled(y, g_ref[...]).astype(BF16)


def _ffn_down(h, wo, res, g_next, layer, g_layer, *, last=False, tm=256):
    m, d_ff = h.shape
    d = res.shape[-1]
    rows = pl.BlockSpec((tm, d), lambda i: (i, 0))
    x_out = jax.ShapeDtypeStruct((m, d), F32)
    return pl.pallas_call(
        functools.partial(_ffn_down_kernel, last=last),
        out_shape=x_out if last else (x_out, jax.ShapeDtypeStruct((m, d), BF16)),
        grid=(m // tm,),
        in_specs=[
            pl.BlockSpec((tm, d_ff), lambda i: (i, 0)),
            pl.BlockSpec((None, d_ff, d), lambda i: (layer, 0, 0), pipeline_mode=pl.Buffered(1)),
            rows,
            pl.BlockSpec((None, 1, d), lambda i: (g_layer, 0, 0)),
        ],
        out_specs=rows if last else (rows, rows),
        compiler_params=_params("parallel"),
        name="ffn_down",
    )(h, wo, res, g_next)


def _in_proj_kernel(xn_ref, wgm_ref, wr_ref, wg_ref, sc_ref, of_ref, ob_ref, og_ref, *, n_gm_tiles, n_f32_tiles):
    j = pl.program_id(1)

    @pl.when(j == 0)
    def _():
        og_ref[...] = _dot(xn_ref[...], wg_ref[...])

    @pl.when(j < n_gm_tiles)
    def _():
        of_ref[...] = _dot(xn_ref[...], wgm_ref[...])

    @pl.when((j >= n_gm_tiles) & (j < n_f32_tiles))
    def _():
        of_ref[...] = _dot(xn_ref[...], wr_ref[...])

    @pl.when(j >= n_f32_tiles)
    def _():
        ob_ref[...] = (_dot(xn_ref[...], wr_ref[...]) * sc_ref[...]).astype(BF16)


def _in_proj(xn, w_gm, w_rest, w_gate, col_scale, layer, *, tm=ROW_TILE, tn=2 * COL_TILE):
    m, d = xn.shape
    ng = w_gate.shape[-1]
    ngm = w_gm.shape[-1] // tn
    nf = N_F32_COLS // tn
    nb = N_BF16_COLS // tn
    assert w_gm.shape[-1] + w_rest.shape[-1] == N_F32_COLS + N_BF16_COLS and ngm < nf
    return pl.pallas_call(
        functools.partial(_in_proj_kernel, n_gm_tiles=ngm, n_f32_tiles=nf),
        out_shape=(jax.ShapeDtypeStruct((m, N_F32_COLS), F32), jax.ShapeDtypeStruct((m, N_BF16_COLS), BF16),
                   jax.ShapeDtypeStruct((m, ng), F32)),
        grid=(m // tm, nf + nb),
        in_specs=[
            pl.BlockSpec((tm, d), lambda i, j: (i, 0)),
            pl.BlockSpec((None, d, tn), lambda i, j: (layer, 0, jnp.minimum(j, ngm - 1))),
            pl.BlockSpec((None, d, tn), lambda i, j: (layer, 0, jnp.maximum(j - ngm, 0))),
            pl.BlockSpec((None, d, ng), lambda i, j: (layer, 0, 0)),
            pl.BlockSpec((1, tn), lambda i, j: (0, j)),
        ],
        out_specs=(
            pl.BlockSpec((tm, tn), lambda i, j: (i, jnp.minimum(j, nf - 1))),
            pl.BlockSpec((tm, tn), lambda i, j: (i, jnp.maximum(j - nf, 0))),
            pl.BlockSpec((tm, ng), lambda i, j: (i, 0)),
        ),
        compiler_params=_params("parallel", "arbitrary"),
        name="in_proj",
    )(xn, w_gm, w_rest, w_gate, col_scale)


def _dil_heads(q, kc, vc, kp, vp, first):
    r = lax.broadcasted_iota(jnp.int32, (Q_BLOCK, Q_BLOCK), 0)
    c = lax.broadcasted_iota(jnp.int32, (Q_BLOCK, Q_BLOCK), 1)
    bias_cur = jnp.where(c <= r, 0.0, NEG)
    if kp is not None:
        bias_prev = jnp.where(c >= r + jnp.where(first, Q_BLOCK, 0), 0.0, NEG)
    head = lax.shift_right_logical(lax.broadcasted_iota(jnp.int32, (Q_BLOCK, A_GROUP_WIDTH), 1),
                                   int(np.log2(A_HEAD_DIM)))
    o_acc = jnp.zeros((Q_BLOCK, A_GROUP_WIDTH), F32)
    l_acc = jnp.zeros((Q_BLOCK, A_GROUP_WIDTH), F32)
    for h in range(A_HEADS_PER_GROUP):
        mine = head == h
        qh = q * jnp.where(mine, 1.0, 0.0).astype(BF16)
        s_c = _dot_nt(qh, kc) + bias_cur
        m = jnp.max(s_c, axis=-1, keepdims=True)
        if kp is not None:
            s_p = _dot_nt(qh, kp) + bias_prev
            m = jnp.maximum(m, jnp.max(s_p, axis=-1, keepdims=True))
        p_c = jnp.exp(s_c - m)
        l = jnp.sum(p_c, axis=-1, keepdims=True)
        pv = _dot(p_c.astype(BF16), vc)
        if kp is not None:
            p_p = jnp.exp(s_p - m)
            l = l + jnp.sum(p_p, axis=-1, keepdims=True)
            pv = pv + _dot(p_p.astype(BF16), vp)
        o_acc = jnp.where(mine, pv / l, o_acc)
        l_acc = jnp.where(mine, m + jnp.log(l), l_acc)
    return o_acc, l_acc


def _dil_attn_dense_kernel(q_ref, kc_ref, kp_ref, vc_ref, vp_ref, o_ref, l_ref):
    for u in range(DIL_TILES):
        rs = slice(u * Q_BLOCK, (u + 1) * Q_BLOCK)
        ps = slice((u - 1) * Q_BLOCK, u * Q_BLOCK)
        kp, vp = (kp_ref[...], vp_ref[...]) if u == 0 else (kc_ref[ps, :], vc_ref[ps, :])
        first = (pl.program_id(1) == 0) if u == 0 else False
        o, lse = _dil_heads(q_ref[rs, :], kc_ref[rs, :], vc_ref[rs, :], kp, vp, first)
        for half in range(A_GROUP_WIDTH // LANES):
            o_ref[half, rs, :] = o[:, half * LANES:(half + 1) * LANES]
            l_ref[half, rs, :] = lse[:, half * LANES:(half + 1) * LANES]


def _dil_attn_strided_kernel(q_ref, k_ref, v_ref, o_ref, l_ref, qf_ref, kf_ref, vf_ref, *, dil, has_prev):
    i = pl.program_id(1)
    step = pl.program_id(2)
    halves = A_GROUP_WIDTH // LANES
    slot = lax.rem(i, 2) if has_prev else 0

    @pl.when(step == 0)
    def _():
        for half in range(halves):
            cs = slice(half * LANES, (half + 1) * LANES)
            qf_ref[half] = q_ref[:, cs].astype(F32)
            kf_ref[slot, half] = k_ref[:, cs].astype(F32)
            vf_ref[slot, half] = v_ref[:, cs].astype(F32)

    if has_prev:
        @pl.when((step == 0) & (i == 0))
        def _():
            kf_ref[1] = jnp.zeros(kf_ref.shape[1:], F32)
            vf_ref[1] = jnp.zeros(vf_ref.shape[1:], F32)

    for u in range(DIL_TILES):
        r = step * DIL_TILES + u

        def rows_of_class(ref, *lead, r=r):
            return jnp.concatenate([ref[(*lead, half, pl.ds(r, Q_BLOCK, stride=dil), slice(None))]
                                    for half in range(halves)], axis=1).astype(BF16)

        kp = rows_of_class(kf_ref, 1 - slot) if has_prev else None
        vp = rows_of_class(vf_ref, 1 - slot) if has_prev else None
        o, lse = _dil_heads(rows_of_class(qf_ref), rows_of_class(kf_ref, slot), rows_of_class(vf_ref, slot),
                            kp, vp, i == 0)
        for half in range(halves):
            o_ref[half, pl.ds(r, Q_BLOCK, stride=dil), :] = o[:, half * LANES:(half + 1) * LANES]
            l_ref[half, pl.ds(r, Q_BLOCK, stride=dil), :] = lse[:, half * LANES:(half + 1) * LANES]


def _dil_attn(cb3, gi):
    bsz, s, _ = cb3.shape
    dil = A_GROUPS[gi][1]
    w = A_GROUP_WIDTH
    halves = w // LANES
    out = jax.ShapeDtypeStruct((bsz, halves, s, LANES), F32)
    name = f"dilated_attention_g{gi}"
    if dil == 1:
        step_rows = Q_BLOCK * DIL_TILES

        def col(off, prev):
            if prev:
                return pl.BlockSpec((None, Q_BLOCK, w),
                                    lambda b, i: (b, jnp.maximum(i * DIL_TILES - 1, 0), (off + gi * w) // w))
            return pl.BlockSpec((None, step_rows, w), lambda b, i: (b, i, (off + gi * w) // w))

        ospec = pl.BlockSpec((None, halves, step_rows, LANES), lambda b, i: (b, 0, i, 0))
        return pl.pallas_call(
            _dil_attn_dense_kernel,
            out_shape=(out, out),
            grid=(bsz, s // step_rows),
            in_specs=[col(BOFF_QA, False), col(BOFF_KA, False), col(BOFF_KA, True), col(BOFF_VA, False), col(BOFF_VA, True)],
            out_specs=(ospec, ospec),
            compiler_params=_params("parallel", "arbitrary"),
            name=name,
        )(cb3, cb3, cb3, cb3, cb3)

    rows = Q_BLOCK * dil
    tiles = s // rows
    has_prev = tiles > 1

    def col(off):
        return pl.BlockSpec((None, rows, w), lambda b, i, r: (b, i, (off + gi * w) // w))

    ospec = pl.BlockSpec((None, halves, rows, LANES), lambda b, i, r: (b, 0, i, 0))
    return pl.pallas_call(
        functools.partial(_dil_attn_strided_kernel, dil=dil, has_prev=has_prev),
        out_shape=(out, out),
        grid=(bsz, tiles, dil // DIL_TILES),
        in_specs=[col(BOFF_QA), col(BOFF_KA), col(BOFF_VA)],
        out_specs=(ospec, ospec),
        scratch_shapes=[pltpu.VMEM((halves, rows, LANES), F32),
                        pltpu.VMEM((2 if has_prev else 1, halves, rows, LANES), F32),
                        pltpu.VMEM((2 if has_prev else 1, halves, rows, LANES), F32)],
        compiler_params=_params("parallel", "arbitrary", "arbitrary"),
        name=name,
    )(cb3, cb3, cb3)


def _pool_kernel(x_ref, w_ref, sc_ref, o_ref):
    s = x_ref.shape[0]
    t = lax.broadcasted_iota(jnp.int32, (s, B_GROUP_DIM), 0)
    for gi, win in enumerate(B_WINDOWS):
        cs = slice(gi * B_GROUP_DIM, (gi + 1) * B_GROUP_DIM)
        x = x_ref[:, cs]
        acc = x
        k = 1
        while k < win:
            acc = acc + jnp.where(t >= k, pltpu.roll(acc, k, axis=0), 0.0)
            k *= 2
        cnt = jnp.minimum(t + 1, win).astype(F32)
        z = acc / cnt - x
        o_ref[:, cs] = _dot(z.astype(BF16), w_ref[gi]) * sc_ref[:, cs]


def _pool_mixer(cf3, pool_w, pool_scale, layer):
    bsz, s, _ = cf3.shape
    ng = len(B_WINDOWS)
    return pl.pallas_call(
        _pool_kernel,
        out_shape=jax.ShapeDtypeStruct((bsz, s, B_WIDTH), F32),
        grid=(bsz,),
        in_specs=[
            pl.BlockSpec((None, s, B_WIDTH), lambda b: (b, 0, FOFF_XB // B_WIDTH)),
            pl.BlockSpec((None, ng, B_GROUP_DIM, B_GROUP_DIM), lambda b: (layer, 0, 0, 0)),
            pl.BlockSpec((None, 1, B_WIDTH), lambda b: (layer, 0, 0)),
        ],
        out_specs=pl.BlockSpec((None, s, B_WIDTH), lambda b: (b, 0, 0)),
        compiler_params=_params("parallel"),
        name="pool_mixer",
    )(cf3, pool_w, pool_scale)


def _gelu_tanh(x):
    return 0.5 * x * (1.0 + jnp.tanh(np.float32(np.sqrt(2.0 / np.pi)) * (x + 0.044715 * (x * x * x))))


def _compress_one(z_ref, pe_ref, w1_ref, w2_ref, o_ref):
    nch = z_ref.shape[0] // CMP_STRIDE
    dh = C_HEAD_DIM
    first = jnp.zeros((nch, dh), F32)
    second = jnp.zeros((nch, dh), F32)
    for p in range(CMP_STRIDE):
        zp = z_ref[pl.ds(p, nch, stride=CMP_STRIDE), :]
        a = (zp + pe_ref[p:p + 1, :]).astype(BF16)
        b = (zp + pe_ref[CMP_STRIDE + p:CMP_STRIDE + p + 1, :]).astype(BF16)
        first = first + _dot(a, w1_ref[p * dh:(p + 1) * dh, :])
        second = second + _dot(b, w1_ref[(CMP_STRIDE + p) * dh:(CMP_STRIDE + p + 1) * dh, :])
    pre = first + pltpu.roll(second, nch - 1, axis=0)
    o_ref[...] = _dot(_gelu_tanh(pre).astype(BF16), w2_ref[...]).astype(BF16)


def _compress_kernel(zk_ref, zv_ref, pek_ref, w1k_ref, w2k_ref, pev_ref, w1v_ref, w2v_ref, kc_ref, vc_ref):
    _compress_one(zk_ref, pek_ref, w1k_ref, w2k_ref, kc_ref)
    _compress_one(zv_ref, pev_ref, w1v_ref, w2v_ref, vc_ref)


def _compress(cf3, pe_k, w1_k, w2_k, pe_v, w1_v, w2_v, layer):
    bsz, s, _ = cf3.shape
    nch = s // CMP_STRIDE
    dh = C_HEAD_DIM
    cdim = CMP_LEN * dh

    def col(off):
        return pl.BlockSpec((None, s, dh), lambda b, g: (b, 0, off // dh + g))

    def wspec(shape):
        return pl.BlockSpec((None,) + shape, lambda b, g: (layer,) + (0,) * len(shape))

    out = jax.ShapeDtypeStruct((bsz, C_KV_GROUPS, nch, dh), BF16)
    ospec = pl.BlockSpec((None, None, nch, dh), lambda b, g: (b, g, 0, 0))
    return pl.pallas_call(
        _compress_kernel,
        out_shape=(out, out),
        grid=(bsz, C_KV_GROUPS),
        in_specs=[col(FOFF_KCMP), col(FOFF_VCMP),
                  wspec((CMP_LEN, dh)), wspec((cdim, dh)), wspec((dh, dh)),
                  wspec((CMP_LEN, dh)), wspec((cdim, dh)), wspec((dh, dh))],
        out_specs=(ospec, ospec),
        compiler_params=_params("parallel", "parallel"),
        name="nsa_compress",
    )(cf3, cf3, pe_k, w1_k, w2_k, pe_v, w1_v, w2_v)


def _split3(x):
    hi = x.astype(BF16)
    r1 = x - hi.astype(F32)
    mid = r1.astype(BF16)
    lo = (r1 - mid.astype(F32)).astype(BF16)
    return hi, mid, lo


def _softmax_pv(s, v):
    m = jnp.max(s, axis=-1, keepdims=True)
    p = jnp.exp(s - m)
    return _dot(p.astype(BF16), v) / jnp.sum(p, axis=-1, keepdims=True)


def _nsa_tile(t0, q, gates, kc_ref, vc_ref, kw_ref, vw_ref, ovt_ref, nb):
    hg = C_HEADS_PER_GROUP
    dh = C_HEAD_DIM
    row = t0 + lax.broadcasted_iota(jnp.int32, (Q_BLOCK, LANES), 0)
    lane = lax.broadcasted_iota(jnp.int32, (Q_BLOCK, LANES), 1)

    q3 = jnp.concatenate([q[:, h * dh:(h + 1) * dh] for h in range(hg)], axis=0)
    rows = [slice(h * Q_BLOCK, (h + 1) * Q_BLOCK) for h in range(hg)]
    gt = _sigmoid(gates)

    cvalid = lane * CMP_STRIDE + (CMP_LEN - 1) <= row
    s3 = _dot_nt(q3, kc_ref[...])
    vc = vc_ref[...]
    out = []
    psum = jnp.zeros((Q_BLOCK, LANES), F32)
    for h in range(hg):
        s = jnp.where(cvalid, s3[rows[h]], NEG)
        p = jnp.exp(s - jnp.max(s, axis=-1, keepdims=True))
        p = jnp.where(cvalid, p / jnp.sum(p, axis=-1, keepdims=True), 0.0)
        out.append(gt[:, 3 * h:3 * h + 1] * _dot(p.astype(BF16), vc))
        psum = psum + p

    ovt = ovt_ref[...]
    imp_t = sum(_dot_nt(ovt, part) for part in _split3(psum))
    blk = lax.broadcasted_iota(jnp.int32, (nb, Q_BLOCK), 0)
    tb = lax.shift_right_logical(t0 + lax.broadcasted_iota(jnp.int32, (nb, Q_BLOCK), 1), int(np.log2(SEL_BLOCK)))
    forced = (blk == 0) | (blk == tb) | (blk == tb - 1)
    score = jnp.where(blk > tb, -1.0, imp_t + jnp.where(forced, FORCE_BONUS, 0.0))
    blk_f = blk.astype(F32)
    sel_t = jnp.zeros((nb, Q_BLOCK), F32)
    for _ in range(min(SEL_TOPN, nb)):
        mx = jnp.max(score, axis=0, keepdims=True)
        idx = jnp.min(jnp.where(score == mx, blk_f, float(nb)), axis=0, keepdims=True)
        hit = blk_f == idx
        sel_t = jnp.where(hit, 1.0, sel_t)
        score = jnp.where(hit, -3.0, score)
    sel = jnp.concatenate([sel_t, jnp.zeros((LANES - nb, Q_BLOCK), F32)], axis=0).T
    sel = jnp.where(sel > 0.5, 0.0, NEG).astype(BF16)

    wlen = WIN + Q_BLOCK
    w0 = pl.multiple_of(jnp.maximum(t0 - WIN, 0), Q_BLOCK)
    kpos = w0 + lax.broadcasted_iota(jnp.int32, (Q_BLOCK, wlen), 1)
    qpos = t0 + lax.broadcasted_iota(jnp.int32, (Q_BLOCK, wlen), 0)
    bias = jnp.where((kpos <= qpos) & (kpos > qpos - WIN), 0.0, NEG)
    s3 = _dot_nt(q3, kw_ref[pl.ds(w0, wlen), :])
    vw = vw_ref[pl.ds(w0, wlen), :]
    for h in range(hg):
        out[h] = out[h] + gt[:, 3 * h + 2:3 * h + 3] * _softmax_pv(s3[rows[h]] + bias, vw)
    return q3, gt, out, sel


def _nsa_kernel(q_ref, kc_ref, vc_ref, ks_ref, vs_ref, kw_ref, vw_ref, gt_ref, ovt_ref, ex_ref, o_ref, *, s_len):
    n = pl.program_id(2)
    hg = C_HEADS_PER_GROUP
    dh = C_HEAD_DIM
    tiles = []
    for u in range(NSA_TILES):
        rs = slice(u * Q_BLOCK, (u + 1) * Q_BLOCK)
        t0 = (n * NSA_TILES + u) * Q_BLOCK
        tiles.append((rs, t0) + _nsa_tile(t0, q_ref[rs, :], gt_ref[rs, :], kc_ref, vc_ref, kw_ref, vw_ref, ovt_ref,
                                          s_len // SEL_BLOCK))

    steps_per_chunk = SEL_CHUNK // (Q_BLOCK * NSA_TILES)
    for cls in range(s_len // SEL_CHUNK):
        klen = (cls + 1) * SEL_CHUNK

        @pl.when(n // steps_per_chunk == cls)
        def _(klen=klen):
            kpos = klen - SEL_CHUNK + lax.broadcasted_iota(jnp.int32, (Q_BLOCK, SEL_CHUNK), 1)
            vs = vs_ref[:klen, :]
            for rs, t0, q3, gt, out, sel in tiles:
                qpos = t0 + lax.broadcasted_iota(jnp.int32, (Q_BLOCK, SEL_CHUNK), 0)
                causal = jnp.where(kpos <= qpos, 0.0, NEG)
                blocked = _dot(sel, ex_ref[:, :klen])
                bias = blocked[:, klen - SEL_CHUNK:] + causal
                if klen > SEL_CHUNK:
                    bias = jnp.concatenate([blocked[:, :klen - SEL_CHUNK], bias], axis=1)
                s3 = _dot_nt(q3, ks_ref[:klen, :])
                for h in range(hg):
                    hs = slice(h * Q_BLOCK, (h + 1) * Q_BLOCK)
                    o_ref[rs, h * dh:(h + 1) * dh] = out[h] + gt[:, 3 * h + 1:3 * h + 2] * _softmax_pv(s3[hs] + bias, vs)


def _nsa_constants(s):
    nch = s // CMP_STRIDE
    nb = s // SEL_BLOCK
    n_cmp = (s - CMP_LEN) // CMP_STRIDE + 1
    ci = np.arange(nch)[None, :] * CMP_STRIDE
    bj = np.arange(nb)[:, None] * SEL_BLOCK
    overlap_t = (ci < bj + SEL_BLOCK) & (ci + CMP_LEN > bj) & (np.arange(nch)[None, :] < n_cmp)
    expand = np.arange(LANES)[:, None] == (np.arange(s)[None, :] // SEL_BLOCK)
    return jnp.asarray(overlap_t, BF16), jnp.asarray(expand, BF16)


def _nsa(cb3, kc, vc, gates3):
    bsz, s, _ = cb3.shape
    nq = s // Q_BLOCK
    nch = s // CMP_STRIDE
    nb = s // SEL_BLOCK
    dh = C_HEAD_DIM
    assert nch == LANES and nb <= LANES and nb % 8 == 0 and s % SEL_CHUNK == 0 and s >= WIN + Q_BLOCK
    step_rows = Q_BLOCK * NSA_TILES
    assert SEL_CHUNK % step_rows == 0
    overlap_t, expand = _nsa_constants(s)

    def kv(off):
        return pl.BlockSpec((None, s, dh), lambda b, g, n: (b, 0, off // dh + g))

    cmp_spec = pl.BlockSpec((None, None, nch, dh), lambda b, g, n: (b, g, 0, 0))
    return pl.pallas_call(
        functools.partial(_nsa_kernel, s_len=s),
        out_shape=jax.ShapeDtypeStruct((bsz, s, C_WIDTH), F32),
        grid=(bsz, C_KV_GROUPS, s // step_rows),
        in_specs=[
            pl.BlockSpec((None, step_rows, C_GROUP_WIDTH), lambda b, g, n: (b, n, BOFF_QC // C_GROUP_WIDTH + g)),
            cmp_spec, cmp_spec,
            kv(BOFF_KSLC), kv(BOFF_VSLC), kv(BOFF_KWIN), kv(BOFF_VWIN),
            pl.BlockSpec((None, step_rows, LANES), lambda b, g, n: (b, n, g)),
            pl.BlockSpec((nb, nch), lambda b, g, n: (0, 0)),
            pl.BlockSpec((LANES, s), lambda b, g, n: (0, 0)),
        ],
        out_specs=pl.BlockSpec((None, step_rows, C_GROUP_WIDTH), lambda b, g, n: (b, n, g)),
        compiler_params=_params("parallel", "parallel", "arbitrary"),
        name="nsa_attention",
    )(cb3, kc, vc, cb3, cb3, cb3, cb3, gates3, overlap_t, expand)


def _merge_kernel(*refs):
    ng = len(A_GROUPS)
    oa_refs, la_refs = refs[:ng], refs[ng:2 * ng]
    zb_ref, oc_ref, g0_ref, g1_ref, g2_ref, x_ref, pa_ref, pb_ref, pc_ref, wo_ref, gn_ref, o_ref, xn_ref = refs[2 * ng:]

    def lanes(ref):
        return jnp.concatenate([ref[half] for half in range(ref.shape[0])], axis=1)

    lse = [lanes(r) for r in la_refs]
    m = functools.reduce(jnp.maximum, lse)
    e = [jnp.exp(l - m) for l in lse]
    oa = sum(w * lanes(r) for w, r in zip(e, oa_refs)) / sum(e)
    ya = _dot(oa.astype(BF16), pa_ref[...])
    yb = _dot(zb_ref[...].astype(BF16), pb_ref[...])
    yc = _dot(oc_ref[...].astype(BF16), pc_ref[...])
    mix = _sigmoid(g0_ref[...]) * ya + _sigmoid(g1_ref[...]) * yb + _sigmoid(g2_ref[...]) * yc
    y = x_ref[...] + _dot(mix.astype(BF16), wo_ref[...])
    o_ref[...] = y
    xn_ref[...] = _rms_scaled(y, gn_ref[...]).astype(BF16)


def _merge(oa, la, zb, oc, cf, x, proj_a, proj_b, proj_c, w_out, g_next, layer, *, tm=256):
    m, d = x.shape
    assert d == D_GATE

    def rows(w):
        return pl.BlockSpec((tm, w), lambda i: (i, 0))

    def gate(k):
        return pl.BlockSpec((tm, d), lambda i: (i, FOFF_GM // d + k))

    def weight(k):
        return pl.BlockSpec((None, k, d), lambda i: (layer, 0, 0), pipeline_mode=pl.Buffered(1))

    _, halves, s, _ = oa[0].shape
    per_seq = s // tm
    group = pl.BlockSpec((None, halves, tm, LANES), lambda i: (i // per_seq, 0, i % per_seq, 0))

    return pl.pallas_call(
        _merge_kernel,
        out_shape=(jax.ShapeDtypeStruct((m, d), F32), jax.ShapeDtypeStruct((m, d), BF16)),
        grid=(m // tm,),
        in_specs=[group] * (2 * len(A_GROUPS))
        + [rows(B_WIDTH), rows(C_WIDTH), gate(0), gate(1), gate(2), rows(d),
           weight(A_GROUP_WIDTH), weight(B_WIDTH), weight(C_WIDTH), weight(d),
           pl.BlockSpec((None, 1, d), lambda i: (layer, 0, 0))],
        out_specs=(rows(d), rows(d)),
        compiler_params=_params("parallel"),
        name="merge",
    )(*oa, *la, zb, oc, cf, cf, cf, x, proj_a, proj_b, proj_c, w_out, g_next)


_A3 = 3 * A_WIDTH
_XB_END = _A3 + B_WIDTH
_QC_END = _XB_END + C_WIDTH
_CMP_END = _QC_END + 2 * C_KV_WIDTH
_KV_END = _QC_END + 6 * C_KV_WIDTH
_GM_START = _KV_END + N_GATES
_REST_RANGES = ((_A3, _XB_END), (_QC_END, _CMP_END), (0, _A3), (_XB_END, _QC_END), (_CMP_END, _KV_END))


def _reorder_w_in(w_in):
    w_gm = w_in[..., _GM_START:].astype(BF16)
    w_rest = jnp.concatenate([w_in[..., lo:hi] for lo, hi in _REST_RANGES], axis=-1).astype(BF16)
    per_group = 3 * C_HEADS_PER_GROUP
    pad = jnp.zeros(w_in.shape[:-1] + (LANES - per_group,), w_in.dtype)
    w_gate = jnp.concatenate(
        [piece for g in range(C_KV_GROUPS)
         for piece in (w_in[..., _KV_END + g * per_group:_KV_END + (g + 1) * per_group], pad)], axis=-1).astype(BF16)
    col_scale = np.ones((1, N_F32_COLS + N_BF16_COLS), np.float32)
    col_scale[:, N_F32_COLS + BOFF_QA:N_F32_COLS + BOFF_QA + A_WIDTH] = A_HEAD_DIM ** -0.5
    col_scale[:, N_F32_COLS + BOFF_QC:N_F32_COLS + BOFF_QC + C_WIDTH] = C_HEAD_DIM ** -0.5
    return w_gm, w_rest, w_gate, jnp.asarray(col_scale)


def _mixing(x, xn, layer, w_gm, w_rest, w_gate, col_scale, pool_w, pool_scale, pe_k, w1_k, w2_k, pe_v, w1_v, w2_v,
            proj_a, proj_b, proj_c, w_out, g_next, bsz, s):
    m, d = x.shape
    cf, cb, gates = _in_proj(xn, w_gm, w_rest, w_gate, col_scale, layer)
    cf3 = cf.reshape(bsz, s, N_F32_COLS)
    cb3 = cb.reshape(bsz, s, N_BF16_COLS)
    gates3 = gates.reshape(bsz, s, C_KV_GROUPS * LANES)

    oa, la = zip(*[_dil_attn(cb3, gi) for gi in range(len(A_GROUPS))])
    zb = _pool_mixer(cf3, pool_w, pool_scale, layer).reshape(m, B_WIDTH)
    kc, vc = _compress(cf3, pe_k, w1_k, w2_k, pe_v, w1_v, w2_v, layer)
    oc = _nsa(cb3, kc, vc, gates3).reshape(m, C_WIDTH)
    return _merge(oa, la, zb, oc, cf, x, proj_a, proj_b, proj_c, w_out, g_next, layer)


def kernel(x, ffn1_norm, ffn1_wi, ffn1_wo, mix_norm, w_in, pool_w, pool_scale, cmp_pe_k, cmp_w1_k, cmp_w2_k,
           cmp_pe_v, cmp_w1_v, cmp_w2_v, proj_a, proj_b, proj_c, w_out, ffn2_norm, ffn2_wi, ffn2_wo, final_norm):
    bsz, s, d = x.shape
    depth = ffn1_wi.shape[0]
    for win, dil in A_GROUPS:
        assert win // dil == Q_BLOCK and s % (dil * Q_BLOCK) == 0
    assert CMP_LEN == 2 * CMP_STRIDE and all(w & (w - 1) == 0 for w in B_WINDOWS)

    bf = lambda w: w.astype(BF16)
    row3 = lambda g: g.reshape(g.shape[0], 1, g.shape[-1])
    w_gm, w_rest, w_gate, col_scale = _reorder_w_in(w_in)
    ffn1_wo, ffn2_wo = bf(ffn1_wo), bf(ffn2_wo)
    pool_w, cmp_w1_k, cmp_w2_k, cmp_w1_v, cmp_w2_v = bf(pool_w), bf(cmp_w1_k), bf(cmp_w2_k), bf(cmp_w1_v), bf(cmp_w2_v)
    proj_a, proj_b, proj_c, w_out = bf(proj_a), bf(proj_b), bf(proj_c), bf(w_out)
    ffn1_norm, mix_norm, ffn2_norm, pool_scale = row3(ffn1_norm), row3(mix_norm), row3(ffn2_norm), row3(pool_scale)

    final_norm = final_norm.reshape(1, 1, d)
    x = x.reshape(bsz * s, d)
    xn = _row_norm(x, ffn1_norm, 0)
    for layer in range(depth):
        x, xn = _ffn_down(_ffn_up(xn, ffn1_wi, layer), ffn1_wo, x, mix_norm, layer, layer)
        x, xn = _mixing(x, xn, layer, w_gm, w_rest, w_gate, col_scale, pool_w, pool_scale, cmp_pe_k, cmp_w1_k, cmp_w2_k,
                        cmp_pe_v, cmp_w1_v, cmp_w2_v, proj_a, proj_b, proj_c, w_out, ffn2_norm, bsz, s)
        h = _ffn_up(xn, ffn2_wi, layer)
        if layer + 1 < depth:
            x, xn = _ffn_down(h, ffn2_wo, x, ffn1_norm, layer, layer + 1)
        else:
            x = _ffn_down(h, ffn2_wo, x, final_norm, layer, 0, last=True)
    return x.reshape(bsz, s, d)
```

```python
import functools

import numpy as np
import jax
import jax.numpy as jnp
from jax import lax
from jax.experimental import pallas as pl
from jax.experimental.pallas import tpu as pltpu

F32 = jnp.float32
BF16 = jnp.bfloat16

EPS = 1e-6
NEG = -1e30
LOG2E = float(np.log2(np.e))
Q_BLOCK = 128
LANES = 128

A_GROUPS = ((128, 1), (512, 4), (2048, 16))
A_HEADS_PER_GROUP = 4
A_HEAD_DIM = 64
A_GROUP_WIDTH = A_HEADS_PER_GROUP * A_HEAD_DIM
A_WIDTH = A_GROUP_WIDTH * len(A_GROUPS)

B_WINDOWS = (2, 4, 8, 16)
B_GROUP_DIM = 128
B_WIDTH = B_GROUP_DIM * len(B_WINDOWS)

C_KV_GROUPS = 2
C_HEADS_PER_GROUP = 3
C_HEADS = C_KV_GROUPS * C_HEADS_PER_GROUP
C_HEAD_DIM = 128
C_GROUP_WIDTH = C_HEADS_PER_GROUP * C_HEAD_DIM
C_WIDTH = C_HEADS * C_HEAD_DIM
C_KV_WIDTH = C_KV_GROUPS * C_HEAD_DIM
CMP_LEN = 32
CMP_STRIDE = 16
SEL_BLOCK = 64
SEL_TOPN = 8
FORCE_BONUS = 100.0
WIN = 512
N_GATES = 3 * C_HEADS
NSA_TILES = 4
DIL_TILES = 4

N_BRANCH = 3
D_GATE = 2048

FOFF_GM = 0
FOFF_XB = N_BRANCH * D_GATE
FOFF_KCMP = FOFF_XB + B_WIDTH
FOFF_VCMP = FOFF_KCMP + C_KV_WIDTH
N_F32_COLS = FOFF_VCMP + C_KV_WIDTH
BOFF_QA = 0
BOFF_KA = BOFF_QA + A_WIDTH
BOFF_VA = BOFF_KA + A_WIDTH
BOFF_QC = BOFF_VA + A_WIDTH
BOFF_KSLC = BOFF_QC + C_WIDTH
BOFF_VSLC = BOFF_KSLC + C_KV_WIDTH
BOFF_KWIN = BOFF_VSLC + C_KV_WIDTH
BOFF_VWIN = BOFF_KWIN + C_KV_WIDTH
N_BF16_COLS = BOFF_VWIN + C_KV_WIDTH

VMEM_LIMIT = 56 * 1024 * 1024
ROW_TILE = 1024
COL_TILE = 512
FFN_ROW_TILE = 1024

def _params(*sem):
    return pltpu.CompilerParams(dimension_semantics=sem, vmem_limit_bytes=VMEM_LIMIT)


def _dot(a, b):
    return jnp.dot(a, b, preferred_element_type=F32)


def _dot_nt(a, b):
    return lax.dot_general(a, b, (((1,), (1,)), ((), ())), preferred_element_type=F32)


def _sigmoid(x):
    return 1.0 / (1.0 + jnp.exp(-x))


def _rms_scaled(x, g):
    return x * lax.rsqrt(jnp.mean(x * x, axis=-1, keepdims=True) + EPS) * g


def _row_norm_kernel(x_ref, g_ref, o_ref):
    o_ref[...] = _rms_scaled(x_ref[...], g_ref[...]).astype(o_ref.dtype)


def _row_norm(x, g, layer, *, tm=512):
    m, d = x.shape
    return pl.pallas_call(
        _row_norm_kernel,
        out_shape=jax.ShapeDtypeStruct((m, d), BF16),
        grid=(m // tm,),
        in_specs=[pl.BlockSpec((tm, d), lambda i: (i, 0)), pl.BlockSpec((None, 1, d), lambda i: (layer, 0, 0))],
        out_specs=pl.BlockSpec((tm, d), lambda i: (i, 0)),
        compiler_params=_params("parallel"),
        name="row_norm",
    )(x, g)


def _ffn_up_kernel(xn_ref, wa_ref, wb_ref, h_ref, w_ref):
    tn = h_ref.shape[-1]

    @pl.when(pl.program_id(1) == 0)
    def _():
        w_ref[:, :tn] = wa_ref[...].astype(BF16)
        w_ref[:, tn:] = wb_ref[...].astype(BF16)

    xn = xn_ref[...]
    a = _dot(xn, w_ref[:, :tn])
    b = _dot(xn, w_ref[:, tn:])
    h_ref[...] = (a * _sigmoid(a) * b).astype(BF16)


def _ffn_up(xn, wi, layer, *, tm=FFN_ROW_TILE, tn=COL_TILE):
    m, d = xn.shape
    d_ff = wi.shape[-1] // 2
    nj = d_ff // tn
    return pl.pallas_call(
        _ffn_up_kernel,
        out_shape=jax.ShapeDtypeStruct((m, d_ff), BF16),
        grid=(nj, m // tm),
        in_specs=[
            pl.BlockSpec((tm, d), lambda j, i: (i, 0)),
            pl.BlockSpec((None, d, tn), lambda j, i: (layer, 0, j)),
            pl.BlockSpec((None, d, tn), lambda j, i: (layer, 0, j + nj)),
        ],
        out_specs=pl.BlockSpec((tm, tn), lambda j, i: (i, j)),
        scratch_shapes=[pltpu.VMEM((d, 2 * tn), BF16)],
        compiler_params=_params("parallel", "arbitrary"),
        name="ffn_up",
    )(xn, wi, wi)


def _ffn_down_kernel(h_ref, w_ref, r_ref, g_ref, *o_refs, last):
    y = r_ref[...] + 0.5 * _dot(h_ref[...], w_ref[...])
    if last:
        o_refs[0][...] = _rms_scaled(y, g_ref[...])
    else:
        o_refs[0][...] = y
        o_refs[1][...] = _rms_scaled(y, g_ref[...]).astype(BF16)


def _ffn_down(h, wo, res, g_next, layer, g_layer, *, last=False, tm=256):
    m, d_ff = h.shape
    d = res.shape[-1]
    rows = pl.BlockSpec((tm, d), lambda i: (i, 0))
    x_out = jax.ShapeDtypeStruct((m, d), F32)
    return pl.pallas_call(
        functools.partial(_ffn_down_kernel, last=last),
        out_shape=x_out if last else (x_out, jax.ShapeDtypeStruct((m, d), BF16)),
        grid=(m // tm,),
        in_specs=[
            pl.BlockSpec((tm, d_ff), lambda i: (i, 0)),
            pl.BlockSpec((None, d_ff, d), lambda i: (layer, 0, 0), pipeline_mode=pl.Buffered(1)),
            rows,
            pl.BlockSpec((None, 1, d), lambda i: (g_layer, 0, 0)),
        ],
        out_specs=rows if last else (rows, rows),
        compiler_params=_params("parallel"),
        name="ffn_down",
    )(h, wo, res, g_next)


def _in_proj_kernel(xn_ref, wgm_ref, wr_ref, wg_ref, sc_ref, of_ref, ob_ref, og_ref, *, n_gm_tiles, n_f32_tiles):
    j = pl.program_id(1)

    @pl.when(j == 0)
    def _():
        og_ref[...] = _dot(xn_ref[...], wg_ref[...])

    @pl.when(j < n_gm_tiles)
    def _():
        of_ref[...] = _dot(xn_ref[...], wgm_ref[...])

    @pl.when((j >= n_gm_tiles) & (j < n_f32_tiles))
    def _():
        of_ref[...] = _dot(xn_ref[...], wr_ref[...])

    @pl.when(j >= n_f32_tiles)
    def _():
        ob_ref[...] = (_dot(xn_ref[...], wr_ref[...]) * sc_ref[...]).astype(BF16)


def _in_proj(xn, w_gm, w_rest, w_gate, col_scale, layer, *, tm=ROW_TILE, tn=2 * COL_TILE):
    m, d = xn.shape
    ng = w_gate.shape[-1]
    ngm = w_gm.shape[-1] // tn
    nf = N_F32_COLS // tn
    nb = N_BF16_COLS // tn
    assert w_gm.shape[-1] + w_rest.shape[-1] == N_F32_COLS + N_BF16_COLS and ngm < nf
    return pl.pallas_call(
        functools.partial(_in_proj_kernel, n_gm_tiles=ngm, n_f32_tiles=nf),
        out_shape=(jax.ShapeDtypeStruct((m, N_F32_COLS), F32), jax.ShapeDtypeStruct((m, N_BF16_COLS), BF16),
                   jax.ShapeDtypeStruct((m, ng), F32)),
        grid=(m // tm, nf + nb),
        in_specs=[
            pl.BlockSpec((tm, d), lambda i, j: (i, 0)),
            pl.BlockSpec((None, d, tn), lambda i, j: (layer, 0, jnp.minimum(j, ngm - 1))),
            pl.BlockSpec((None, d, tn), lambda i, j: (layer, 0, jnp.maximum(j - ngm, 0))),
            pl.BlockSpec((None, d, ng), lambda i, j: (layer, 0, 0)),
            pl.BlockSpec((1, tn), lambda i, j: (0, j)),
        ],
        out_specs=(
            pl.BlockSpec((tm, tn), lambda i, j: (i, jnp.minimum(j, nf - 1))),
            pl.BlockSpec((tm, tn), lambda i, j: (i, jnp.maximum(j - nf, 0))),
            pl.BlockSpec((tm, ng), lambda i, j: (i, 0)),
        ),
        compiler_params=_params("parallel", "arbitrary"),
        name="in_proj",
    )(xn, w_gm, w_rest, w_gate, col_scale)


def _dil_heads(q, kc, vc, kp, vp, first):
    r = lax.broadcasted_iota(jnp.int32, (Q_BLOCK, Q_BLOCK), 0)
    c = lax.broadcasted_iota(jnp.int32, (Q_BLOCK, Q_BLOCK), 1)
    bias_cur = jnp.where(c <= r, 0.0, NEG)
    if kp is not None:
        bias_prev = jnp.where(c >= r + jnp.where(first, Q_BLOCK, 0), 0.0, NEG)
    head = lax.shift_right_logical(lax.broadcasted_iota(jnp.int32, (Q_BLOCK, A_GROUP_WIDTH), 1),
                                   int(np.log2(A_HEAD_DIM)))
    o_acc = jnp.zeros((Q_BLOCK, A_GROUP_WIDTH), F32)
    l_acc = jnp.zeros((Q_BLOCK, A_GROUP_WIDTH), F32)
    for h in range(A_HEADS_PER_GROUP):
        mine = head == h
        qh = q * jnp.where(mine, 1.0, 0.0).astype(BF16)
        s_c = _dot_nt(qh, kc) + bias_cur
        m = jnp.max(s_c, axis=-1, keepdims=True)
        if kp is not None:
            s_p = _dot_nt(qh, kp) + bias_prev
            m = jnp.maximum(m, jnp.max(s_p, axis=-1, keepdims=True))
        p_c = jnp.exp2(s_c - m)
        l = jnp.sum(p_c, axis=-1, keepdims=True)
        pv = _dot(p_c.astype(BF16), vc)
        if kp is not None:
            p_p = jnp.exp2(s_p - m)
            l = l + jnp.sum(p_p, axis=-1, keepdims=True)
            pv = pv + _dot(p_p.astype(BF16), vp)
        o_acc = jnp.where(mine, pv / l, o_acc)
        l_acc = jnp.where(mine, m + jnp.log2(l), l_acc)
    return o_acc, l_acc


def _dil_attn_dense_kernel(q_ref, kc_ref, kp_ref, vc_ref, vp_ref, o_ref, l_ref):
    for u in range(DIL_TILES):
        rs = slice(u * Q_BLOCK, (u + 1) * Q_BLOCK)
        ps = slice((u - 1) * Q_BLOCK, u * Q_BLOCK)
        kp, vp = (kp_ref[...], vp_ref[...]) if u == 0 else (kc_ref[ps, :], vc_ref[ps, :])
        first = (pl.program_id(1) == 0) if u == 0 else False
        o, lse = _dil_heads(q_ref[rs, :], kc_ref[rs, :], vc_ref[rs, :], kp, vp, first)
        for half in range(A_GROUP_WIDTH // LANES):
            o_ref[half, rs, :] = o[:, half * LANES:(half + 1) * LANES]
            l_ref[half, rs, :] = lse[:, half * LANES:(half + 1) * LANES]


def _dil_attn_strided_kernel(q_ref, k_ref, v_ref, o_ref, l_ref, qf_ref, kf_ref, vf_ref, *, dil, has_prev):
    i = pl.program_id(1)
    step = pl.program_id(2)
    halves = A_GROUP_WIDTH // LANES
    slot = lax.rem(i, 2) if has_prev else 0

    @pl.when(step == 0)
    def _():
        for half in range(halves):
            cs = slice(half * LANES, (half + 1) * LANES)
            qf_ref[half] = q_ref[:, cs].astype(F32)
            kf_ref[slot, half] = k_ref[:, cs].astype(F32)
            vf_ref[slot, half] = v_ref[:, cs].astype(F32)

    if has_prev:
        @pl.when((step == 0) & (i == 0))
        def _():
            kf_ref[1] = jnp.zeros(kf_ref.shape[1:], F32)
            vf_ref[1] = jnp.zeros(vf_ref.shape[1:], F32)

    for u in range(DIL_TILES):
        r = step * DIL_TILES + u

        def rows_of_class(ref, *lead, r=r):
            return jnp.concatenate([ref[(*lead, half, pl.ds(r, Q_BLOCK, stride=dil), slice(None))]
                                    for half in range(halves)], axis=1).astype(BF16)

        kp = rows_of_class(kf_ref, 1 - slot) if has_prev else None
        vp = rows_of_class(vf_ref, 1 - slot) if has_prev else None
        o, lse = _dil_heads(rows_of_class(qf_ref), rows_of_class(kf_ref, slot), rows_of_class(vf_ref, slot),
                            kp, vp, i == 0)
        for half in range(halves):
            o_ref[half, pl.ds(r, Q_BLOCK, stride=dil), :] = o[:, half * LANES:(half + 1) * LANES]
            l_ref[half, pl.ds(r, Q_BLOCK, stride=dil), :] = lse[:, half * LANES:(half + 1) * LANES]


def _dil_attn(cb3, gi):
    bsz, s, _ = cb3.shape
    dil = A_GROUPS[gi][1]
    w = A_GROUP_WIDTH
    halves = w // LANES
    out = jax.ShapeDtypeStruct((bsz, halves, s, LANES), F32)
    name = f"dilated_attention_g{gi}"
    if dil == 1:
        step_rows = Q_BLOCK * DIL_TILES

        def col(off, prev):
            if prev:
                return pl.BlockSpec((None, Q_BLOCK, w),
                                    lambda b, i: (b, jnp.maximum(i * DIL_TILES - 1, 0), (off + gi * w) // w))
            return pl.BlockSpec((None, step_rows, w), lambda b, i: (b, i, (off + gi * w) // w))

        ospec = pl.BlockSpec((None, halves, step_rows, LANES), lambda b, i: (b, 0, i, 0))
        return pl.pallas_call(
            _dil_attn_dense_kernel,
            out_shape=(out, out),
            grid=(bsz, s // step_rows),
            in_specs=[col(BOFF_QA, False), col(BOFF_KA, False), col(BOFF_KA, True), col(BOFF_VA, False), col(BOFF_VA, True)],
            out_specs=(ospec, ospec),
            compiler_params=_params("parallel", "arbitrary"),
            name=name,
        )(cb3, cb3, cb3, cb3, cb3)

    rows = Q_BLOCK * dil
    tiles = s // rows
    has_prev = tiles > 1

    def col(off):
        return pl.BlockSpec((None, rows, w), lambda b, i, r: (b, i, (off + gi * w) // w))

    ospec = pl.BlockSpec((None, halves, rows, LANES), lambda b, i, r: (b, 0, i, 0))
    return pl.pallas_call(
        functools.partial(_dil_attn_strided_kernel, dil=dil, has_prev=has_prev),
        out_shape=(out, out),
        grid=(bsz, tiles, dil // DIL_TILES),
        in_specs=[col(BOFF_QA), col(BOFF_KA), col(BOFF_VA)],
        out_specs=(ospec, ospec),
        scratch_shapes=[pltpu.VMEM((halves, rows, LANES), F32),
                        pltpu.VMEM((2 if has_prev else 1, halves, rows, LANES), F32),
                        pltpu.VMEM((2 if has_prev else 1, halves, rows, LANES), F32)],
        compiler_params=_params("parallel", "arbitrary", "arbitrary"),
        name=name,
    )(cb3, cb3, cb3)


def _pool_kernel(x_ref, w_ref, sc_ref, o_ref):
    s = x_ref.shape[0]
    t = lax.broadcasted_iota(jnp.int32, (s, B_GROUP_DIM), 0)
    for gi, win in enumerate(B_WINDOWS):
        cs = slice(gi * B_GROUP_DIM, (gi + 1) * B_GROUP_DIM)
        x = x_ref[:, cs]
        acc = x
        k = 1
        while k < win:
            acc = acc + jnp.where(t >= k, pltpu.roll(acc, k, axis=0), 0.0)
            k *= 2
        cnt = jnp.minimum(t + 1, win).astype(F32)
        z = acc / cnt - x
        o_ref[:, cs] = _dot(z.astype(BF16), w_ref[gi]) * sc_ref[:, cs]


def _pool_mixer(cf3, pool_w, pool_scale, layer):
    bsz, s, _ = cf3.shape
    ng = len(B_WINDOWS)
    return pl.pallas_call(
        _pool_kernel,
        out_shape=jax.ShapeDtypeStruct((bsz, s, B_WIDTH), F32),
        grid=(bsz,),
        in_specs=[
            pl.BlockSpec((None, s, B_WIDTH), lambda b: (b, 0, FOFF_XB // B_WIDTH)),
            pl.BlockSpec((None, ng, B_GROUP_DIM, B_GROUP_DIM), lambda b: (layer, 0, 0, 0)),
            pl.BlockSpec((None, 1, B_WIDTH), lambda b: (layer, 0, 0)),
        ],
        out_specs=pl.BlockSpec((None, s, B_WIDTH), lambda b: (b, 0, 0)),
        compiler_params=_params("parallel"),
        name="pool_mixer",
    )(cf3, pool_w, pool_scale)


def _gelu_tanh(x):
    return 0.5 * x * (1.0 + jnp.tanh(np.float32(np.sqrt(2.0 / np.pi)) * (x + 0.044715 * (x * x * x))))


def _compress_one(z_ref, pe_ref, w1_ref, w2_ref, o_ref):
    nch = z_ref.shape[0] // CMP_STRIDE
    dh = C_HEAD_DIM
    first = jnp.zeros((nch, dh), F32)
    second = jnp.zeros((nch, dh), F32)
    for p in range(CMP_STRIDE):
        zp = z_ref[pl.ds(p, nch, stride=CMP_STRIDE), :]
        a = (zp + pe_ref[p:p + 1, :]).astype(BF16)
        b = (zp + pe_ref[CMP_STRIDE + p:CMP_STRIDE + p + 1, :]).astype(BF16)
        first = first + _dot(a, w1_ref[p * dh:(p + 1) * dh, :])
        second = second + _dot(b, w1_ref[(CMP_STRIDE + p) * dh:(CMP_STRIDE + p + 1) * dh, :])
    pre = first + pltpu.roll(second, nch - 1, axis=0)
    o_ref[...] = _dot(_gelu_tanh(pre).astype(BF16), w2_ref[...]).astype(BF16)


def _compress_kernel(zk_ref, zv_ref, pek_ref, w1k_ref, w2k_ref, pev_ref, w1v_ref, w2v_ref, kc_ref, vc_ref):
    _compress_one(zk_ref, pek_ref, w1k_ref, w2k_ref, kc_ref)
    _compress_one(zv_ref, pev_ref, w1v_ref, w2v_ref, vc_ref)


def _compress(cf3, pe_k, w1_k, w2_k, pe_v, w1_v, w2_v, layer):
    bsz, s, _ = cf3.shape
    nch = s // CMP_STRIDE
    dh = C_HEAD_DIM
    cdim = CMP_LEN * dh

    def col(off):
        return pl.BlockSpec((None, s, dh), lambda b, g: (b, 0, off // dh + g))

    def wspec(shape):
        return pl.BlockSpec((None,) + shape, lambda b, g: (layer,) + (0,) * len(shape))

    out = jax.ShapeDtypeStruct((bsz, C_KV_GROUPS, nch, dh), BF16)
    ospec = pl.BlockSpec((None, None, nch, dh), lambda b, g: (b, g, 0, 0))
    return pl.pallas_call(
        _compress_kernel,
        out_shape=(out, out),
        grid=(bsz, C_KV_GROUPS),
        in_specs=[col(FOFF_KCMP), col(FOFF_VCMP),
                  wspec((CMP_LEN, dh)), wspec((cdim, dh)), wspec((dh, dh)),
                  wspec((CMP_LEN, dh)), wspec((cdim, dh)), wspec((dh, dh))],
        out_specs=(ospec, ospec),
        compiler_params=_params("parallel", "parallel"),
        name="nsa_compress",
    )(cf3, cf3, pe_k, w1_k, w2_k, pe_v, w1_v, w2_v)


def _split3(x):
    hi = x.astype(BF16)
    r1 = x - hi.astype(F32)
    mid = r1.astype(BF16)
    lo = (r1 - mid.astype(F32)).astype(BF16)
    return hi, mid, lo


def _softmax_pv(s, v):
    m = jnp.max(s, axis=-1, keepdims=True)
    p = jnp.exp2(s - m)
    return _dot(p.astype(BF16), v) / jnp.sum(p, axis=-1, keepdims=True)


def _nsa_tile(t0, q, gates, kc_ref, vc_ref, kw_ref, vw_ref, ovt_ref, nb):
    hg = C_HEADS_PER_GROUP
    dh = C_HEAD_DIM
    row = t0 + lax.broadcasted_iota(jnp.int32, (Q_BLOCK, LANES), 0)
    lane = lax.broadcasted_iota(jnp.int32, (Q_BLOCK, LANES), 1)

    q3 = jnp.concatenate([q[:, h * dh:(h + 1) * dh] for h in range(hg)], axis=0)
    rows = [slice(h * Q_BLOCK, (h + 1) * Q_BLOCK) for h in range(hg)]
    gt = _sigmoid(gates)

    cvalid = lane * CMP_STRIDE + (CMP_LEN - 1) <= row
    s3 = _dot_nt(q3, kc_ref[...])
    vc = vc_ref[...]
    out = []
    psum = jnp.zeros((Q_BLOCK, LANES), F32)
    for h in range(hg):
        s = jnp.where(cvalid, s3[rows[h]], NEG)
        p = jnp.exp2(s - jnp.max(s, axis=-1, keepdims=True))
        p = jnp.where(cvalid, p / jnp.sum(p, axis=-1, keepdims=True), 0.0)
        out.append(gt[:, 3 * h:3 * h + 1] * _dot(p.astype(BF16), vc))
        psum = psum + p

    ovt = ovt_ref[...]
    imp_t = sum(_dot_nt(ovt, part) for part in _split3(psum))
    blk = lax.broadcasted_iota(jnp.int32, (nb, Q_BLOCK), 0)
    tb = lax.shift_right_logical(t0 + lax.broadcasted_iota(jnp.int32, (nb, Q_BLOCK), 1), int(np.log2(SEL_BLOCK)))
    forced = (blk == 0) | (blk == tb) | (blk == tb - 1)
    score = jnp.where(blk > tb, -1.0, imp_t + jnp.where(forced, FORCE_BONUS, 0.0))
    blk_f = blk.astype(F32)
    sel_t = jnp.zeros((nb, Q_BLOCK), F32)
    for _ in range(min(SEL_TOPN, nb)):
        mx = jnp.max(score, axis=0, keepdims=True)
        idx = jnp.min(jnp.where(score == mx, blk_f, float(nb)), axis=0, keepdims=True)
        hit = blk_f == idx
        sel_t = jnp.where(hit, 1.0, sel_t)
        score = jnp.where(hit, -3.0, score)
    sel = jnp.concatenate([sel_t, jnp.zeros((LANES - nb, Q_BLOCK), F32)], axis=0).T
    sel = jnp.where(sel > 0.5, 0.0, NEG).astype(BF16)

    wlen = WIN + Q_BLOCK
    w0 = pl.multiple_of(jnp.maximum(t0 - WIN, 0), Q_BLOCK)
    kpos = w0 + lax.broadcasted_iota(jnp.int32, (Q_BLOCK, wlen), 1)
    qpos = t0 + lax.broadcasted_iota(jnp.int32, (Q_BLOCK, wlen), 0)
    bias = jnp.where((kpos <= qpos) & (kpos > qpos - WIN), 0.0, NEG)
    s3 = _dot_nt(q3, kw_ref[pl.ds(w0, wlen), :])
    vw = vw_ref[pl.ds(w0, wlen), :]
    for h in range(hg):
        out[h] = out[h] + gt[:, 3 * h + 2:3 * h + 3] * _softmax_pv(s3[rows[h]] + bias, vw)
    return q3, gt, out, sel


def _nsa_kernel(q_ref, kc_ref, vc_ref, ks_ref, vs_ref, kw_ref, vw_ref, gt_ref, ovt_ref, ex_ref, o_ref, *, s_len):
    n = pl.program_id(2)
    hg = C_HEADS_PER_GROUP
    dh = C_HEAD_DIM
    tiles = []
    for u in range(NSA_TILES):
        rs = slice(u * Q_BLOCK, (u + 1) * Q_BLOCK)
        t0 = (n * NSA_TILES + u) * Q_BLOCK
        tiles.append((rs, t0) + _nsa_tile(t0, q_ref[rs, :], gt_ref[rs, :], kc_ref, vc_ref, kw_ref, vw_ref, ovt_ref,
                                          s_len // SEL_BLOCK))

    r = lax.broadcasted_iota(jnp.int32, (Q_BLOCK, Q_BLOCK), 0)
    c = lax.broadcasted_iota(jnp.int32, (Q_BLOCK, Q_BLOCK), 1)
    causal = jnp.where(c <= r, 0.0, NEG)
    for step in range(s_len // (Q_BLOCK * NSA_TILES)):

        @pl.when(n == step)
        def _(step=step):
            for u, (rs, _, q3, gt, out, sel) in enumerate(tiles):
                klen = (step * NSA_TILES + u + 1) * Q_BLOCK
                blocked = _dot(sel, ex_ref[:, :klen])
                bias = blocked[:, klen - Q_BLOCK:] + causal
                if klen > Q_BLOCK:
                    bias = jnp.concatenate([blocked[:, :klen - Q_BLOCK], bias], axis=1)
                s3 = _dot_nt(q3, ks_ref[:klen, :])
                vs = vs_ref[:klen, :]
                for h in range(hg):
                    hs = slice(h * Q_BLOCK, (h + 1) * Q_BLOCK)
                    o_ref[rs, h * dh:(h + 1) * dh] = out[h] + gt[:, 3 * h + 1:3 * h + 2] * _softmax_pv(s3[hs] + bias, vs)


def _nsa_constants(s):
    nch = s // CMP_STRIDE
    nb = s // SEL_BLOCK
    n_cmp = (s - CMP_LEN) // CMP_STRIDE + 1
    ci = np.arange(nch)[None, :] * CMP_STRIDE
    bj = np.arange(nb)[:, None] * SEL_BLOCK
    overlap_t = (ci < bj + SEL_BLOCK) & (ci + CMP_LEN > bj) & (np.arange(nch)[None, :] < n_cmp)
    expand = np.arange(LANES)[:, None] == (np.arange(s)[None, :] // SEL_BLOCK)
    return jnp.asarray(overlap_t, BF16), jnp.asarray(expand, BF16)


def _nsa(cb3, kc, vc, gates3):
    bsz, s, _ = cb3.shape
    nq = s // Q_BLOCK
    nch = s // CMP_STRIDE
    nb = s // SEL_BLOCK
    dh = C_HEAD_DIM
    step_rows = Q_BLOCK * NSA_TILES
    assert nch == LANES and nb <= LANES and nb % 8 == 0 and s % step_rows == 0 and s >= WIN + Q_BLOCK
    overlap_t, expand = _nsa_constants(s)

    def kv(off):
        return pl.BlockSpec((None, s, dh), lambda b, g, n: (b, 0, off // dh + g))

    cmp_spec = pl.BlockSpec((None, None, nch, dh), lambda b, g, n: (b, g, 0, 0))
    return pl.pallas_call(
        functools.partial(_nsa_kernel, s_len=s),
        out_shape=jax.ShapeDtypeStruct((bsz, s, C_WIDTH), F32),
        grid=(bsz, C_KV_GROUPS, s // step_rows),
        in_specs=[
            pl.BlockSpec((None, step_rows, C_GROUP_WIDTH), lambda b, g, n: (b, n, BOFF_QC // C_GROUP_WIDTH + g)),
            cmp_spec, cmp_spec,
            kv(BOFF_KSLC), kv(BOFF_VSLC), kv(BOFF_KWIN), kv(BOFF_VWIN),
            pl.BlockSpec((None, step_rows, LANES), lambda b, g, n: (b, n, g)),
            pl.BlockSpec((nb, nch), lambda b, g, n: (0, 0)),
            pl.BlockSpec((LANES, s), lambda b, g, n: (0, 0)),
        ],
        out_specs=pl.BlockSpec((None, step_rows, C_GROUP_WIDTH), lambda b, g, n: (b, n, g)),
        compiler_params=_params("parallel", "parallel", "arbitrary"),
        name="nsa_attention",
    )(cb3, kc, vc, cb3, cb3, cb3, cb3, gates3, overlap_t, expand)


def _merge_kernel(*refs):
    ng = len(A_GROUPS)
    oa_refs, la_refs = refs[:ng], refs[ng:2 * ng]
    zb_ref, oc_ref, g0_ref, g1_ref, g2_ref, x_ref, pa_ref, pb_ref, pc_ref, wo_ref, gn_ref, o_ref, xn_ref = refs[2 * ng:]

    def lanes(ref):
        return jnp.concatenate([ref[half] for half in range(ref.shape[0])], axis=1)

    lse = [lanes(r) for r in la_refs]
    m = functools.reduce(jnp.maximum, lse)
    e = [jnp.exp2(l - m) for l in lse]
    oa = sum(w * lanes(r) for w, r in zip(e, oa_refs)) / sum(e)
    ya = _dot(oa.astype(BF16), pa_ref[...])
    yb = _dot(zb_ref[...].astype(BF16), pb_ref[...])
    yc = _dot(oc_ref[...].astype(BF16), pc_ref[...])
    mix = _sigmoid(g0_ref[...]) * ya + _sigmoid(g1_ref[...]) * yb + _sigmoid(g2_ref[...]) * yc
    y = x_ref[...] + _dot(mix.astype(BF16), wo_ref[...])
    o_ref[...] = y
    xn_ref[...] = _rms_scaled(y, gn_ref[...]).astype(BF16)


def _merge(oa, la, zb, oc, cf, x, proj_a, proj_b, proj_c, w_out, g_next, layer, *, tm=256):
    m, d = x.shape
    assert d == D_GATE

    def rows(w):
        return pl.BlockSpec((tm, w), lambda i: (i, 0))

    def gate(k):
        return pl.BlockSpec((tm, d), lambda i: (i, FOFF_GM // d + k))

    def weight(k):
        return pl.BlockSpec((None, k, d), lambda i: (layer, 0, 0), pipeline_mode=pl.Buffered(1))

    _, halves, s, _ = oa[0].shape
    per_seq = s // tm
    group = pl.BlockSpec((None, halves, tm, LANES), lambda i: (i // per_seq, 0, i % per_seq, 0))

    return pl.pallas_call(
        _merge_kernel,
        out_shape=(jax.ShapeDtypeStruct((m, d), F32), jax.ShapeDtypeStruct((m, d), BF16)),
        grid=(m // tm,),
        in_specs=[group] * (2 * len(A_GROUPS))
        + [rows(B_WIDTH), rows(C_WIDTH), gate(0), gate(1), gate(2), rows(d),
           weight(A_GROUP_WIDTH), weight(B_WIDTH), weight(C_WIDTH), weight(d),
           pl.BlockSpec((None, 1, d), lambda i: (layer, 0, 0))],
        out_specs=(rows(d), rows(d)),
        compiler_params=_params("parallel"),
        name="merge",
    )(*oa, *la, zb, oc, cf, cf, cf, x, proj_a, proj_b, proj_c, w_out, g_next)


_A3 = 3 * A_WIDTH
_XB_END = _A3 + B_WIDTH
_QC_END = _XB_END + C_WIDTH
_CMP_END = _QC_END + 2 * C_KV_WIDTH
_KV_END = _QC_END + 6 * C_KV_WIDTH
_GM_START = _KV_END + N_GATES
_REST_RANGES = ((_A3, _XB_END), (_QC_END, _CMP_END), (0, _A3), (_XB_END, _QC_END), (_CMP_END, _KV_END))


def _reorder_w_in(w_in):
    w_gm = w_in[..., _GM_START:].astype(BF16)
    w_rest = jnp.concatenate([w_in[..., lo:hi] for lo, hi in _REST_RANGES], axis=-1).astype(BF16)
    per_group = 3 * C_HEADS_PER_GROUP
    pad = jnp.zeros(w_in.shape[:-1] + (LANES - per_group,), w_in.dtype)
    w_gate = jnp.concatenate(
        [piece for g in range(C_KV_GROUPS)
         for piece in (w_in[..., _KV_END + g * per_group:_KV_END + (g + 1) * per_group], pad)], axis=-1).astype(BF16)
    col_scale = np.ones((1, N_F32_COLS + N_BF16_COLS), np.float32)
    col_scale[:, N_F32_COLS + BOFF_QA:N_F32_COLS + BOFF_QA + A_WIDTH] = A_HEAD_DIM ** -0.5 * LOG2E
    col_scale[:, N_F32_COLS + BOFF_QC:N_F32_COLS + BOFF_QC + C_WIDTH] = C_HEAD_DIM ** -0.5 * LOG2E
    return w_gm, w_rest, w_gate, jnp.asarray(col_scale)


def _mixing(x, xn, layer, w_gm, w_rest, w_gate, col_scale, pool_w, pool_scale, pe_k, w1_k, w2_k, pe_v, w1_v, w2_v,
            proj_a, proj_b, proj_c, w_out, g_next, bsz, s):
    m, d = x.shape
    cf, cb, gates = _in_proj(xn, w_gm, w_rest, w_gate, col_scale, layer)
    cf3 = cf.reshape(bsz, s, N_F32_COLS)
    cb3 = cb.reshape(bsz, s, N_BF16_COLS)
    gates3 = gates.reshape(bsz, s, C_KV_GROUPS * LANES)

    oa, la = zip(*[_dil_attn(cb3, gi) for gi in range(len(A_GROUPS))])
    zb = _pool_mixer(cf3, pool_w, pool_scale, layer).reshape(m, B_WIDTH)
    kc, vc = _compress(cf3, pe_k, w1_k, w2_k, pe_v, w1_v, w2_v, layer)
    oc = _nsa(cb3, kc, vc, gates3).reshape(m, C_WIDTH)
    return _merge(oa, la, zb, oc, cf, x, proj_a, proj_b, proj_c, w_out, g_next, layer)


def kernel(x, ffn1_norm, ffn1_wi, ffn1_wo, mix_norm, w_in, pool_w, pool_scale, cmp_pe_k, cmp_w1_k, cmp_w2_k,
           cmp_pe_v, cmp_w1_v, cmp_w2_v, proj_a, proj_b, proj_c, w_out, ffn2_norm, ffn2_wi, ffn2_wo, final_norm):
    bsz, s, d = x.shape
    depth = ffn1_wi.shape[0]
    for win, dil in A_GROUPS:
        assert win // dil == Q_BLOCK and s % (dil * Q_BLOCK) == 0
    assert CMP_LEN == 2 * CMP_STRIDE and all(w & (w - 1) == 0 for w in B_WINDOWS)

    bf = lambda w: w.astype(BF16)
    row3 = lambda g: g.reshape(g.shape[0], 1, g.shape[-1])
    w_gm, w_rest, w_gate, col_scale = _reorder_w_in(w_in)
    ffn1_wo, ffn2_wo = bf(ffn1_wo), bf(ffn2_wo)
    pool_w, cmp_w1_k, cmp_w2_k, cmp_w1_v, cmp_w2_v = bf(pool_w), bf(cmp_w1_k), bf(cmp_w2_k), bf(cmp_w1_v), bf(cmp_w2_v)
    proj_a, proj_b, proj_c, w_out = bf(proj_a), bf(proj_b), bf(proj_c), bf(w_out)
    ffn1_norm, mix_norm, ffn2_norm, pool_scale = row3(ffn1_norm), row3(mix_norm), row3(ffn2_norm), row3(pool_scale)

    final_norm = final_norm.reshape(1, 1, d)
    x = x.reshape(bsz * s, d)
    xn = _row_norm(x, ffn1_norm, 0)
    for layer in range(depth):
        x, xn = _ffn_down(_ffn_up(xn, ffn1_wi, layer), ffn1_wo, x, mix_norm, layer, layer)
        x, xn = _mixing(x, xn, layer, w_gm, w_rest, w_gate, col_scale, pool_w, pool_scale, cmp_pe_k, cmp_w1_k, cmp_w2_k,
                        cmp_pe_v, cmp_w1_v, cmp_w2_v, proj_a, proj_b, proj_c, w_out, ffn2_norm, bsz, s)
        h = _ffn_up(xn, ffn2_wi, layer)
        if layer + 1 < depth:
            x, xn = _ffn_down(h, ffn2_wo, x, ffn1_norm, layer, layer + 1)
        else:
            x = _ffn_down(h, ffn2_wo, x, final_norm, layer, 0, last=True)
    return x.reshape(bsz, s, d)
```

```python
import functools

import numpy as np
import jax
import jax.numpy as jnp
from jax import lax
from jax.experimental import pallas as pl
from jax.experimental.pallas import tpu as pltpu

F32 = jnp.float32
BF16 = jnp.bfloat16

EPS = 1e-6
NEG = -1e30
LOG2E = float(np.log2(np.e))
Q_BLOCK = 128
LANES = 128

A_GROUPS = ((128, 1), (512, 4), (2048, 16))
A_HEADS_PER_GROUP = 4
A_HEAD_DIM = 64
A_GROUP_WIDTH = A_HEADS_PER_GROUP * A_HEAD_DIM
A_WIDTH = A_GROUP_WIDTH * len(A_GROUPS)

B_WINDOWS = (2, 4, 8, 16)
B_GROUP_DIM = 128
B_WIDTH = B_GROUP_DIM * len(B_WINDOWS)

C_KV_GROUPS = 2
C_HEADS_PER_GROUP = 3
C_HEADS = C_KV_GROUPS * C_HEADS_PER_GROUP
C_HEAD_DIM = 128
C_GROUP_WIDTH = C_HEADS_PER_GROUP * C_HEAD_DIM
C_WIDTH = C_HEADS * C_HEAD_DIM
C_KV_WIDTH = C_KV_GROUPS * C_HEAD_DIM
CMP_LEN = 32
CMP_STRIDE = 16
SEL_BLOCK = 64
SEL_TOPN = 8
FORCE_BONUS = 100.0
WIN = 512
N_GATES = 3 * C_HEADS
NSA_TILES = 4
DIL_TILES = 4

N_BRANCH = 3
D_GATE = 2048

FOFF_GM = 0
FOFF_XB = N_BRANCH * D_GATE
FOFF_KCMP = FOFF_XB + B_WIDTH
FOFF_VCMP = FOFF_KCMP + C_KV_WIDTH
N_F32_COLS = FOFF_VCMP + C_KV_WIDTH
BOFF_QA = 0
BOFF_KA = BOFF_QA + A_WIDTH
BOFF_VA = BOFF_KA + A_WIDTH
BOFF_QC = BOFF_VA + A_WIDTH
BOFF_KSLC = BOFF_QC + C_WIDTH
BOFF_VSLC = BOFF_KSLC + C_KV_WIDTH
BOFF_KWIN = BOFF_VSLC + C_KV_WIDTH
BOFF_VWIN = BOFF_KWIN + C_KV_WIDTH
N_BF16_COLS = BOFF_VWIN + C_KV_WIDTH

VMEM_LIMIT = 56 * 1024 * 1024
ROW_TILE = 1024
COL_TILE = 512
FFN_ROW_TILE = 1024

def _params(*sem):
    return pltpu.CompilerParams(dimension_semantics=sem, vmem_limit_bytes=VMEM_LIMIT)


def _dot(a, b):
    return jnp.dot(a, b, preferred_element_type=F32)


def _dot_nt(a, b):
    return lax.dot_general(a, b, (((1,), (1,)), ((), ())), preferred_element_type=F32)


def _sigmoid(x):
    return 1.0 / (1.0 + jnp.exp(-x))


def _rms_scaled(x, g):
    return x * lax.rsqrt(jnp.mean(x * x, axis=-1, keepdims=True) + EPS) * g


def _row_norm_kernel(x_ref, g_ref, o_ref):
    o_ref[...] = _rms_scaled(x_ref[...], g_ref[...]).astype(o_ref.dtype)


def _row_norm(x, g, layer, *, tm=512):
    m, d = x.shape
    return pl.pallas_call(
        _row_norm_kernel,
        out_shape=jax.ShapeDtypeStruct((m, d), BF16),
        grid=(m // tm,),
        in_specs=[pl.BlockSpec((tm, d), lambda i: (i, 0)), pl.BlockSpec((None, 1, d), lambda i: (layer, 0, 0))],
        out_specs=pl.BlockSpec((tm, d), lambda i: (i, 0)),
        compiler_params=_params("parallel"),
        name="row_norm",
    )(x, g)


def _ffn_up_kernel(xn_ref, wa_ref, wb_ref, h_ref, w_ref):
    tn = h_ref.shape[-1]

    @pl.when(pl.program_id(1) == 0)
    def _():
        w_ref[:, :tn] = wa_ref[...].astype(BF16)
        w_ref[:, tn:] = wb_ref[...].astype(BF16)

    xn = xn_ref[...]
    a = _dot(xn, w_ref[:, :tn])
    b = _dot(xn, w_ref[:, tn:])
    h_ref[...] = (a * _sigmoid(a) * b).astype(BF16)


def _ffn_up(xn, wi, layer, *, tm=FFN_ROW_TILE, tn=COL_TILE):
    m, d = xn.shape
    d_ff = wi.shape[-1] // 2
    nj = d_ff // tn
    return pl.pallas_call(
        _ffn_up_kernel,
        out_shape=jax.ShapeDtypeStruct((m, d_ff), BF16),
        grid=(nj, m // tm),
        in_specs=[
            pl.BlockSpec((tm, d), lambda j, i: (i, 0)),
            pl.BlockSpec((None, d, tn), lambda j, i: (layer, 0, j)),
            pl.BlockSpec((None, d, tn), lambda j, i: (layer, 0, j + nj)),
        ],
        out_specs=pl.BlockSpec((tm, tn), lambda j, i: (i, j)),
        scratch_shapes=[pltpu.VMEM((d, 2 * tn), BF16)],
        compiler_params=_params("parallel", "arbitrary"),
        name="ffn_up",
    )(xn, wi, wi)


def _ffn_down_kernel(h_ref, w_ref, r_ref, g_ref, *o_refs, last):
    y = r_ref[...] + 0.5 * _dot(h_ref[...], w_ref[...])
    if last:
        o_refs[0][...] = _rms_scaled(y, g_ref[...])
    else:
        o_refs[0][...] = y
        o_refs[1][...] = _rms_scaled(y, g_ref[...]).astype(BF16)


def _ffn_down(h, wo, res, g_next, layer, g_layer, *, last=False, tm=256):
    m, d_ff = h.shape
    d = res.shape[-1]
    rows = pl.BlockSpec((tm, d), lambda i: (i, 0))
    x_out = jax.ShapeDtypeStruct((m, d), F32)
    return pl.pallas_call(
        functools.partial(_ffn_down_kernel, last=last),
        out_shape=x_out if last else (x_out, jax.ShapeDtypeStruct((m, d), BF16)),
        grid=(m // tm,),
        in_specs=[
            pl.BlockSpec((tm, d_ff), lambda i: (i, 0)),
            pl.BlockSpec((None, d_ff, d), lambda i: (layer, 0, 0), pipeline_mode=pl.Buffered(1)),
            rows,
            pl.BlockSpec((None, 1, d), lambda i: (g_layer, 0, 0)),
        ],
        out_specs=rows if last else (rows, rows),
        compiler_params=_params("parallel"),
        name="ffn_down",
    )(h, wo, res, g_next)


def _in_proj_f32_kernel(xn_ref, w_ref, wg_ref, of_ref, og_ref):
    @pl.when(pl.program_id(1) == 0)
    def _():
        og_ref[...] = _dot(xn_ref[...], wg_ref[...])

    of_ref[...] = _dot(xn_ref[...], w_ref[...])


def _in_proj_f32(xn, w_f32cols, w_gate, layer, *, tm=ROW_TILE, tn=2 * COL_TILE):
    m, d = xn.shape
    ng = w_gate.shape[-1]
    n = w_f32cols.shape[-1]
    return pl.pallas_call(
        _in_proj_f32_kernel,
        out_shape=(jax.ShapeDtypeStruct((m, n), F32), jax.ShapeDtypeStruct((m, ng), F32)),
        grid=(m // tm, n // tn),
        in_specs=[
            pl.BlockSpec((tm, d), lambda i, j: (i, 0)),
            pl.BlockSpec((None, d, tn), lambda i, j: (layer, 0, j)),
            pl.BlockSpec((None, d, ng), lambda i, j: (layer, 0, 0)),
        ],
        out_specs=(pl.BlockSpec((tm, tn), lambda i, j: (i, j)), pl.BlockSpec((tm, ng), lambda i, j: (i, 0))),
        compiler_params=_params("parallel", "arbitrary"),
        name="in_proj_f32",
    )(xn, w_f32cols, w_gate)


def _in_proj_bf16_kernel(xn_ref, w_ref, sc_ref, o_ref, wb_ref):
    @pl.when(pl.program_id(1) == 0)
    def _():
        wb_ref[...] = w_ref[...].astype(BF16)

    o_ref[...] = (_dot(xn_ref[...], wb_ref[...]) * sc_ref[...]).astype(BF16)


def _in_proj_bf16(xn, w_bf16cols, col_scale, layer, *, tm=ROW_TILE, tn=2 * COL_TILE):
    m, d = xn.shape
    n = w_bf16cols.shape[-1]
    return pl.pallas_call(
        _in_proj_bf16_kernel,
        out_shape=jax.ShapeDtypeStruct((m, n), BF16),
        grid=(n // tn, m // tm),
        in_specs=[
            pl.BlockSpec((tm, d), lambda j, i: (i, 0)),
            pl.BlockSpec((None, d, tn), lambda j, i: (layer, 0, j)),
            pl.BlockSpec((1, tn), lambda j, i: (0, j)),
        ],
        out_specs=pl.BlockSpec((tm, tn), lambda j, i: (i, j)),
        scratch_shapes=[pltpu.VMEM((d, tn), BF16)],
        compiler_params=_params("parallel", "arbitrary"),
        name="in_proj_bf16",
    )(xn, w_bf16cols, col_scale)


def _dil_heads(q, kc, vc, kp, vp, first):
    r = lax.broadcasted_iota(jnp.int32, (Q_BLOCK, Q_BLOCK), 0)
    c = lax.broadcasted_iota(jnp.int32, (Q_BLOCK, Q_BLOCK), 1)
    bias_cur = jnp.where(c <= r, 0.0, NEG)
    if kp is not None:
        bias_prev = jnp.where(c >= r + jnp.where(first, Q_BLOCK, 0), 0.0, NEG)
    head = lax.shift_right_logical(lax.broadcasted_iota(jnp.int32, (Q_BLOCK, A_GROUP_WIDTH), 1),
                                   int(np.log2(A_HEAD_DIM)))
    o_acc = jnp.zeros((Q_BLOCK, A_GROUP_WIDTH), F32)
    l_acc = jnp.zeros((Q_BLOCK, A_GROUP_WIDTH), F32)
    for h in range(A_HEADS_PER_GROUP):
        mine = head == h
        qh = q * jnp.where(mine, 1.0, 0.0).astype(BF16)
        s_c = _dot_nt(qh, kc) + bias_cur
        m = jnp.max(s_c, axis=-1, keepdims=True)
        if kp is not None:
            s_p = _dot_nt(qh, kp) + bias_prev
            m = jnp.maximum(m, jnp.max(s_p, axis=-1, keepdims=True))
        p_c = jnp.exp2(s_c - m)
        l = jnp.sum(p_c, axis=-1, keepdims=True)
        pv = _dot(p_c.astype(BF16), vc)
        if kp is not None:
            p_p = jnp.exp2(s_p - m)
            l = l + jnp.sum(p_p, axis=-1, keepdims=True)
            pv = pv + _dot(p_p.astype(BF16), vp)
        o_acc = jnp.where(mine, pv / l, o_acc)
        l_acc = jnp.where(mine, m + jnp.log2(l), l_acc)
    return o_acc, l_acc


def _dil_attn_dense_kernel(q_ref, kc_ref, kp_ref, vc_ref, vp_ref, o_ref, l_ref):
    for u in range(DIL_TILES):
        rs = slice(u * Q_BLOCK, (u + 1) * Q_BLOCK)
        ps = slice((u - 1) * Q_BLOCK, u * Q_BLOCK)
        kp, vp = (kp_ref[...], vp_ref[...]) if u == 0 else (kc_ref[ps, :], vc_ref[ps, :])
        first = (pl.program_id(1) == 0) if u == 0 else False
        o, lse = _dil_heads(q_ref[rs, :], kc_ref[rs, :], vc_ref[rs, :], kp, vp, first)
        for half in range(A_GROUP_WIDTH // LANES):
            o_ref[half, rs, :] = o[:, half * LANES:(half + 1) * LANES]
            l_ref[half, rs, :] = lse[:, half * LANES:(half + 1) * LANES]


def _dil_attn_strided_kernel(q_ref, k_ref, v_ref, o_ref, l_ref, qf_ref, kf_ref, vf_ref, *, dil, has_prev):
    i = pl.program_id(1)
    step = pl.program_id(2)
    halves = A_GROUP_WIDTH // LANES
    slot = lax.rem(i, 2) if has_prev else 0

    @pl.when(step == 0)
    def _():
        for half in range(halves):
            cs = slice(half * LANES, (half + 1) * LANES)
            qf_ref[half] = q_ref[:, cs].astype(F32)
            kf_ref[slot, half] = k_ref[:, cs].astype(F32)
            vf_ref[slot, half] = v_ref[:, cs].astype(F32)

    if has_prev:
        @pl.when((step == 0) & (i == 0))
        def _():
            kf_ref[1] = jnp.zeros(kf_ref.shape[1:], F32)
            vf_ref[1] = jnp.zeros(vf_ref.shape[1:], F32)

    for u in range(DIL_TILES):
        r = step * DIL_TILES + u

        def rows_of_class(ref, *lead, r=r):
            return jnp.concatenate([ref[(*lead, half, pl.ds(r, Q_BLOCK, stride=dil), slice(None))]
                                    for half in range(halves)], axis=1).astype(BF16)

        kp = rows_of_class(kf_ref, 1 - slot) if has_prev else None
        vp = rows_of_class(vf_ref, 1 - slot) if has_prev else None
        o, lse = _dil_heads(rows_of_class(qf_ref), rows_of_class(kf_ref, slot), rows_of_class(vf_ref, slot),
                            kp, vp, i == 0)
        for half in range(halves):
            o_ref[half, pl.ds(r, Q_BLOCK, stride=dil), :] = o[:, half * LANES:(half + 1) * LANES]
            l_ref[half, pl.ds(r, Q_BLOCK, stride=dil), :] = lse[:, half * LANES:(half + 1) * LANES]


def _dil_attn(cb3, gi):
    bsz, s, _ = cb3.shape
    dil = A_GROUPS[gi][1]
    w = A_GROUP_WIDTH
    halves = w // LANES
    out = jax.ShapeDtypeStruct((bsz, halves, s, LANES), F32)
    name = f"dilated_attention_g{gi}"
    if dil == 1:
        step_rows = Q_BLOCK * DIL_TILES

        def col(off, prev):
            if prev:
                return pl.BlockSpec((None, Q_BLOCK, w),
                                    lambda b, i: (b, jnp.maximum(i * DIL_TILES - 1, 0), (off + gi * w) // w))
            return pl.BlockSpec((None, step_rows, w), lambda b, i: (b, i, (off + gi * w) // w))

        ospec = pl.BlockSpec((None, halves, step_rows, LANES), lambda b, i: (b, 0, i, 0))
        return pl.pallas_call(
            _dil_attn_dense_kernel,
            out_shape=(out, out),
            grid=(bsz, s // step_rows),
            in_specs=[col(BOFF_QA, False), col(BOFF_KA, False), col(BOFF_KA, True), col(BOFF_VA, False), col(BOFF_VA, True)],
            out_specs=(ospec, ospec),
            compiler_params=_params("parallel", "arbitrary"),
            name=name,
        )(cb3, cb3, cb3, cb3, cb3)

    rows = Q_BLOCK * dil
    tiles = s // rows
    has_prev = tiles > 1

    def col(off):
        return pl.BlockSpec((None, rows, w), lambda b, i, r: (b, i, (off + gi * w) // w))

    ospec = pl.BlockSpec((None, halves, rows, LANES), lambda b, i, r: (b, 0, i, 0))
    return pl.pallas_call(
        functools.partial(_dil_attn_strided_kernel, dil=dil, has_prev=has_prev),
        out_shape=(out, out),
        grid=(bsz, tiles, dil // DIL_TILES),
        in_specs=[col(BOFF_QA), col(BOFF_KA), col(BOFF_VA)],
        out_specs=(ospec, ospec),
        scratch_shapes=[pltpu.VMEM((halves, rows, LANES), F32),
                        pltpu.VMEM((2 if has_prev else 1, halves, rows, LANES), F32),
                        pltpu.VMEM((2 if has_prev else 1, halves, rows, LANES), F32)],
        compiler_params=_params("parallel", "arbitrary", "arbitrary"),
        name=name,
    )(cb3, cb3, cb3)


def _pool_kernel(x_ref, w_ref, sc_ref, o_ref):
    s = x_ref.shape[0]
    t = lax.broadcasted_iota(jnp.int32, (s, B_GROUP_DIM), 0)
    for gi, win in enumerate(B_WINDOWS):
        cs = slice(gi * B_GROUP_DIM, (gi + 1) * B_GROUP_DIM)
        x = x_ref[:, cs]
        acc = x
        k = 1
        while k < win:
            acc = acc + jnp.where(t >= k, pltpu.roll(acc, k, axis=0), 0.0)
            k *= 2
        cnt = jnp.minimum(t + 1, win).astype(F32)
        z = acc / cnt - x
        o_ref[:, cs] = _dot(z.astype(BF16), w_ref[gi]) * sc_ref[:, cs]


def _pool_mixer(cf3, pool_w, pool_scale, layer):
    bsz, s, _ = cf3.shape
    ng = len(B_WINDOWS)
    return pl.pallas_call(
        _pool_kernel,
        out_shape=jax.ShapeDtypeStruct((bsz, s, B_WIDTH), F32),
        grid=(bsz,),
        in_specs=[
            pl.BlockSpec((None, s, B_WIDTH), lambda b: (b, 0, FOFF_XB // B_WIDTH)),
            pl.BlockSpec((None, ng, B_GROUP_DIM, B_GROUP_DIM), lambda b: (layer, 0, 0, 0)),
            pl.BlockSpec((None, 1, B_WIDTH), lambda b: (layer, 0, 0)),
        ],
        out_specs=pl.BlockSpec((None, s, B_WIDTH), lambda b: (b, 0, 0)),
        compiler_params=_params("parallel"),
        name="pool_mixer",
    )(cf3, pool_w, pool_scale)


def _gelu_tanh(x):
    return 0.5 * x * (1.0 + jnp.tanh(np.float32(np.sqrt(2.0 / np.pi)) * (x + 0.044715 * (x * x * x))))


def _compress_one(z_ref, pe_ref, w1_ref, w2_ref, o_ref):
    nch = z_ref.shape[0] // CMP_STRIDE
    dh = C_HEAD_DIM
    first = jnp.zeros((nch, dh), F32)
    second = jnp.zeros((nch, dh), F32)
    for p in range(CMP_STRIDE):
        zp = z_ref[pl.ds(p, nch, stride=CMP_STRIDE), :]
        a = (zp + pe_ref[p:p + 1, :]).astype(BF16)
        b = (zp + pe_ref[CMP_STRIDE + p:CMP_STRIDE + p + 1, :]).astype(BF16)
        first = first + _dot(a, w1_ref[p * dh:(p + 1) * dh, :])
        second = second + _dot(b, w1_ref[(CMP_STRIDE + p) * dh:(CMP_STRIDE + p + 1) * dh, :])
    pre = first + pltpu.roll(second, nch - 1, axis=0)
    o_ref[...] = _dot(_gelu_tanh(pre).astype(BF16), w2_ref[...]).astype(BF16)


def _compress_kernel(zk_ref, zv_ref, pek_ref, w1k_ref, w2k_ref, pev_ref, w1v_ref, w2v_ref, kc_ref, vc_ref):
    _compress_one(zk_ref, pek_ref, w1k_ref, w2k_ref, kc_ref)
    _compress_one(zv_ref, pev_ref, w1v_ref, w2v_ref, vc_ref)


def _compress(cf3, pe_k, w1_k, w2_k, pe_v, w1_v, w2_v, layer):
    bsz, s, _ = cf3.shape
    nch = s // CMP_STRIDE
    dh = C_HEAD_DIM
    cdim = CMP_LEN * dh

    def col(off):
        return pl.BlockSpec((None, s, dh), lambda b, g: (b, 0, off // dh + g))

    def wspec(shape):
        return pl.BlockSpec((None,) + shape, lambda b, g: (layer,) + (0,) * len(shape))

    out = jax.ShapeDtypeStruct((bsz, C_KV_GROUPS, nch, dh), BF16)
    ospec = pl.BlockSpec((None, None, nch, dh), lambda b, g: (b, g, 0, 0))
    return pl.pallas_call(
        _compress_kernel,
        out_shape=(out, out),
        grid=(bsz, C_KV_GROUPS),
        in_specs=[col(FOFF_KCMP), col(FOFF_VCMP),
                  wspec((CMP_LEN, dh)), wspec((cdim, dh)), wspec((dh, dh)),
                  wspec((CMP_LEN, dh)), wspec((cdim, dh)), wspec((dh, dh))],
        out_specs=(ospec, ospec),
        compiler_params=_params("parallel", "parallel"),
        name="nsa_compress",
    )(cf3, cf3, pe_k, w1_k, w2_k, pe_v, w1_v, w2_v)


def _split3(x):
    hi = x.astype(BF16)
    r1 = x - hi.astype(F32)
    mid = r1.astype(BF16)
    lo = (r1 - mid.astype(F32)).astype(BF16)
    return hi, mid, lo


def _softmax_pv(s, v):
    m = jnp.max(s, axis=-1, keepdims=True)
    p = jnp.exp2(s - m)
    return _dot(p.astype(BF16), v) / jnp.sum(p, axis=-1, keepdims=True)


def _nsa_tile(t0, q, gates, kc_ref, vc_ref, kw_ref, vw_ref, ovt_ref, nb):
    hg = C_HEADS_PER_GROUP
    dh = C_HEAD_DIM
    row = t0 + lax.broadcasted_iota(jnp.int32, (Q_BLOCK, LANES), 0)
    lane = lax.broadcasted_iota(jnp.int32, (Q_BLOCK, LANES), 1)

    q3 = jnp.concatenate([q[:, h * dh:(h + 1) * dh] for h in range(hg)], axis=0)
    rows = [slice(h * Q_BLOCK, (h + 1) * Q_BLOCK) for h in range(hg)]
    gt = _sigmoid(gates)

    cvalid = lane * CMP_STRIDE + (CMP_LEN - 1) <= row
    s3 = _dot_nt(q3, kc_ref[...])
    vc = vc_ref[...]
    out = []
    psum = jnp.zeros((Q_BLOCK, LANES), F32)
    for h in range(hg):
        s = jnp.where(cvalid, s3[rows[h]], NEG)
        p = jnp.exp2(s - jnp.max(s, axis=-1, keepdims=True))
        p = jnp.where(cvalid, p / jnp.sum(p, axis=-1, keepdims=True), 0.0)
        out.append(gt[:, 3 * h:3 * h + 1] * _dot(p.astype(BF16), vc))
        psum = psum + p

    ovt = ovt_ref[...]
    imp_t = sum(_dot_nt(ovt, part) for part in _split3(psum))
    blk = lax.broadcasted_iota(jnp.int32, (nb, Q_BLOCK), 0)
    tb = lax.shift_right_logical(t0 + lax.broadcasted_iota(jnp.int32, (nb, Q_BLOCK), 1), int(np.log2(SEL_BLOCK)))
    forced = (blk == 0) | (blk == tb) | (blk == tb - 1)
    score = jnp.where(blk > tb, -1.0, imp_t + jnp.where(forced, FORCE_BONUS, 0.0))
    blk_f = blk.astype(F32)
    sel_t = jnp.zeros((nb, Q_BLOCK), F32)
    for _ in range(min(SEL_TOPN, nb)):
        mx = jnp.max(score, axis=0, keepdims=True)
        idx = jnp.min(jnp.where(score == mx, blk_f, float(nb)), axis=0, keepdims=True)
        hit = blk_f == idx
        sel_t = jnp.where(hit, 1.0, sel_t)
        score = jnp.where(hit, -3.0, score)
    sel = jnp.concatenate([sel_t, jnp.zeros((LANES - nb, Q_BLOCK), F32)], axis=0).T
    sel = jnp.where(sel > 0.5, 0.0, NEG).astype(BF16)

    wlen = WIN + Q_BLOCK
    w0 = pl.multiple_of(jnp.maximum(t0 - WIN, 0), Q_BLOCK)
    kpos = w0 + lax.broadcasted_iota(jnp.int32, (Q_BLOCK, wlen), 1)
    qpos = t0 + lax.broadcasted_iota(jnp.int32, (Q_BLOCK, wlen), 0)
    bias = jnp.where((kpos <= qpos) & (kpos > qpos - WIN), 0.0, NEG)
    s3 = _dot_nt(q3, kw_ref[pl.ds(w0, wlen), :])
    vw = vw_ref[pl.ds(w0, wlen), :]
    for h in range(hg):
        out[h] = out[h] + gt[:, 3 * h + 2:3 * h + 3] * _softmax_pv(s3[rows[h]] + bias, vw)
    return q3, gt, out, sel


def _nsa_kernel(q_ref, kc_ref, vc_ref, ks_ref, vs_ref, kw_ref, vw_ref, gt_ref, ovt_ref, ex_ref, o_ref, *, s_len):
    n = pl.program_id(2)
    hg = C_HEADS_PER_GROUP
    dh = C_HEAD_DIM
    tiles = []
    for u in range(NSA_TILES):
        rs = slice(u * Q_BLOCK, (u + 1) * Q_BLOCK)
        t0 = (n * NSA_TILES + u) * Q_BLOCK
        tiles.append((rs, t0) + _nsa_tile(t0, q_ref[rs, :], gt_ref[rs, :], kc_ref, vc_ref, kw_ref, vw_ref, ovt_ref,
                                          s_len // SEL_BLOCK))

    r = lax.broadcasted_iota(jnp.int32, (Q_BLOCK, Q_BLOCK), 0)
    c = lax.broadcasted_iota(jnp.int32, (Q_BLOCK, Q_BLOCK), 1)
    causal = jnp.where(c <= r, 0.0, NEG)
    for step in range(s_len // (Q_BLOCK * NSA_TILES)):

        @pl.when(n == step)
        def _(step=step):
            for u, (rs, _, q3, gt, out, sel) in enumerate(tiles):
                klen = (step * NSA_TILES + u + 1) * Q_BLOCK
                blocked = _dot(sel, ex_ref[:, :klen])
                bias = blocked[:, klen - Q_BLOCK:] + causal
                if klen > Q_BLOCK:
                    bias = jnp.concatenate([blocked[:, :klen - Q_BLOCK], bias], axis=1)
                s3 = _dot_nt(q3, ks_ref[:klen, :])
                vs = vs_ref[:klen, :]
                for h in range(hg):
                    hs = slice(h * Q_BLOCK, (h + 1) * Q_BLOCK)
                    o_ref[rs, h * dh:(h + 1) * dh] = out[h] + gt[:, 3 * h + 1:3 * h + 2] * _softmax_pv(s3[hs] + bias, vs)


def _nsa_constants(s):
    nch = s // CMP_STRIDE
    nb = s // SEL_BLOCK
    n_cmp = (s - CMP_LEN) // CMP_STRIDE + 1
    ci = np.arange(nch)[None, :] * CMP_STRIDE
    bj = np.arange(nb)[:, None] * SEL_BLOCK
    overlap_t = (ci < bj + SEL_BLOCK) & (ci + CMP_LEN > bj) & (np.arange(nch)[None, :] < n_cmp)
    expand = np.arange(LANES)[:, None] == (np.arange(s)[None, :] // SEL_BLOCK)
    return jnp.asarray(overlap_t, BF16), jnp.asarray(expand, BF16)


def _nsa(cb3, kc, vc, gates3):
    bsz, s, _ = cb3.shape
    nq = s // Q_BLOCK
    nch = s // CMP_STRIDE
    nb = s // SEL_BLOCK
    dh = C_HEAD_DIM
    step_rows = Q_BLOCK * NSA_TILES
    assert nch == LANES and nb <= LANES and nb % 8 == 0 and s % step_rows == 0 and s >= WIN + Q_BLOCK
    overlap_t, expand = _nsa_constants(s)

    def kv(off):
        return pl.BlockSpec((None, s, dh), lambda b, g, n: (b, 0, off // dh + g))

    cmp_spec = pl.BlockSpec((None, None, nch, dh), lambda b, g, n: (b, g, 0, 0))
    return pl.pallas_call(
        functools.partial(_nsa_kernel, s_len=s),
        out_shape=jax.ShapeDtypeStruct((bsz, s, C_WIDTH), F32),
        grid=(bsz, C_KV_GROUPS, s // step_rows),
        in_specs=[
            pl.BlockSpec((None, step_rows, C_GROUP_WIDTH), lambda b, g, n: (b, n, BOFF_QC // C_GROUP_WIDTH + g)),
            cmp_spec, cmp_spec,
            kv(BOFF_KSLC), kv(BOFF_VSLC), kv(BOFF_KWIN), kv(BOFF_VWIN),
            pl.BlockSpec((None, step_rows, LANES), lambda b, g, n: (b, n, g)),
            pl.BlockSpec((nb, nch), lambda b, g, n: (0, 0)),
            pl.BlockSpec((LANES, s), lambda b, g, n: (0, 0)),
        ],
        out_specs=pl.BlockSpec((None, step_rows, C_GROUP_WIDTH), lambda b, g, n: (b, n, g)),
        compiler_params=_params("parallel", "parallel", "arbitrary"),
        name="nsa_attention",
    )(cb3, kc, vc, cb3, cb3, cb3, cb3, gates3, overlap_t, expand)


def _merge_kernel(*refs):
    ng = len(A_GROUPS)
    oa_refs, la_refs = refs[:ng], refs[ng:2 * ng]
    zb_ref, oc_ref, g0_ref, g1_ref, g2_ref, x_ref, pa_ref, pb_ref, pc_ref, wo_ref, gn_ref, o_ref, xn_ref = refs[2 * ng:]

    def lanes(ref):
        return jnp.concatenate([ref[half] for half in range(ref.shape[0])], axis=1)

    lse = [lanes(r) for r in la_refs]
    m = functools.reduce(jnp.maximum, lse)
    e = [jnp.exp2(l - m) for l in lse]
    oa = sum(w * lanes(r) for w, r in zip(e, oa_refs)) / sum(e)
    ya = _dot(oa.astype(BF16), pa_ref[...])
    yb = _dot(zb_ref[...].astype(BF16), pb_ref[...])
    yc = _dot(oc_ref[...].astype(BF16), pc_ref[...])
    mix = _sigmoid(g0_ref[...]) * ya + _sigmoid(g1_ref[...]) * yb + _sigmoid(g2_ref[...]) * yc
    y = x_ref[...] + _dot(mix.astype(BF16), wo_ref[...])
    o_ref[...] = y
    xn_ref[...] = _rms_scaled(y, gn_ref[...]).astype(BF16)


def _merge(oa, la, zb, oc, cf, x, proj_a, proj_b, proj_c, w_out, g_next, layer, *, tm=256):
    m, d = x.shape
    assert d == D_GATE

    def rows(w):
        return pl.BlockSpec((tm, w), lambda i: (i, 0))

    def gate(k):
        return pl.BlockSpec((tm, d), lambda i: (i, FOFF_GM // d + k))

    def weight(k):
        return pl.BlockSpec((None, k, d), lambda i: (layer, 0, 0), pipeline_mode=pl.Buffered(1))

    _, halves, s, _ = oa[0].shape
    per_seq = s // tm
    group = pl.BlockSpec((None, halves, tm, LANES), lambda i: (i // per_seq, 0, i % per_seq, 0))

    return pl.pallas_call(
        _merge_kernel,
        out_shape=(jax.ShapeDtypeStruct((m, d), F32), jax.ShapeDtypeStruct((m, d), BF16)),
        grid=(m // tm,),
        in_specs=[group] * (2 * len(A_GROUPS))
        + [rows(B_WIDTH), rows(C_WIDTH), gate(0), gate(1), gate(2), rows(d),
           weight(A_GROUP_WIDTH), weight(B_WIDTH), weight(C_WIDTH), weight(d),
           pl.BlockSpec((None, 1, d), lambda i: (layer, 0, 0))],
        out_specs=(rows(d), rows(d)),
        compiler_params=_params("parallel"),
        name="merge",
    )(*oa, *la, zb, oc, cf, cf, cf, x, proj_a, proj_b, proj_c, w_out, g_next)


_A3 = 3 * A_WIDTH
_XB_END = _A3 + B_WIDTH
_QC_END = _XB_END + C_WIDTH
_CMP_END = _QC_END + 2 * C_KV_WIDTH
_KV_END = _QC_END + 6 * C_KV_WIDTH
_GM_START = _KV_END + N_GATES
_F32_RANGES = ((_GM_START, _GM_START + N_BRANCH * D_GATE), (_A3, _XB_END), (_QC_END, _CMP_END))
_BF16_RANGES = ((0, _A3), (_XB_END, _QC_END), (_CMP_END, _KV_END))


def _reorder_w_in(w_in):
    w_f32cols = jnp.concatenate([w_in[..., lo:hi] for lo, hi in _F32_RANGES], axis=-1).astype(BF16)
    w_bf16cols = jnp.concatenate([w_in[..., lo:hi] for lo, hi in _BF16_RANGES], axis=-1)
    per_group = 3 * C_HEADS_PER_GROUP
    pad = jnp.zeros(w_in.shape[:-1] + (LANES - per_group,), w_in.dtype)
    w_gate = jnp.concatenate(
        [piece for g in range(C_KV_GROUPS)
         for piece in (w_in[..., _KV_END + g * per_group:_KV_END + (g + 1) * per_group], pad)], axis=-1).astype(BF16)
    col_scale = np.ones((1, N_BF16_COLS), np.float32)
    col_scale[:, BOFF_QA:BOFF_QA + A_WIDTH] = A_HEAD_DIM ** -0.5 * LOG2E
    col_scale[:, BOFF_QC:BOFF_QC + C_WIDTH] = C_HEAD_DIM ** -0.5 * LOG2E
    return w_f32cols, w_bf16cols, w_gate, jnp.asarray(col_scale)


def _mixing(x, xn, layer, w_f32cols, w_bf16cols, w_gate, col_scale, pool_w, pool_scale, pe_k, w1_k, w2_k, pe_v, w1_v, w2_v,
            proj_a, proj_b, proj_c, w_out, g_next, bsz, s):
    m, d = x.shape
    cf, gates = _in_proj_f32(xn, w_f32cols, w_gate, layer)
    cb = _in_proj_bf16(xn, w_bf16cols, col_scale, layer)
    cf3 = cf.reshape(bsz, s, N_F32_COLS)
    cb3 = cb.reshape(bsz, s, N_BF16_COLS)
    gates3 = gates.reshape(bsz, s, C_KV_GROUPS * LANES)

    oa, la = zip(*[_dil_attn(cb3, gi) for gi in range(len(A_GROUPS))])
    zb = _pool_mixer(cf3, pool_w, pool_scale, layer).reshape(m, B_WIDTH)
    kc, vc = _compress(cf3, pe_k, w1_k, w2_k, pe_v, w1_v, w2_v, layer)
    oc = _nsa(cb3, kc, vc, gates3).reshape(m, C_WIDTH)
    return _merge(oa, la, zb, oc, cf, x, proj_a, proj_b, proj_c, w_out, g_next, layer)


def kernel(x, ffn1_norm, ffn1_wi, ffn1_wo, mix_norm, w_in, pool_w, pool_scale, cmp_pe_k, cmp_w1_k, cmp_w2_k,
           cmp_pe_v, cmp_w1_v, cmp_w2_v, proj_a, proj_b, proj_c, w_out, ffn2_norm, ffn2_wi, ffn2_wo, final_norm):
    bsz, s, d = x.shape
    depth = ffn1_wi.shape[0]
    for win, dil in A_GROUPS:
        assert win // dil == Q_BLOCK and s % (dil * Q_BLOCK) == 0
    assert CMP_LEN == 2 * CMP_STRIDE and all(w & (w - 1) == 0 for w in B_WINDOWS)

    bf = lambda w: w.astype(BF16)
    row3 = lambda g: g.reshape(g.shape[0], 1, g.shape[-1])
    w_f32cols, w_bf16cols, w_gate, col_scale = _reorder_w_in(w_in)
    ffn1_wo, ffn2_wo = bf(ffn1_wo), bf(ffn2_wo)
    pool_w, cmp_w1_k, cmp_w2_k, cmp_w1_v, cmp_w2_v = bf(pool_w), bf(cmp_w1_k), bf(cmp_w2_k), bf(cmp_w1_v), bf(cmp_w2_v)
    proj_a, proj_b, proj_c, w_out = bf(proj_a), bf(proj_b), bf(proj_c), bf(w_out)
    ffn1_norm, mix_norm, ffn2_norm, pool_scale = row3(ffn1_norm), row3(mix_norm), row3(ffn2_norm), row3(pool_scale)

    final_norm = final_norm.reshape(1, 1, d)
    x = x.reshape(bsz * s, d)
    xn = _row_norm(x, ffn1_norm, 0)
    for layer in range(depth):
        x, xn = _ffn_down(_ffn_up(xn, ffn1_wi, layer), ffn1_wo, x, mix_norm, layer, layer)
        x, xn = _mixing(x, xn, layer, w_f32cols, w_bf16cols, w_gate, col_scale, pool_w, pool_scale, cmp_pe_k, cmp_w1_k, cmp_w2_k,
                        cmp_pe_v, cmp_w1_v, cmp_w2_v, proj_a, proj_b, proj_c, w_out, ffn2_norm, bsz, s)
        h = _ffn_up(xn, ffn2_wi, layer)
        if layer + 1 < depth:
            x, xn = _ffn_down(h, ffn2_wo, x, ffn1_norm, layer, layer + 1)
        else:
            x = _ffn_down(h, ffn2_wo, x, final_norm, layer, 0, last=True)
    return x.reshape(bsz, s, d)
```

```python
import functools

import numpy as np
import jax
import jax.numpy as jnp
from jax import lax
from jax.experimental import pallas as pl
from jax.experimental.pallas import tpu as pltpu

F32 = jnp.float32
BF16 = jnp.bfloat16

EPS = 1e-6
NEG = -1e30
LOG2E = float(np.log2(np.e))
Q_BLOCK = 128
LANES = 128

A_GROUPS = ((128, 1), (512, 4), (2048, 16))
A_HEADS_PER_GROUP = 4
A_HEAD_DIM = 64
A_GROUP_WIDTH = A_HEADS_PER_GROUP * A_HEAD_DIM
A_WIDTH = A_GROUP_WIDTH * len(A_GROUPS)

B_WINDOWS = (2, 4, 8, 16)
B_GROUP_DIM = 128
B_WIDTH = B_GROUP_DIM * len(B_WINDOWS)

C_KV_GROUPS = 2
C_HEADS_PER_GROUP = 3
C_HEADS = C_KV_GROUPS * C_HEADS_PER_GROUP
C_HEAD_DIM = 128
C_GROUP_WIDTH = C_HEADS_PER_GROUP * C_HEAD_DIM
C_WIDTH = C_HEADS * C_HEAD_DIM
C_KV_WIDTH = C_KV_GROUPS * C_HEAD_DIM
CMP_LEN = 32
CMP_STRIDE = 16
SEL_BLOCK = 64
SEL_TOPN = 8
FORCE_BONUS = 100.0
WIN = 512
N_GATES = 3 * C_HEADS
NSA_TILES = 4
DIL_TILES = (8, 4, 8)

N_BRANCH = 3
D_GATE = 2048

FOFF_GM = 0
FOFF_XB = N_BRANCH * D_GATE
FOFF_KCMP = FOFF_XB + B_WIDTH
FOFF_VCMP = FOFF_KCMP + C_KV_WIDTH
N_F32_COLS = FOFF_VCMP + C_KV_WIDTH
BOFF_QA = 0
BOFF_KA = BOFF_QA + A_WIDTH
BOFF_VA = BOFF_KA + A_WIDTH
BOFF_QC = BOFF_VA + A_WIDTH
BOFF_KSLC = BOFF_QC + C_WIDTH
BOFF_VSLC = BOFF_KSLC + C_KV_WIDTH
BOFF_KWIN = BOFF_VSLC + C_KV_WIDTH
BOFF_VWIN = BOFF_KWIN + C_KV_WIDTH
N_BF16_COLS = BOFF_VWIN + C_KV_WIDTH

VMEM_LIMIT = 56 * 1024 * 1024
ROW_TILE = 1024
COL_TILE = 512
FFN_ROW_TILE = 1024

def _params(*sem):
    return pltpu.CompilerParams(dimension_semantics=sem, vmem_limit_bytes=VMEM_LIMIT)


def _dot(a, b):
    return jnp.dot(a, b, preferred_element_type=F32)


def _dot_nt(a, b):
    return lax.dot_general(a, b, (((1,), (1,)), ((), ())), preferred_element_type=F32)


def _sigmoid(x):
    return 0.5 * jnp.tanh(0.5 * x) + 0.5


def _rms_scaled(x, g):
    return x * lax.rsqrt(jnp.mean(x * x, axis=-1, keepdims=True) + EPS) * g


def _row_norm_kernel(x_ref, g_ref, o_ref):
    o_ref[...] = _rms_scaled(x_ref[...], g_ref[...]).astype(o_ref.dtype)


def _row_norm(x, g, layer, *, tm=512):
    m, d = x.shape
    return pl.pallas_call(
        _row_norm_kernel,
        out_shape=jax.ShapeDtypeStruct((m, d), BF16),
        grid=(m // tm,),
        in_specs=[pl.BlockSpec((tm, d), lambda i: (i, 0)), pl.BlockSpec((None, 1, d), lambda i: (layer, 0, 0))],
        out_specs=pl.BlockSpec((tm, d), lambda i: (i, 0)),
        compiler_params=_params("parallel"),
        name="row_norm",
    )(x, g)


def _ffn_up_kernel(xn_ref, wa_ref, wb_ref, h_ref, w_ref):
    tn = h_ref.shape[-1]

    @pl.when(pl.program_id(1) == 0)
    def _():
        w_ref[:, :tn] = wa_ref[...].astype(BF16)
        w_ref[:, tn:] = wb_ref[...].astype(BF16)

    xn = xn_ref[...]
    a = _dot(xn, w_ref[:, :tn])
    b = _dot(xn, w_ref[:, tn:])
    h_ref[...] = (a * _sigmoid(a) * b).astype(BF16)


def _ffn_up(xn, wi, layer, *, tm=FFN_ROW_TILE, tn=COL_TILE):
    m, d = xn.shape
    d_ff = wi.shape[-1] // 2
    nj = d_ff // tn
    return pl.pallas_call(
        _ffn_up_kernel,
        out_shape=jax.ShapeDtypeStruct((m, d_ff), BF16),
        grid=(nj, m // tm),
        in_specs=[
            pl.BlockSpec((tm, d), lambda j, i: (i, 0)),
            pl.BlockSpec((None, d, tn), lambda j, i: (layer, 0, j)),
            pl.BlockSpec((None, d, tn), lambda j, i: (layer, 0, j + nj)),
        ],
        out_specs=pl.BlockSpec((tm, tn), lambda j, i: (i, j)),
        scratch_shapes=[pltpu.VMEM((d, 2 * tn), BF16)],
        compiler_params=_params("parallel", "arbitrary"),
        name="ffn_up",
    )(xn, wi, wi)


def _ffn_down_kernel(h_ref, w_ref, r_ref, g_ref, *o_refs, last):
    y = r_ref[...] + 0.5 * _dot(h_ref[...], w_ref[...])
    if last:
        o_refs[0][...] = _rms_scaled(y, g_ref[...])
    else:
        o_refs[0][...] = y
        o_refs[1][...] = _rms_scaled(y, g_ref[...]).astype(BF16)


def _ffn_down(h, wo, res, g_next, layer, g_layer, *, last=False, tm=256):
    m, d_ff = h.shape
    d = res.shape[-1]
    rows = pl.BlockSpec((tm, d), lambda i: (i, 0))
    x_out = jax.ShapeDtypeStruct((m, d), F32)
    return pl.pallas_call(
        functools.partial(_ffn_down_kernel, last=last),
        out_shape=x_out if last else (x_out, jax.ShapeDtypeStruct((m, d), BF16)),
        grid=(m // tm,),
        in_specs=[
            pl.BlockSpec((tm, d_ff), lambda i: (i, 0)),
            pl.BlockSpec((None, d_ff, d), lambda i: (layer, 0, 0), pipeline_mode=pl.Buffered(1)),
            rows,
            pl.BlockSpec((None, 1, d), lambda i: (g_layer, 0, 0)),
        ],
        out_specs=rows if last else (rows, rows),
        compiler_params=_params("parallel"),
        name="ffn_down",
    )(h, wo, res, g_next)


def _in_proj_kernel(xn_ref, wgm_ref, wr_ref, wg_ref, sc_ref, of_ref, ob_ref, og_ref, *, n_gm_tiles, n_f32_tiles):
    j = pl.program_id(1)

    @pl.when(j == 0)
    def _():
        og_ref[...] = _dot(xn_ref[...], wg_ref[...])

    @pl.when(j < n_gm_tiles)
    def _():
        of_ref[...] = _dot(xn_ref[...], wgm_ref[...])

    @pl.when((j >= n_gm_tiles) & (j < n_f32_tiles))
    def _():
        of_ref[...] = _dot(xn_ref[...], wr_ref[...])

    @pl.when(j >= n_f32_tiles)
    def _():
        ob_ref[...] = (_dot(xn_ref[...], wr_ref[...]) * sc_ref[...]).astype(BF16)


def _in_proj(xn, w_gm, w_rest, w_gate, col_scale, layer, *, tm=ROW_TILE, tn=2 * COL_TILE):
    m, d = xn.shape
    ng = w_gate.shape[-1]
    ngm = w_gm.shape[-1] // tn
    nf = N_F32_COLS // tn
    nb = N_BF16_COLS // tn
    assert w_gm.shape[-1] + w_rest.shape[-1] == N_F32_COLS + N_BF16_COLS and ngm < nf
    return pl.pallas_call(
        functools.partial(_in_proj_kernel, n_gm_tiles=ngm, n_f32_tiles=nf),
        out_shape=(jax.ShapeDtypeStruct((m, N_F32_COLS), F32), jax.ShapeDtypeStruct((m, N_BF16_COLS), BF16),
                   jax.ShapeDtypeStruct((m, ng), F32)),
        grid=(m // tm, nf + nb),
        in_specs=[
            pl.BlockSpec((tm, d), lambda i, j: (i, 0)),
            pl.BlockSpec((None, d, tn), lambda i, j: (layer, 0, jnp.minimum(j, ngm - 1))),
            pl.BlockSpec((None, d, tn), lambda i, j: (layer, 0, jnp.maximum(j - ngm, 0))),
            pl.BlockSpec((None, d, ng), lambda i, j: (layer, 0, 0)),
            pl.BlockSpec((1, tn), lambda i, j: (0, j)),
        ],
        out_specs=(
            pl.BlockSpec((tm, tn), lambda i, j: (i, jnp.minimum(j, nf - 1))),
            pl.BlockSpec((tm, tn), lambda i, j: (i, jnp.maximum(j - nf, 0))),
            pl.BlockSpec((tm, ng), lambda i, j: (i, 0)),
        ),
        compiler_params=_params("parallel", "arbitrary"),
        name="in_proj",
    )(xn, w_gm, w_rest, w_gate, col_scale)


def _dil_heads(q, kc, vc, kp, vp, first):
    r = lax.broadcasted_iota(jnp.int32, (Q_BLOCK, Q_BLOCK), 0)
    c = lax.broadcasted_iota(jnp.int32, (Q_BLOCK, Q_BLOCK), 1)
    bias_cur = jnp.where(c <= r, 0.0, NEG)
    if kp is not None:
        bias_prev = jnp.where(c >= r + jnp.where(first, Q_BLOCK, 0), 0.0, NEG)
    head = lax.shift_right_logical(lax.broadcasted_iota(jnp.int32, (Q_BLOCK, A_GROUP_WIDTH), 1),
                                   int(np.log2(A_HEAD_DIM)))
    o_acc = jnp.zeros((Q_BLOCK, A_GROUP_WIDTH), F32)
    l_acc = jnp.zeros((Q_BLOCK, A_GROUP_WIDTH), F32)
    for h in range(A_HEADS_PER_GROUP):
        mine = head == h
        qh = q * jnp.where(mine, 1.0, 0.0).astype(BF16)
        s_c = _dot_nt(qh, kc) + bias_cur
        m = jnp.max(s_c, axis=-1, keepdims=True)
        if kp is not None:
            s_p = _dot_nt(qh, kp) + bias_prev
            m = jnp.maximum(m, jnp.max(s_p, axis=-1, keepdims=True))
        p_c = jnp.exp2(s_c - m)
        l = jnp.sum(p_c, axis=-1, keepdims=True)
        pv = _dot(p_c.astype(BF16), vc)
        if kp is not None:
            p_p = jnp.exp2(s_p - m)
            l = l + jnp.sum(p_p, axis=-1, keepdims=True)
            pv = pv + _dot(p_p.astype(BF16), vp)
        o_acc = jnp.where(mine, pv / l, o_acc)
        l_acc = jnp.where(mine, m + jnp.log2(l), l_acc)
    return o_acc, l_acc


def _dil_attn_dense_kernel(q_ref, kc_ref, kp_ref, vc_ref, vp_ref, o_ref, l_ref, *, unroll):
    for u in range(unroll):
        rs = slice(u * Q_BLOCK, (u + 1) * Q_BLOCK)
        ps = slice((u - 1) * Q_BLOCK, u * Q_BLOCK)
        kp, vp = (kp_ref[...], vp_ref[...]) if u == 0 else (kc_ref[ps, :], vc_ref[ps, :])
        first = (pl.program_id(1) == 0) if u == 0 else False
        o, lse = _dil_heads(q_ref[rs, :], kc_ref[rs, :], vc_ref[rs, :], kp, vp, first)
        for half in range(A_GROUP_WIDTH // LANES):
            o_ref[half, rs, :] = o[:, half * LANES:(half + 1) * LANES]
            l_ref[half, rs, :] = lse[:, half * LANES:(half + 1) * LANES]


def _dil_attn_strided_kernel(q_ref, k_ref, v_ref, o_ref, l_ref, qf_ref, kf_ref, vf_ref, *, dil, has_prev, unroll):
    i = pl.program_id(1)
    step = pl.program_id(2)
    halves = A_GROUP_WIDTH // LANES
    slot = lax.rem(i, 2) if has_prev else 0

    @pl.when(step == 0)
    def _():
        for half in range(halves):
            cs = slice(half * LANES, (half + 1) * LANES)
            qf_ref[half] = q_ref[:, cs].astype(F32)
            kf_ref[slot, half] = k_ref[:, cs].astype(F32)
            vf_ref[slot, half] = v_ref[:, cs].astype(F32)

    if has_prev:
        @pl.when((step == 0) & (i == 0))
        def _():
            kf_ref[1] = jnp.zeros(kf_ref.shape[1:], F32)
            vf_ref[1] = jnp.zeros(vf_ref.shape[1:], F32)

    for u in range(unroll):
        r = step * unroll + u

        def rows_of_class(ref, *lead, r=r):
            return jnp.concatenate([ref[(*lead, half, pl.ds(r, Q_BLOCK, stride=dil), slice(None))]
                                    for half in range(halves)], axis=1).astype(BF16)

        kp = rows_of_class(kf_ref, 1 - slot) if has_prev else None
        vp = rows_of_class(vf_ref, 1 - slot) if has_prev else None
        o, lse = _dil_heads(rows_of_class(qf_ref), rows_of_class(kf_ref, slot), rows_of_class(vf_ref, slot),
                            kp, vp, i == 0)
        for half in range(halves):
            o_ref[half, pl.ds(r, Q_BLOCK, stride=dil), :] = o[:, half * LANES:(half + 1) * LANES]
            l_ref[half, pl.ds(r, Q_BLOCK, stride=dil), :] = lse[:, half * LANES:(half + 1) * LANES]


def _dil_attn(cb3, gi):
    bsz, s, _ = cb3.shape
    dil = A_GROUPS[gi][1]
    w = A_GROUP_WIDTH
    halves = w // LANES
    out = jax.ShapeDtypeStruct((bsz, halves, s, LANES), F32)
    name = f"dilated_attention_g{gi}"
    unroll = DIL_TILES[gi]
    if dil == 1:
        step_rows = Q_BLOCK * unroll

        def col(off, prev):
            if prev:
                return pl.BlockSpec((None, Q_BLOCK, w),
                                    lambda b, i: (b, jnp.maximum(i * unroll - 1, 0), (off + gi * w) // w))
            return pl.BlockSpec((None, step_rows, w), lambda b, i: (b, i, (off + gi * w) // w))

        ospec = pl.BlockSpec((None, halves, step_rows, LANES), lambda b, i: (b, 0, i, 0))
        return pl.pallas_call(
            functools.partial(_dil_attn_dense_kernel, unroll=unroll),
            out_shape=(out, out),
            grid=(bsz, s // step_rows),
            in_specs=[col(BOFF_QA, False), col(BOFF_KA, False), col(BOFF_KA, True), col(BOFF_VA, False), col(BOFF_VA, True)],
            out_specs=(ospec, ospec),
            compiler_params=_params("parallel", "arbitrary"),
            name=name,
        )(cb3, cb3, cb3, cb3, cb3)

    rows = Q_BLOCK * dil
    tiles = s // rows
    has_prev = tiles > 1

    def col(off):
        return pl.BlockSpec((None, rows, w), lambda b, i, r: (b, i, (off + gi * w) // w))

    ospec = pl.BlockSpec((None, halves, rows, LANES), lambda b, i, r: (b, 0, i, 0))
    return pl.pallas_call(
        functools.partial(_dil_attn_strided_kernel, dil=dil, has_prev=has_prev, unroll=unroll),
        out_shape=(out, out),
        grid=(bsz, tiles, dil // unroll),
        in_specs=[col(BOFF_QA), col(BOFF_KA), col(BOFF_VA)],
        out_specs=(ospec, ospec),
        scratch_shapes=[pltpu.VMEM((halves, rows, LANES), F32),
                        pltpu.VMEM((2 if has_prev else 1, halves, rows, LANES), F32),
                        pltpu.VMEM((2 if has_prev else 1, halves, rows, LANES), F32)],
        compiler_params=_params("parallel", "arbitrary", "arbitrary"),
        name=name,
    )(cb3, cb3, cb3)


def _pool_kernel(x_ref, w_ref, sc_ref, o_ref):
    s = x_ref.shape[0]
    t = lax.broadcasted_iota(jnp.int32, (s, B_GROUP_DIM), 0)
    for gi, win in enumerate(B_WINDOWS):
        cs = slice(gi * B_GROUP_DIM, (gi + 1) * B_GROUP_DIM)
        x = x_ref[:, cs]
        acc = x
        k = 1
        while k < win:
            acc = acc + jnp.where(t >= k, pltpu.roll(acc, k, axis=0), 0.0)
            k *= 2
        cnt = jnp.minimum(t + 1, win).astype(F32)
        z = acc / cnt - x
        o_ref[:, cs] = _dot(z.astype(BF16), w_ref[gi]) * sc_ref[:, cs]


def _pool_mixer(cf3, pool_w, pool_scale, layer):
    bsz, s, _ = cf3.shape
    ng = len(B_WINDOWS)
    return pl.pallas_call(
        _pool_kernel,
        out_shape=jax.ShapeDtypeStruct((bsz, s, B_WIDTH), F32),
        grid=(bsz,),
        in_specs=[
            pl.BlockSpec((None, s, B_WIDTH), lambda b: (b, 0, FOFF_XB // B_WIDTH)),
            pl.BlockSpec((None, ng, B_GROUP_DIM, B_GROUP_DIM), lambda b: (layer, 0, 0, 0)),
            pl.BlockSpec((None, 1, B_WIDTH), lambda b: (layer, 0, 0)),
        ],
        out_specs=pl.BlockSpec((None, s, B_WIDTH), lambda b: (b, 0, 0)),
        compiler_params=_params("parallel"),
        name="pool_mixer",
    )(cf3, pool_w, pool_scale)


def _gelu_tanh(x):
    return 0.5 * x * (1.0 + jnp.tanh(np.float32(np.sqrt(2.0 / np.pi)) * (x + 0.044715 * (x * x * x))))


def _compress_one(z_ref, pe_ref, w1_ref, w2_ref, o_ref):
    nch = z_ref.shape[0] // CMP_STRIDE
    dh = C_HEAD_DIM
    first = jnp.zeros((nch, dh), F32)
    second = jnp.zeros((nch, dh), F32)
    for p in range(CMP_STRIDE):
        zp = z_ref[pl.ds(p, nch, stride=CMP_STRIDE), :]
        a = (zp + pe_ref[p:p + 1, :]).astype(BF16)
        b = (zp + pe_ref[CMP_STRIDE + p:CMP_STRIDE + p + 1, :]).astype(BF16)
        first = first + _dot(a, w1_ref[p * dh:(p + 1) * dh, :])
        second = second + _dot(b, w1_ref[(CMP_STRIDE + p) * dh:(CMP_STRIDE + p + 1) * dh, :])
    pre = first + pltpu.roll(second, nch - 1, axis=0)
    o_ref[...] = _dot(_gelu_tanh(pre).astype(BF16), w2_ref[...]).astype(BF16)


def _compress_kernel(zk_ref, zv_ref, pek_ref, w1k_ref, w2k_ref, pev_ref, w1v_ref, w2v_ref, kc_ref, vc_ref):
    _compress_one(zk_ref, pek_ref, w1k_ref, w2k_ref, kc_ref)
    _compress_one(zv_ref, pev_ref, w1v_ref, w2v_ref, vc_ref)


def _compress(cf3, pe_k, w1_k, w2_k, pe_v, w1_v, w2_v, layer):
    bsz, s, _ = cf3.shape
    nch = s // CMP_STRIDE
    dh = C_HEAD_DIM
    cdim = CMP_LEN * dh

    def col(off):
        return pl.BlockSpec((None, s, dh), lambda b, g: (b, 0, off // dh + g))

    def wspec(shape):
        return pl.BlockSpec((None,) + shape, lambda b, g: (layer,) + (0,) * len(shape))

    out = jax.ShapeDtypeStruct((bsz, C_KV_GROUPS, nch, dh), BF16)
    ospec = pl.BlockSpec((None, None, nch, dh), lambda b, g: (b, g, 0, 0))
    return pl.pallas_call(
        _compress_kernel,
        out_shape=(out, out),
        grid=(bsz, C_KV_GROUPS),
        in_specs=[col(FOFF_KCMP), col(FOFF_VCMP),
                  wspec((CMP_LEN, dh)), wspec((cdim, dh)), wspec((dh, dh)),
                  wspec((CMP_LEN, dh)), wspec((cdim, dh)), wspec((dh, dh))],
        out_specs=(ospec, ospec),
        compiler_params=_params("parallel", "parallel"),
        name="nsa_compress",
    )(cf3, cf3, pe_k, w1_k, w2_k, pe_v, w1_v, w2_v)


def _split3(x):
    hi = x.astype(BF16)
    r1 = x - hi.astype(F32)
    mid = r1.astype(BF16)
    lo = (r1 - mid.astype(F32)).astype(BF16)
    return hi, mid, lo


def _softmax_pv(s, v):
    m = jnp.max(s, axis=-1, keepdims=True)
    p = jnp.exp2(s - m)
    return _dot(p.astype(BF16), v) / jnp.sum(p, axis=-1, keepdims=True)


def _nsa_tile(t0, q, gates, kc_ref, vc_ref, kw_ref, vw_ref, ovt_ref, nb):
    hg = C_HEADS_PER_GROUP
    dh = C_HEAD_DIM
    row = t0 + lax.broadcasted_iota(jnp.int32, (Q_BLOCK, LANES), 0)
    lane = lax.broadcasted_iota(jnp.int32, (Q_BLOCK, LANES), 1)

    q3 = jnp.concatenate([q[:, h * dh:(h + 1) * dh] for h in range(hg)], axis=0)
    rows = [slice(h * Q_BLOCK, (h + 1) * Q_BLOCK) for h in range(hg)]
    gt = _sigmoid(gates)

    cvalid = lane * CMP_STRIDE + (CMP_LEN - 1) <= row
    s3 = _dot_nt(q3, kc_ref[...])
    vc = vc_ref[...]
    out = []
    psum = jnp.zeros((Q_BLOCK, LANES), F32)
    for h in range(hg):
        s = jnp.where(cvalid, s3[rows[h]], NEG)
        p = jnp.exp2(s - jnp.max(s, axis=-1, keepdims=True))
        p = jnp.where(cvalid, p / jnp.sum(p, axis=-1, keepdims=True), 0.0)
        out.append(gt[:, 3 * h:3 * h + 1] * _dot(p.astype(BF16), vc))
        psum = psum + p

    ovt = ovt_ref[...]
    imp_t = sum(_dot_nt(ovt, part) for part in _split3(psum))
    blk = lax.broadcasted_iota(jnp.int32, (nb, Q_BLOCK), 0)
    tb = lax.shift_right_logical(t0 + lax.broadcasted_iota(jnp.int32, (nb, Q_BLOCK), 1), int(np.log2(SEL_BLOCK)))
    forced = (blk == 0) | (blk == tb) | (blk == tb - 1)
    score = jnp.where(blk > tb, -1.0, imp_t + jnp.where(forced, FORCE_BONUS, 0.0))
    blk_f = blk.astype(F32)
    sel_t = jnp.zeros((nb, Q_BLOCK), F32)
    for _ in range(min(SEL_TOPN, nb)):
        mx = jnp.max(score, axis=0, keepdims=True)
        idx = jnp.min(jnp.where(score == mx, blk_f, float(nb)), axis=0, keepdims=True)
        hit = blk_f == idx
        sel_t = jnp.where(hit, 1.0, sel_t)
        score = jnp.where(hit, -3.0, score)
    sel = jnp.concatenate([sel_t, jnp.zeros((LANES - nb, Q_BLOCK), F32)], axis=0).T
    sel = jnp.where(sel > 0.5, 0.0, NEG).astype(BF16)

    wlen = WIN + Q_BLOCK
    w0 = pl.multiple_of(jnp.maximum(t0 - WIN, 0), Q_BLOCK)
    kpos = w0 + lax.broadcasted_iota(jnp.int32, (Q_BLOCK, wlen), 1)
    qpos = t0 + lax.broadcasted_iota(jnp.int32, (Q_BLOCK, wlen), 0)
    bias = jnp.where((kpos <= qpos) & (kpos > qpos - WIN), 0.0, NEG)
    s3 = _dot_nt(q3, kw_ref[pl.ds(w0, wlen), :])
    vw = vw_ref[pl.ds(w0, wlen), :]
    for h in range(hg):
        out[h] = out[h] + gt[:, 3 * h + 2:3 * h + 3] * _softmax_pv(s3[rows[h]] + bias, vw)
    return q3, gt, out, sel


def _nsa_kernel(q_ref, kc_ref, vc_ref, ks_ref, vs_ref, kw_ref, vw_ref, gt_ref, ovt_ref, ex_ref, o_ref, *, s_len):
    n = pl.program_id(2)
    hg = C_HEADS_PER_GROUP
    dh = C_HEAD_DIM
    tiles = []
    for u in range(NSA_TILES):
        rs = slice(u * Q_BLOCK, (u + 1) * Q_BLOCK)
        t0 = (n * NSA_TILES + u) * Q_BLOCK
        tiles.append((rs, t0) + _nsa_tile(t0, q_ref[rs, :], gt_ref[rs, :], kc_ref, vc_ref, kw_ref, vw_ref, ovt_ref,
                                          s_len // SEL_BLOCK))

    r = lax.broadcasted_iota(jnp.int32, (Q_BLOCK, Q_BLOCK), 0)
    c = lax.broadcasted_iota(jnp.int32, (Q_BLOCK, Q_BLOCK), 1)
    causal = jnp.where(c <= r, 0.0, NEG)
    for step in range(s_len // (Q_BLOCK * NSA_TILES)):

        @pl.when(n == step)
        def _(step=step):
            for u, (rs, _, q3, gt, out, sel) in enumerate(tiles):
                klen = (step * NSA_TILES + u + 1) * Q_BLOCK
                blocked = _dot(sel, ex_ref[:, :klen])
                bias = blocked[:, klen - Q_BLOCK:] + causal
                if klen > Q_BLOCK:
                    bias = jnp.concatenate([blocked[:, :klen - Q_BLOCK], bias], axis=1)
                s3 = _dot_nt(q3, ks_ref[:klen, :])
                vs = vs_ref[:klen, :]
                for h in range(hg):
                    hs = slice(h * Q_BLOCK, (h + 1) * Q_BLOCK)
                    o_ref[rs, h * dh:(h + 1) * dh] = out[h] + gt[:, 3 * h + 1:3 * h + 2] * _softmax_pv(s3[hs] + bias, vs)


def _nsa_constants(s):
    nch = s // CMP_STRIDE
    nb = s // SEL_BLOCK
    n_cmp = (s - CMP_LEN) // CMP_STRIDE + 1
    ci = np.arange(nch)[None, :] * CMP_STRIDE
    bj = np.arange(nb)[:, None] * SEL_BLOCK
    overlap_t = (ci < bj + SEL_BLOCK) & (ci + CMP_LEN > bj) & (np.arange(nch)[None, :] < n_cmp)
    expand = np.arange(LANES)[:, None] == (np.arange(s)[None, :] // SEL_BLOCK)
    return jnp.asarray(overlap_t, BF16), jnp.asarray(expand, BF16)


def _nsa(cb3, kc, vc, gates3):
    bsz, s, _ = cb3.shape
    nq = s // Q_BLOCK
    nch = s // CMP_STRIDE
    nb = s // SEL_BLOCK
    dh = C_HEAD_DIM
    step_rows = Q_BLOCK * NSA_TILES
    assert nch == LANES and nb <= LANES and nb % 8 == 0 and s % step_rows == 0 and s >= WIN + Q_BLOCK
    overlap_t, expand = _nsa_constants(s)

    def kv(off):
        return pl.BlockSpec((None, s, dh), lambda b, g, n: (b, 0, off // dh + g))

    cmp_spec = pl.BlockSpec((None, None, nch, dh), lambda b, g, n: (b, g, 0, 0))
    return pl.pallas_call(
        functools.partial(_nsa_kernel, s_len=s),
        out_shape=jax.ShapeDtypeStruct((bsz, s, C_WIDTH), F32),
        grid=(bsz, C_KV_GROUPS, s // step_rows),
        in_specs=[
            pl.BlockSpec((None, step_rows, C_GROUP_WIDTH), lambda b, g, n: (b, n, BOFF_QC // C_GROUP_WIDTH + g)),
            cmp_spec, cmp_spec,
            kv(BOFF_KSLC), kv(BOFF_VSLC), kv(BOFF_KWIN), kv(BOFF_VWIN),
            pl.BlockSpec((None, step_rows, LANES), lambda b, g, n: (b, n, g)),
            pl.BlockSpec((nb, nch), lambda b, g, n: (0, 0)),
            pl.BlockSpec((LANES, s), lambda b, g, n: (0, 0)),
        ],
        out_specs=pl.BlockSpec((None, step_rows, C_GROUP_WIDTH), lambda b, g, n: (b, n, g)),
        compiler_params=_params("parallel", "parallel", "arbitrary"),
        name="nsa_attention",
    )(cb3, kc, vc, cb3, cb3, cb3, cb3, gates3, overlap_t, expand)


def _merge_kernel(*refs):
    ng = len(A_GROUPS)
    oa_refs, la_refs = refs[:ng], refs[ng:2 * ng]
    zb_ref, oc_ref, g0_ref, g1_ref, g2_ref, x_ref, pa_ref, pb_ref, pc_ref, wo_ref, gn_ref, o_ref, xn_ref = refs[2 * ng:]

    def lanes(ref):
        return jnp.concatenate([ref[half] for half in range(ref.shape[0])], axis=1)

    lse = [lanes(r) for r in la_refs]
    m = functools.reduce(jnp.maximum, lse)
    e = [jnp.exp2(l - m) for l in lse]
    oa = sum(w * lanes(r) for w, r in zip(e, oa_refs)) / sum(e)
    ya = _dot(oa.astype(BF16), pa_ref[...])
    yb = _dot(zb_ref[...].astype(BF16), pb_ref[...])
    yc = _dot(oc_ref[...].astype(BF16), pc_ref[...])
    mix = _sigmoid(g0_ref[...]) * ya + _sigmoid(g1_ref[...]) * yb + _sigmoid(g2_ref[...]) * yc
    y = x_ref[...] + _dot(mix.astype(BF16), wo_ref[...])
    o_ref[...] = y
    xn_ref[...] = _rms_scaled(y, gn_ref[...]).astype(BF16)


def _merge(oa, la, zb, oc, cf, x, proj_a, proj_b, proj_c, w_out, g_next, layer, *, tm=256):
    m, d = x.shape
    assert d == D_GATE

    def rows(w):
        return pl.BlockSpec((tm, w), lambda i: (i, 0))

    def gate(k):
        return pl.BlockSpec((tm, d), lambda i: (i, FOFF_GM // d + k))

    def weight(k):
        return pl.BlockSpec((None, k, d), lambda i: (layer, 0, 0), pipeline_mode=pl.Buffered(1))

    _, halves, s, _ = oa[0].shape
    per_seq = s // tm
    group = pl.BlockSpec((None, halves, tm, LANES), lambda i: (i // per_seq, 0, i % per_seq, 0))

    return pl.pallas_call(
        _merge_kernel,
        out_shape=(jax.ShapeDtypeStruct((m, d), F32), jax.ShapeDtypeStruct((m, d), BF16)),
        grid=(m // tm,),
        in_specs=[group] * (2 * len(A_GROUPS))
        + [rows(B_WIDTH), rows(C_WIDTH), gate(0), gate(1), gate(2), rows(d),
           weight(A_GROUP_WIDTH), weight(B_WIDTH), weight(C_WIDTH), weight(d),
           pl.BlockSpec((None, 1, d), lambda i: (layer, 0, 0))],
        out_specs=(rows(d), rows(d)),
        compiler_params=_params("parallel"),
        name="merge",
    )(*oa, *la, zb, oc, cf, cf, cf, x, proj_a, proj_b, proj_c, w_out, g_next)


_A3 = 3 * A_WIDTH
_XB_END = _A3 + B_WIDTH
_QC_END = _XB_END + C_WIDTH
_CMP_END = _QC_END + 2 * C_KV_WIDTH
_KV_END = _QC_END + 6 * C_KV_WIDTH
_GM_START = _KV_END + N_GATES
_REST_RANGES = ((_A3, _XB_END), (_QC_END, _CMP_END), (0, _A3), (_XB_END, _QC_END), (_CMP_END, _KV_END))


def _reorder_w_in(w_in):
    w_gm = w_in[..., _GM_START:].astype(BF16)
    w_rest = jnp.concatenate([w_in[..., lo:hi] for lo, hi in _REST_RANGES], axis=-1).astype(BF16)
    per_group = 3 * C_HEADS_PER_GROUP
    pad = jnp.zeros(w_in.shape[:-1] + (LANES - per_group,), w_in.dtype)
    w_gate = jnp.concatenate(
        [piece for g in range(C_KV_GROUPS)
         for piece in (w_in[..., _KV_END + g * per_group:_KV_END + (g + 1) * per_group], pad)], axis=-1).astype(BF16)
    col_scale = np.ones((1, N_F32_COLS + N_BF16_COLS), np.float32)
    col_scale[:, N_F32_COLS + BOFF_QA:N_F32_COLS + BOFF_QA + A_WIDTH] = A_HEAD_DIM ** -0.5 * LOG2E
    col_scale[:, N_F32_COLS + BOFF_QC:N_F32_COLS + BOFF_QC + C_WIDTH] = C_HEAD_DIM ** -0.5 * LOG2E
    return w_gm, w_rest, w_gate, jnp.asarray(col_scale)


def _mixing(x, xn, layer, w_gm, w_rest, w_gate, col_scale, pool_w, pool_scale, pe_k, w1_k, w2_k, pe_v, w1_v, w2_v,
            proj_a, proj_b, proj_c, w_out, g_next, bsz, s):
    m, d = x.shape
    cf, cb, gates = _in_proj(xn, w_gm, w_rest, w_gate, col_scale, layer)
    cf3 = cf.reshape(bsz, s, N_F32_COLS)
    cb3 = cb.reshape(bsz, s, N_BF16_COLS)
    gates3 = gates.reshape(bsz, s, C_KV_GROUPS * LANES)

    oa, la = zip(*[_dil_attn(cb3, gi) for gi in range(len(A_GROUPS))])
    zb = _pool_mixer(cf3, pool_w, pool_scale, layer).reshape(m, B_WIDTH)
    kc, vc = _compress(cf3, pe_k, w1_k, w2_k, pe_v, w1_v, w2_v, layer)
    oc = _nsa(cb3, kc, vc, gates3).reshape(m, C_WIDTH)
    return _merge(oa, la, zb, oc, cf, x, proj_a, proj_b, proj_c, w_out, g_next, layer)


def kernel(x, ffn1_norm, ffn1_wi, ffn1_wo, mix_norm, w_in, pool_w, pool_scale, cmp_pe_k, cmp_w1_k, cmp_w2_k,
           cmp_pe_v, cmp_w1_v, cmp_w2_v, proj_a, proj_b, proj_c, w_out, ffn2_norm, ffn2_wi, ffn2_wo, final_norm):
    bsz, s, d = x.shape
    depth = ffn1_wi.shape[0]
    for win, dil in A_GROUPS:
        assert win // dil == Q_BLOCK and s % (dil * Q_BLOCK) == 0
    assert CMP_LEN == 2 * CMP_STRIDE and all(w & (w - 1) == 0 for w in B_WINDOWS)

    bf = lambda w: w.astype(BF16)
    row3 = lambda g: g.reshape(g.shape[0], 1, g.shape[-1])
    w_gm, w_rest, w_gate, col_scale = _reorder_w_in(w_in)
    ffn1_wo, ffn2_wo = bf(ffn1_wo), bf(ffn2_wo)
    pool_w, cmp_w1_k, cmp_w2_k, cmp_w1_v, cmp_w2_v = bf(pool_w), bf(cmp_w1_k), bf(cmp_w2_k), bf(cmp_w1_v), bf(cmp_w2_v)
    proj_a, proj_b, proj_c, w_out = bf(proj_a), bf(proj_b), bf(proj_c), bf(w_out)
    ffn1_norm, mix_norm, ffn2_norm, pool_scale = row3(ffn1_norm), row3(mix_norm), row3(ffn2_norm), row3(pool_scale)

    final_norm = final_norm.reshape(1, 1, d)
    x = x.reshape(bsz * s, d)
    xn = _row_norm(x, ffn1_norm, 0)
    for layer in range(depth):
        x, xn = _ffn_down(_ffn_up(xn, ffn1_wi, layer), ffn1_wo, x, mix_norm, layer, layer)
        x, xn = _mixing(x, xn, layer, w_gm, w_rest, w_gate, col_scale, pool_w, pool_scale, cmp_pe_k, cmp_w1_k, cmp_w2_k,
                        cmp_pe_v, cmp_w1_v, cmp_w2_v, proj_a, proj_b, proj_c, w_out, ffn2_norm, bsz, s)
        h = _ffn_up(xn, ffn2_wi, layer)
        if layer + 1 < depth:
            x, xn = _ffn_down(h, ffn2_wo, x, ffn1_norm, layer, layer + 1)
        else:
            x = _ffn_down(h, ffn2_wo, x, final_norm, layer, 0, last=True)
    return x.reshape(bsz, s, d)
```

```python
import functools

import numpy as np
import jax
import jax.numpy as jnp
from jax import lax
from jax.experimental import pallas as pl
from jax.experimental.pallas import tpu as pltpu

F32 = jnp.float32
BF16 = jnp.bfloat16

EPS = 1e-6
NEG = -1e30
LOG2E = float(np.log2(np.e))
Q_BLOCK = 128
LANES = 128

A_GROUPS = ((128, 1), (512, 4), (2048, 16))
A_HEADS_PER_GROUP = 4
A_HEAD_DIM = 64
A_GROUP_WIDTH = A_HEADS_PER_GROUP * A_HEAD_DIM
A_WIDTH = A_GROUP_WIDTH * len(A_GROUPS)

B_WINDOWS = (2, 4, 8, 16)
B_GROUP_DIM = 128
B_WIDTH = B_GROUP_DIM * len(B_WINDOWS)

C_KV_GROUPS = 2
C_HEADS_PER_GROUP = 3
C_HEADS = C_KV_GROUPS * C_HEADS_PER_GROUP
C_HEAD_DIM = 128
C_GROUP_WIDTH = C_HEADS_PER_GROUP * C_HEAD_DIM
C_WIDTH = C_HEADS * C_HEAD_DIM
C_KV_WIDTH = C_KV_GROUPS * C_HEAD_DIM
CMP_LEN = 32
CMP_STRIDE = 16
SEL_BLOCK = 64
SEL_TOPN = 8
FORCE_BONUS = 100.0
WIN = 512
N_GATES = 3 * C_HEADS
NSA_TILES = 4
DIL_TILES = (8, 4, 8)

N_BRANCH = 3
D_GATE = 2048

FOFF_GM = 0
FOFF_XB = N_BRANCH * D_GATE
FOFF_KCMP = FOFF_XB + B_WIDTH
FOFF_VCMP = FOFF_KCMP + C_KV_WIDTH
N_F32_COLS = FOFF_VCMP + C_KV_WIDTH
BOFF_QA = 0
BOFF_KA = BOFF_QA + A_WIDTH
BOFF_VA = BOFF_KA + A_WIDTH
BOFF_QC = BOFF_VA + A_WIDTH
BOFF_KSLC = BOFF_QC + C_WIDTH
BOFF_VSLC = BOFF_KSLC + C_KV_WIDTH
BOFF_KWIN = BOFF_VSLC + C_KV_WIDTH
BOFF_VWIN = BOFF_KWIN + C_KV_WIDTH
N_BF16_COLS = BOFF_VWIN + C_KV_WIDTH

VMEM_LIMIT = 56 * 1024 * 1024
ROW_TILE = 1024
COL_TILE = 512
FFN_ROW_TILE = 1024

def _params(*sem):
    return pltpu.CompilerParams(dimension_semantics=sem, vmem_limit_bytes=VMEM_LIMIT)


def _dot(a, b):
    return jnp.dot(a, b, preferred_element_type=F32)


def _dot_nt(a, b):
    return lax.dot_general(a, b, (((1,), (1,)), ((), ())), preferred_element_type=F32)


def _sigmoid(x):
    return 0.5 * jnp.tanh(0.5 * x) + 0.5


def _rms_scaled(x, g):
    return x * lax.rsqrt(jnp.mean(x * x, axis=-1, keepdims=True) + EPS) * g


def _row_norm_kernel(x_ref, g_ref, o_ref):
    o_ref[...] = _rms_scaled(x_ref[...], g_ref[...]).astype(o_ref.dtype)


def _row_norm(x, g, layer, *, tm=512):
    m, d = x.shape
    return pl.pallas_call(
        _row_norm_kernel,
        out_shape=jax.ShapeDtypeStruct((m, d), BF16),
        grid=(m // tm,),
        in_specs=[pl.BlockSpec((tm, d), lambda i: (i, 0)), pl.BlockSpec((None, 1, d), lambda i: (layer, 0, 0))],
        out_specs=pl.BlockSpec((tm, d), lambda i: (i, 0)),
        compiler_params=_params("parallel"),
        name="row_norm",
    )(x, g)


def _ffn_up_kernel(xn_ref, wa_ref, wb_ref, wo_ref, h_ref, wo_bf_ref, w_ref):
    tn = h_ref.shape[-1]

    @pl.when(pl.program_id(1) == 0)
    def _():
        w_ref[:, :tn] = wa_ref[...].astype(BF16)
        w_ref[:, tn:] = wb_ref[...].astype(BF16)

    wo_bf_ref[...] = wo_ref[...].astype(BF16)
    xn = xn_ref[...]
    a = _dot(xn, w_ref[:, :tn])
    b = _dot(xn, w_ref[:, tn:])
    h_ref[...] = (a * _sigmoid(a) * b).astype(BF16)


def _ffn_up(xn, wi, wo, layer, *, tm=FFN_ROW_TILE, tn=COL_TILE):
    m, d = xn.shape
    d_ff = wi.shape[-1] // 2
    nj, ni = d_ff // tn, m // tm
    wo_rows = d_ff // (nj * ni)
    assert wo_rows * nj * ni == d_ff and wo_rows % 16 == 0
    return pl.pallas_call(
        _ffn_up_kernel,
        out_shape=(jax.ShapeDtypeStruct((m, d_ff), BF16), jax.ShapeDtypeStruct((d_ff, d), BF16)),
        grid=(nj, ni),
        in_specs=[
            pl.BlockSpec((tm, d), lambda j, i: (i, 0)),
            pl.BlockSpec((None, d, tn), lambda j, i: (layer, 0, j)),
            pl.BlockSpec((None, d, tn), lambda j, i: (layer, 0, j + nj)),
            pl.BlockSpec((None, wo_rows, d), lambda j, i: (layer, j * ni + i, 0)),
        ],
        out_specs=(pl.BlockSpec((tm, tn), lambda j, i: (i, j)),
                   pl.BlockSpec((wo_rows, d), lambda j, i: (j * ni + i, 0))),
        scratch_shapes=[pltpu.VMEM((d, 2 * tn), BF16)],
        compiler_params=_params("parallel", "arbitrary"),
        name="ffn_up",
    )(xn, wi, wi, wo)


def _ffn_down_kernel(h_ref, w_ref, r_ref, g_ref, *o_refs, last):
    y = r_ref[...] + 0.5 * _dot(h_ref[...], w_ref[...])
    if last:
        o_refs[0][...] = _rms_scaled(y, g_ref[...])
    else:
        o_refs[0][...] = y
        o_refs[1][...] = _rms_scaled(y, g_ref[...]).astype(BF16)


def _ffn_down(h, wo, res, g_next, g_layer, *, last=False, tm=256):
    m, d_ff = h.shape
    d = res.shape[-1]
    rows = pl.BlockSpec((tm, d), lambda i: (i, 0))
    x_out = jax.ShapeDtypeStruct((m, d), F32)
    return pl.pallas_call(
        functools.partial(_ffn_down_kernel, last=last),
        out_shape=x_out if last else (x_out, jax.ShapeDtypeStruct((m, d), BF16)),
        grid=(m // tm,),
        in_specs=[
            pl.BlockSpec((tm, d_ff), lambda i: (i, 0)),
            pl.BlockSpec((d_ff, d), lambda i: (0, 0), pipeline_mode=pl.Buffered(1)),
            rows,
            pl.BlockSpec((None, 1, d), lambda i: (g_layer, 0, 0)),
        ],
        out_specs=rows if last else (rows, rows),
        compiler_params=_params("parallel"),
        name="ffn_down",
    )(h, wo, res, g_next)


def _in_proj_kernel(xn_ref, wgm_ref, wr_ref, wg_ref, sc_ref, of_ref, ob_ref, og_ref, *, n_gm_tiles, n_f32_tiles):
    j = pl.program_id(1)

    @pl.when(j == 0)
    def _():
        og_ref[...] = _dot(xn_ref[...], wg_ref[...])

    @pl.when(j < n_gm_tiles)
    def _():
        of_ref[...] = _dot(xn_ref[...], wgm_ref[...])

    @pl.when((j >= n_gm_tiles) & (j < n_f32_tiles))
    def _():
        of_ref[...] = _dot(xn_ref[...], wr_ref[...])

    @pl.when(j >= n_f32_tiles)
    def _():
        ob_ref[...] = (_dot(xn_ref[...], wr_ref[...]) * sc_ref[...]).astype(BF16)


def _in_proj(xn, w_gm, w_rest, w_gate, col_scale, layer, *, tm=ROW_TILE, tn=2 * COL_TILE):
    m, d = xn.shape
    ng = w_gate.shape[-1]
    ngm = w_gm.shape[-1] // tn
    nf = N_F32_COLS // tn
    nb = N_BF16_COLS // tn
    assert w_gm.shape[-1] + w_rest.shape[-1] == N_F32_COLS + N_BF16_COLS and ngm < nf
    return pl.pallas_call(
        functools.partial(_in_proj_kernel, n_gm_tiles=ngm, n_f32_tiles=nf),
        out_shape=(jax.ShapeDtypeStruct((m, N_F32_COLS), F32), jax.ShapeDtypeStruct((m, N_BF16_COLS), BF16),
                   jax.ShapeDtypeStruct((m, ng), F32)),
        grid=(m // tm, nf + nb),
        in_specs=[
            pl.BlockSpec((tm, d), lambda i, j: (i, 0)),
            pl.BlockSpec((None, d, tn), lambda i, j: (layer, 0, jnp.minimum(j, ngm - 1))),
            pl.BlockSpec((None, d, tn), lambda i, j: (layer, 0, jnp.maximum(j - ngm, 0))),
            pl.BlockSpec((None, d, ng), lambda i, j: (layer, 0, 0)),
            pl.BlockSpec((1, tn), lambda i, j: (0, j)),
        ],
        out_specs=(
            pl.BlockSpec((tm, tn), lambda i, j: (i, jnp.minimum(j, nf - 1))),
            pl.BlockSpec((tm, tn), lambda i, j: (i, jnp.maximum(j - nf, 0))),
            pl.BlockSpec((tm, ng), lambda i, j: (i, 0)),
        ),
        compiler_params=_params("parallel", "arbitrary"),
        name="in_proj",
    )(xn, w_gm, w_rest, w_gate, col_scale)


def _dil_heads(q, kc, vc, kp, vp, first):
    r = lax.broadcasted_iota(jnp.int32, (Q_BLOCK, Q_BLOCK), 0)
    c = lax.broadcasted_iota(jnp.int32, (Q_BLOCK, Q_BLOCK), 1)
    bias_cur = jnp.where(c <= r, 0.0, NEG)
    if kp is not None:
        bias_prev = jnp.where(c >= r + jnp.where(first, Q_BLOCK, 0), 0.0, NEG)
    head = lax.shift_right_logical(lax.broadcasted_iota(jnp.int32, (Q_BLOCK, A_GROUP_WIDTH), 1),
                                   int(np.log2(A_HEAD_DIM)))
    o_acc = jnp.zeros((Q_BLOCK, A_GROUP_WIDTH), F32)
    l_acc = jnp.zeros((Q_BLOCK, A_GROUP_WIDTH), F32)
    for h in range(A_HEADS_PER_GROUP):
        mine = head == h
        qh = q * jnp.where(mine, 1.0, 0.0).astype(BF16)
        s_c = _dot_nt(qh, kc) + bias_cur
        m = jnp.max(s_c, axis=-1, keepdims=True)
        if kp is not None:
            s_p = _dot_nt(qh, kp) + bias_prev
            m = jnp.maximum(m, jnp.max(s_p, axis=-1, keepdims=True))
        p_c = jnp.exp2(s_c - m)
        l = jnp.sum(p_c, axis=-1, keepdims=True)
        pv = _dot(p_c.astype(BF16), vc)
        if kp is not None:
            p_p = jnp.exp2(s_p - m)
            l = l + jnp.sum(p_p, axis=-1, keepdims=True)
            pv = pv + _dot(p_p.astype(BF16), vp)
        o_acc = jnp.where(mine, pv / l, o_acc)
        l_acc = jnp.where(mine, m + jnp.log2(l), l_acc)
    return o_acc, l_acc


def _dil_attn_dense_kernel(q_ref, kc_ref, kp_ref, vc_ref, vp_ref, o_ref, l_ref, *, unroll):
    for u in range(unroll):
        rs = slice(u * Q_BLOCK, (u + 1) * Q_BLOCK)
        ps = slice((u - 1) * Q_BLOCK, u * Q_BLOCK)
        kp, vp = (kp_ref[...], vp_ref[...]) if u == 0 else (kc_ref[ps, :], vc_ref[ps, :])
        first = (pl.program_id(1) == 0) if u == 0 else False
        o, lse = _dil_heads(q_ref[rs, :], kc_ref[rs, :], vc_ref[rs, :], kp, vp, first)
        for half in range(A_GROUP_WIDTH // LANES):
            o_ref[half, rs, :] = o[:, half * LANES:(half + 1) * LANES]
            l_ref[half, rs, :] = lse[:, half * LANES:(half + 1) * LANES]


def _dil_attn_strided_kernel(q_ref, k_ref, v_ref, o_ref, l_ref, qf_ref, kf_ref, vf_ref, *, dil, has_prev, unroll):
    i = pl.program_id(1)
    step = pl.program_id(2)
    halves = A_GROUP_WIDTH // LANES
    slot = lax.rem(i, 2) if has_prev else 0

    @pl.when(step == 0)
    def _():
        for half in range(halves):
            cs = slice(half * LANES, (half + 1) * LANES)
            qf_ref[half] = q_ref[:, cs].astype(F32)
            kf_ref[slot, half] = k_ref[:, cs].astype(F32)
            vf_ref[slot, half] = v_ref[:, cs].astype(F32)

    if has_prev:
        @pl.when((step == 0) & (i == 0))
        def _():
            kf_ref[1] = jnp.zeros(kf_ref.shape[1:], F32)
            vf_ref[1] = jnp.zeros(vf_ref.shape[1:], F32)

    for u in range(unroll):
        r = step * unroll + u

        def rows_of_class(ref, *lead, r=r):
            return jnp.concatenate([ref[(*lead, half, pl.ds(r, Q_BLOCK, stride=dil), slice(None))]
                                    for half in range(halves)], axis=1).astype(BF16)

        kp = rows_of_class(kf_ref, 1 - slot) if has_prev else None
        vp = rows_of_class(vf_ref, 1 - slot) if has_prev else None
        o, lse = _dil_heads(rows_of_class(qf_ref), rows_of_class(kf_ref, slot), rows_of_class(vf_ref, slot),
                            kp, vp, i == 0)
        for half in range(halves):
            o_ref[half, pl.ds(r, Q_BLOCK, stride=dil), :] = o[:, half * LANES:(half + 1) * LANES]
            l_ref[half, pl.ds(r, Q_BLOCK, stride=dil), :] = lse[:, half * LANES:(half + 1) * LANES]


def _dil_attn(cb3, gi):
    bsz, s, _ = cb3.shape
    dil = A_GROUPS[gi][1]
    w = A_GROUP_WIDTH
    halves = w // LANES
    out = jax.ShapeDtypeStruct((bsz, halves, s, LANES), F32)
    name = f"dilated_attention_g{gi}"
    unroll = DIL_TILES[gi]
    if dil == 1:
        step_rows = Q_BLOCK * unroll

        def col(off, prev):
            if prev:
                return pl.BlockSpec((None, Q_BLOCK, w),
                                    lambda b, i: (b, jnp.maximum(i * unroll - 1, 0), (off + gi * w) // w))
            return pl.BlockSpec((None, step_rows, w), lambda b, i: (b, i, (off + gi * w) // w))

        ospec = pl.BlockSpec((None, halves, step_rows, LANES), lambda b, i: (b, 0, i, 0))
        return pl.pallas_call(
            functools.partial(_dil_attn_dense_kernel, unroll=unroll),
            out_shape=(out, out),
            grid=(bsz, s // step_rows),
            in_specs=[col(BOFF_QA, False), col(BOFF_KA, False), col(BOFF_KA, True), col(BOFF_VA, False), col(BOFF_VA, True)],
            out_specs=(ospec, ospec),
            compiler_params=_params("parallel", "arbitrary"),
            name=name,
        )(cb3, cb3, cb3, cb3, cb3)

    rows = Q_BLOCK * dil
    tiles = s // rows
    has_prev = tiles > 1

    def col(off):
        return pl.BlockSpec((None, rows, w), lambda b, i, r: (b, i, (off + gi * w) // w))

    ospec = pl.BlockSpec((None, halves, rows, LANES), lambda b, i, r: (b, 0, i, 0))
    return pl.pallas_call(
        functools.partial(_dil_attn_strided_kernel, dil=dil, has_prev=has_prev, unroll=unroll),
        out_shape=(out, out),
        grid=(bsz, tiles, dil // unroll),
        in_specs=[col(BOFF_QA), col(BOFF_KA), col(BOFF_VA)],
        out_specs=(ospec, ospec),
        scratch_shapes=[pltpu.VMEM((halves, rows, LANES), F32),
                        pltpu.VMEM((2 if has_prev else 1, halves, rows, LANES), F32),
                        pltpu.VMEM((2 if has_prev else 1, halves, rows, LANES), F32)],
        compiler_params=_params("parallel", "arbitrary", "arbitrary"),
        name=name,
    )(cb3, cb3, cb3)


def _pool_kernel(x_ref, w_ref, sc_ref, o_ref):
    s = x_ref.shape[0]
    t = lax.broadcasted_iota(jnp.int32, (s, B_GROUP_DIM), 0)
    for gi, win in enumerate(B_WINDOWS):
        cs = slice(gi * B_GROUP_DIM, (gi + 1) * B_GROUP_DIM)
        x = x_ref[:, cs]
        acc = x
        k = 1
        while k < win:
            acc = acc + jnp.where(t >= k, pltpu.roll(acc, k, axis=0), 0.0)
            k *= 2
        cnt = jnp.minimum(t + 1, win).astype(F32)
        z = acc / cnt - x
        o_ref[:, cs] = _dot(z.astype(BF16), w_ref[gi]) * sc_ref[:, cs]


def _pool_mixer(cf3, pool_w, pool_scale, layer):
    bsz, s, _ = cf3.shape
    ng = len(B_WINDOWS)
    return pl.pallas_call(
        _pool_kernel,
        out_shape=jax.ShapeDtypeStruct((bsz, s, B_WIDTH), F32),
        grid=(bsz,),
        in_specs=[
            pl.BlockSpec((None, s, B_WIDTH), lambda b: (b, 0, FOFF_XB // B_WIDTH)),
            pl.BlockSpec((None, ng, B_GROUP_DIM, B_GROUP_DIM), lambda b: (layer, 0, 0, 0)),
            pl.BlockSpec((None, 1, B_WIDTH), lambda b: (layer, 0, 0)),
        ],
        out_specs=pl.BlockSpec((None, s, B_WIDTH), lambda b: (b, 0, 0)),
        compiler_params=_params("parallel"),
        name="pool_mixer",
    )(cf3, pool_w, pool_scale)


def _gelu_tanh(x):
    return 0.5 * x * (1.0 + jnp.tanh(np.float32(np.sqrt(2.0 / np.pi)) * (x + 0.044715 * (x * x * x))))


def _compress_one(z_ref, pe_ref, w1_ref, w2_ref, o_ref):
    nch = z_ref.shape[0] // CMP_STRIDE
    dh = C_HEAD_DIM
    first = jnp.zeros((nch, dh), F32)
    second = jnp.zeros((nch, dh), F32)
    for p in range(CMP_STRIDE):
        zp = z_ref[pl.ds(p, nch, stride=CMP_STRIDE), :]
        a = (zp + pe_ref[p:p + 1, :]).astype(BF16)
        b = (zp + pe_ref[CMP_STRIDE + p:CMP_STRIDE + p + 1, :]).astype(BF16)
        first = first + _dot(a, w1_ref[p * dh:(p + 1) * dh, :])
        second = second + _dot(b, w1_ref[(CMP_STRIDE + p) * dh:(CMP_STRIDE + p + 1) * dh, :])
    pre = first + pltpu.roll(second, nch - 1, axis=0)
    o_ref[...] = _dot(_gelu_tanh(pre).astype(BF16), w2_ref[...]).astype(BF16)


def _compress_kernel(zk_ref, zv_ref, pek_ref, w1k_ref, w2k_ref, pev_ref, w1v_ref, w2v_ref, kc_ref, vc_ref):
    _compress_one(zk_ref, pek_ref, w1k_ref, w2k_ref, kc_ref)
    _compress_one(zv_ref, pev_ref, w1v_ref, w2v_ref, vc_ref)


def _compress(cf3, pe_k, w1_k, w2_k, pe_v, w1_v, w2_v, layer):
    bsz, s, _ = cf3.shape
    nch = s // CMP_STRIDE
    dh = C_HEAD_DIM
    cdim = CMP_LEN * dh

    def col(off):
        return pl.BlockSpec((None, s, dh), lambda b, g: (b, 0, off // dh + g))

    def wspec(shape):
        return pl.BlockSpec((None,) + shape, lambda b, g: (layer,) + (0,) * len(shape))

    out = jax.ShapeDtypeStruct((bsz, C_KV_GROUPS, nch, dh), BF16)
    ospec = pl.BlockSpec((None, None, nch, dh), lambda b, g: (b, g, 0, 0))
    return pl.pallas_call(
        _compress_kernel,
        out_shape=(out, out),
        grid=(bsz, C_KV_GROUPS),
        in_specs=[col(FOFF_KCMP), col(FOFF_VCMP),
                  wspec((CMP_LEN, dh)), wspec((cdim, dh)), wspec((dh, dh)),
                  wspec((CMP_LEN, dh)), wspec((cdim, dh)), wspec((dh, dh))],
        out_specs=(ospec, ospec),
        compiler_params=_params("parallel", "parallel"),
        name="nsa_compress",
    )(cf3, cf3, pe_k, w1_k, w2_k, pe_v, w1_v, w2_v)


def _split3(x):
    hi = x.astype(BF16)
    r1 = x - hi.astype(F32)
    mid = r1.astype(BF16)
    lo = (r1 - mid.astype(F32)).astype(BF16)
    return hi, mid, lo


def _softmax_pv(s, v):
    m = jnp.max(s, axis=-1, keepdims=True)
    p = jnp.exp2(s - m)
    return _dot(p.astype(BF16), v) / jnp.sum(p, axis=-1, keepdims=True)


def _nsa_tile(t0, q, gates, kc_ref, vc_ref, kw_ref, vw_ref, ovt_ref, nb):
    hg = C_HEADS_PER_GROUP
    dh = C_HEAD_DIM
    row = t0 + lax.broadcasted_iota(jnp.int32, (Q_BLOCK, LANES), 0)
    lane = lax.broadcasted_iota(jnp.int32, (Q_BLOCK, LANES), 1)

    q3 = jnp.concatenate([q[:, h * dh:(h + 1) * dh] for h in range(hg)], axis=0)
    rows = [slice(h * Q_BLOCK, (h + 1) * Q_BLOCK) for h in range(hg)]
    gt = _sigmoid(gates)

    cvalid = lane * CMP_STRIDE + (CMP_LEN - 1) <= row
    s3 = _dot_nt(q3, kc_ref[...])
    vc = vc_ref[...]
    out = []
    psum = jnp.zeros((Q_BLOCK, LANES), F32)
    for h in range(hg):
        s = jnp.where(cvalid, s3[rows[h]], NEG)
        p = jnp.exp2(s - jnp.max(s, axis=-1, keepdims=True))
        p = jnp.where(cvalid, p / jnp.sum(p, axis=-1, keepdims=True), 0.0)
        out.append(gt[:, 3 * h:3 * h + 1] * _dot(p.astype(BF16), vc))
        psum = psum + p

    ovt = ovt_ref[...]
    imp_t = sum(_dot_nt(ovt, part) for part in _split3(psum))
    blk = lax.broadcasted_iota(jnp.int32, (nb, Q_BLOCK), 0)
    tb = lax.shift_right_logical(t0 + lax.broadcasted_iota(jnp.int32, (nb, Q_BLOCK), 1), int(np.log2(SEL_BLOCK)))
    forced = (blk == 0) | (blk == tb) | (blk == tb - 1)
    score = jnp.where(blk > tb, -1.0, imp_t + jnp.where(forced, FORCE_BONUS, 0.0))
    blk_f = blk.astype(F32)
    sel_t = jnp.zeros((nb, Q_BLOCK), F32)
    for _ in range(min(SEL_TOPN, nb)):
        mx = jnp.max(score, axis=0, keepdims=True)
        idx = jnp.min(jnp.where(score == mx, blk_f, float(nb)), axis=0, keepdims=True)
        hit = blk_f == idx
        sel_t = jnp.where(hit, 1.0, sel_t)
        score = jnp.where(hit, -3.0, score)
    sel = jnp.concatenate([sel_t, jnp.zeros((LANES - nb, Q_BLOCK), F32)], axis=0).T
    sel = jnp.where(sel > 0.5, 0.0, NEG).astype(BF16)

    wlen = WIN + Q_BLOCK
    w0 = pl.multiple_of(jnp.maximum(t0 - WIN, 0), Q_BLOCK)
    kpos = w0 + lax.broadcasted_iota(jnp.int32, (Q_BLOCK, wlen), 1)
    qpos = t0 + lax.broadcasted_iota(jnp.int32, (Q_BLOCK, wlen), 0)
    bias = jnp.where((kpos <= qpos) & (kpos > qpos - WIN), 0.0, NEG)
    s3 = _dot_nt(q3, kw_ref[pl.ds(w0, wlen), :])
    vw = vw_ref[pl.ds(w0, wlen), :]
    for h in range(hg):
        out[h] = out[h] + gt[:, 3 * h + 2:3 * h + 3] * _softmax_pv(s3[rows[h]] + bias, vw)
    return q3, gt, out, sel


def _nsa_kernel(q_ref, kc_ref, vc_ref, ks_ref, vs_ref, kw_ref, vw_ref, gt_ref, ovt_ref, ex_ref, o_ref, *, s_len):
    n = pl.program_id(2)
    hg = C_HEADS_PER_GROUP
    dh = C_HEAD_DIM
    tiles = []
    for u in range(NSA_TILES):
        rs = slice(u * Q_BLOCK, (u + 1) * Q_BLOCK)
        t0 = (n * NSA_TILES + u) * Q_BLOCK
        tiles.append((rs, t0) + _nsa_tile(t0, q_ref[rs, :], gt_ref[rs, :], kc_ref, vc_ref, kw_ref, vw_ref, ovt_ref,
                                          s_len // SEL_BLOCK))

    r = lax.broadcasted_iota(jnp.int32, (Q_BLOCK, Q_BLOCK), 0)
    c = lax.broadcasted_iota(jnp.int32, (Q_BLOCK, Q_BLOCK), 1)
    causal = jnp.where(c <= r, 0.0, NEG)
    for step in range(s_len // (Q_BLOCK * NSA_TILES)):

        @pl.when(n == step)
        def _(step=step):
            for u, (rs, _, q3, gt, out, sel) in enumerate(tiles):
                klen = (step * NSA_TILES + u + 1) * Q_BLOCK
                blocked = _dot(sel, ex_ref[:, :klen])
                bias = blocked[:, klen - Q_BLOCK:] + causal
                if klen > Q_BLOCK:
                    bias = jnp.concatenate([blocked[:, :klen - Q_BLOCK], bias], axis=1)
                s3 = _dot_nt(q3, ks_ref[:klen, :])
                vs = vs_ref[:klen, :]
                for h in range(hg):
                    hs = slice(h * Q_BLOCK, (h + 1) * Q_BLOCK)
                    o_ref[rs, h * dh:(h + 1) * dh] = out[h] + gt[:, 3 * h + 1:3 * h + 2] * _softmax_pv(s3[hs] + bias, vs)


def _nsa_constants(s):
    nch = s // CMP_STRIDE
    nb = s // SEL_BLOCK
    n_cmp = (s - CMP_LEN) // CMP_STRIDE + 1
    ci = np.arange(nch)[None, :] * CMP_STRIDE
    bj = np.arange(nb)[:, None] * SEL_BLOCK
    overlap_t = (ci < bj + SEL_BLOCK) & (ci + CMP_LEN > bj) & (np.arange(nch)[None, :] < n_cmp)
    expand = np.arange(LANES)[:, None] == (np.arange(s)[None, :] // SEL_BLOCK)
    return jnp.asarray(overlap_t, BF16), jnp.asarray(expand, BF16)


def _nsa(cb3, kc, vc, gates3):
    bsz, s, _ = cb3.shape
    nq = s // Q_BLOCK
    nch = s // CMP_STRIDE
    nb = s // SEL_BLOCK
    dh = C_HEAD_DIM
    step_rows = Q_BLOCK * NSA_TILES
    assert nch == LANES and nb <= LANES and nb % 8 == 0 and s % step_rows == 0 and s >= WIN + Q_BLOCK
    overlap_t, expand = _nsa_constants(s)

    def kv(off):
        return pl.BlockSpec((None, s, dh), lambda b, g, n: (b, 0, off // dh + g))

    cmp_spec = pl.BlockSpec((None, None, nch, dh), lambda b, g, n: (b, g, 0, 0))
    return pl.pallas_call(
        functools.partial(_nsa_kernel, s_len=s),
        out_shape=jax.ShapeDtypeStruct((bsz, s, C_WIDTH), F32),
        grid=(bsz, C_KV_GROUPS, s // step_rows),
        in_specs=[
            pl.BlockSpec((None, step_rows, C_GROUP_WIDTH), lambda b, g, n: (b, n, BOFF_QC // C_GROUP_WIDTH + g)),
            cmp_spec, cmp_spec,
            kv(BOFF_KSLC), kv(BOFF_VSLC), kv(BOFF_KWIN), kv(BOFF_VWIN),
            pl.BlockSpec((None, step_rows, LANES), lambda b, g, n: (b, n, g)),
            pl.BlockSpec((nb, nch), lambda b, g, n: (0, 0)),
            pl.BlockSpec((LANES, s), lambda b, g, n: (0, 0)),
        ],
        out_specs=pl.BlockSpec((None, step_rows, C_GROUP_WIDTH), lambda b, g, n: (b, n, g)),
        compiler_params=_params("parallel", "parallel", "arbitrary"),
        name="nsa_attention",
    )(cb3, kc, vc, cb3, cb3, cb3, cb3, gates3, overlap_t, expand)


def _merge_kernel(*refs):
    ng = len(A_GROUPS)
    oa_refs, la_refs = refs[:ng], refs[ng:2 * ng]
    zb_ref, oc_ref, g0_ref, g1_ref, g2_ref, x_ref, pa_ref, pb_ref, pc_ref, wo_ref, gn_ref, o_ref, xn_ref = refs[2 * ng:]

    def lanes(ref):
        return jnp.concatenate([ref[half] for half in range(ref.shape[0])], axis=1)

    lse = [lanes(r) for r in la_refs]
    m = functools.reduce(jnp.maximum, lse)
    e = [jnp.exp2(l - m) for l in lse]
    oa = sum(w * lanes(r) for w, r in zip(e, oa_refs)) / sum(e)
    ya = _dot(oa.astype(BF16), pa_ref[...])
    yb = _dot(zb_ref[...].astype(BF16), pb_ref[...])
    yc = _dot(oc_ref[...].astype(BF16), pc_ref[...])
    mix = _sigmoid(g0_ref[...]) * ya + _sigmoid(g1_ref[...]) * yb + _sigmoid(g2_ref[...]) * yc
    y = x_ref[...] + _dot(mix.astype(BF16), wo_ref[...])
    o_ref[...] = y
    xn_ref[...] = _rms_scaled(y, gn_ref[...]).astype(BF16)


def _merge(oa, la, zb, oc, cf, x, proj_a, proj_b, proj_c, w_out, g_next, layer, *, tm=256):
    m, d = x.shape
    assert d == D_GATE

    def rows(w):
        return pl.BlockSpec((tm, w), lambda i: (i, 0))

    def gate(k):
        return pl.BlockSpec((tm, d), lambda i: (i, FOFF_GM // d + k))

    def weight(k):
        return pl.BlockSpec((None, k, d), lambda i: (layer, 0, 0), pipeline_mode=pl.Buffered(1))

    _, halves, s, _ = oa[0].shape
    per_seq = s // tm
    group = pl.BlockSpec((None, halves, tm, LANES), lambda i: (i // per_seq, 0, i % per_seq, 0))

    return pl.pallas_call(
        _merge_kernel,
        out_shape=(jax.ShapeDtypeStruct((m, d), F32), jax.ShapeDtypeStruct((m, d), BF16)),
        grid=(m // tm,),
        in_specs=[group] * (2 * len(A_GROUPS))
        + [rows(B_WIDTH), rows(C_WIDTH), gate(0), gate(1), gate(2), rows(d),
           weight(A_GROUP_WIDTH), weight(B_WIDTH), weight(C_WIDTH), weight(d),
           pl.BlockSpec((None, 1, d), lambda i: (layer, 0, 0))],
        out_specs=(rows(d), rows(d)),
        compiler_params=_params("parallel"),
        name="merge",
    )(*oa, *la, zb, oc, cf, cf, cf, x, proj_a, proj_b, proj_c, w_out, g_next)


_A3 = 3 * A_WIDTH
_XB_END = _A3 + B_WIDTH
_QC_END = _XB_END + C_WIDTH
_CMP_END = _QC_END + 2 * C_KV_WIDTH
_KV_END = _QC_END + 6 * C_KV_WIDTH
_GM_START = _KV_END + N_GATES
_REST_RANGES = ((_A3, _XB_END), (_QC_END, _CMP_END), (0, _A3), (_XB_END, _QC_END), (_CMP_END, _KV_END))


def _reorder_w_in(w_in):
    w_gm = w_in[..., _GM_START:].astype(BF16)
    w_rest = jnp.concatenate([w_in[..., lo:hi] for lo, hi in _REST_RANGES], axis=-1).astype(BF16)
    per_group = 3 * C_HEADS_PER_GROUP
    pad = jnp.zeros(w_in.shape[:-1] + (LANES - per_group,), w_in.dtype)
    w_gate = jnp.concatenate(
        [piece for g in range(C_KV_GROUPS)
         for piece in (w_in[..., _KV_END + g * per_group:_KV_END + (g + 1) * per_group], pad)], axis=-1).astype(BF16)
    col_scale = np.ones((1, N_F32_COLS + N_BF16_COLS), np.float32)
    col_scale[:, N_F32_COLS + BOFF_QA:N_F32_COLS + BOFF_QA + A_WIDTH] = A_HEAD_DIM ** -0.5 * LOG2E
    col_scale[:, N_F32_COLS + BOFF_QC:N_F32_COLS + BOFF_QC + C_WIDTH] = C_HEAD_DIM ** -0.5 * LOG2E
    return w_gm, w_rest, w_gate, jnp.asarray(col_scale)


def _mixing(x, xn, layer, w_gm, w_rest, w_gate, col_scale, pool_w, pool_scale, pe_k, w1_k, w2_k, pe_v, w1_v, w2_v,
            proj_a, proj_b, proj_c, w_out, g_next, bsz, s):
    m, d = x.shape
    cf, cb, gates = _in_proj(xn, w_gm, w_rest, w_gate, col_scale, layer)
    cf3 = cf.reshape(bsz, s, N_F32_COLS)
    cb3 = cb.reshape(bsz, s, N_BF16_COLS)
    gates3 = gates.reshape(bsz, s, C_KV_GROUPS * LANES)

    oa, la = zip(*[_dil_attn(cb3, gi) for gi in range(len(A_GROUPS))])
    zb = _pool_mixer(cf3, pool_w, pool_scale, layer).reshape(m, B_WIDTH)
    kc, vc = _compress(cf3, pe_k, w1_k, w2_k, pe_v, w1_v, w2_v, layer)
    oc = _nsa(cb3, kc, vc, gates3).reshape(m, C_WIDTH)
    return _merge(oa, la, zb, oc, cf, x, proj_a, proj_b, proj_c, w_out, g_next, layer)


def kernel(x, ffn1_norm, ffn1_wi, ffn1_wo, mix_norm, w_in, pool_w, pool_scale, cmp_pe_k, cmp_w1_k, cmp_w2_k,
           cmp_pe_v, cmp_w1_v, cmp_w2_v, proj_a, proj_b, proj_c, w_out, ffn2_norm, ffn2_wi, ffn2_wo, final_norm):
    bsz, s, d = x.shape
    depth = ffn1_wi.shape[0]
    for win, dil in A_GROUPS:
        assert win // dil == Q_BLOCK and s % (dil * Q_BLOCK) == 0
    assert CMP_LEN == 2 * CMP_STRIDE and all(w & (w - 1) == 0 for w in B_WINDOWS)

    bf = lambda w: w.astype(BF16)
    row3 = lambda g: g.reshape(g.shape[0], 1, g.shape[-1])
    w_gm, w_rest, w_gate, col_scale = _reorder_w_in(w_in)
    pool_w, cmp_w1_k, cmp_w2_k, cmp_w1_v, cmp_w2_v = bf(pool_w), bf(cmp_w1_k), bf(cmp_w2_k), bf(cmp_w1_v), bf(cmp_w2_v)
    proj_a, proj_b, proj_c, w_out = bf(proj_a), bf(proj_b), bf(proj_c), bf(w_out)
    ffn1_norm, mix_norm, ffn2_norm, pool_scale = row3(ffn1_norm), row3(mix_norm), row3(ffn2_norm), row3(pool_scale)

    final_norm = final_norm.reshape(1, 1, d)
    x = x.reshape(bsz * s, d)
    xn = _row_norm(x, ffn1_norm, 0)
    for layer in range(depth):
        h, wo = _ffn_up(xn, ffn1_wi, ffn1_wo, layer)
        x, xn = _ffn_down(h, wo, x, mix_norm, layer)
        x, xn = _mixing(x, xn, layer, w_gm, w_rest, w_gate, col_scale, pool_w, pool_scale, cmp_pe_k, cmp_w1_k, cmp_w2_k,
                        cmp_pe_v, cmp_w1_v, cmp_w2_v, proj_a, proj_b, proj_c, w_out, ffn2_norm, bsz, s)
        h, wo = _ffn_up(xn, ffn2_wi, ffn2_wo, layer)
        if layer + 1 < depth:
            x, xn = _ffn_down(h, wo, x, ffn1_norm, layer + 1)
        else:
            x = _ffn_down(h, wo, x, final_norm, 0, last=True)
    return x.reshape(bsz, s, d)
```

```python
import functools

import numpy as np
import jax
import jax.numpy as jnp
from jax import lax
from jax.experimental import pallas as pl
from jax.experimental.pallas import tpu as pltpu

F32 = jnp.float32
BF16 = jnp.bfloat16

EPS = 1e-6
NEG = -1e30
LOG2E = float(np.log2(np.e))
Q_BLOCK = 128
LANES = 128

A_GROUPS = ((128, 1), (512, 4), (2048, 16))
A_HEADS_PER_GROUP = 4
A_HEAD_DIM = 64
A_GROUP_WIDTH = A_HEADS_PER_GROUP * A_HEAD_DIM
A_WIDTH = A_GROUP_WIDTH * len(A_GROUPS)

B_WINDOWS = (2, 4, 8, 16)
B_GROUP_DIM = 128
B_WIDTH = B_GROUP_DIM * len(B_WINDOWS)

C_KV_GROUPS = 2
C_HEADS_PER_GROUP = 3
C_HEADS = C_KV_GROUPS * C_HEADS_PER_GROUP
C_HEAD_DIM = 128
C_GROUP_WIDTH = C_HEADS_PER_GROUP * C_HEAD_DIM
C_WIDTH = C_HEADS * C_HEAD_DIM
C_KV_WIDTH = C_KV_GROUPS * C_HEAD_DIM
CMP_LEN = 32
CMP_STRIDE = 16
SEL_BLOCK = 64
SEL_TOPN = 8
FORCE_BONUS = 100.0
WIN = 512
N_GATES = 3 * C_HEADS
NSA_TILES = 4
DIL_TILES = (8, 4, 8)

N_BRANCH = 3
D_GATE = 2048

FOFF_GM = 0
FOFF_XB = N_BRANCH * D_GATE
FOFF_KCMP = FOFF_XB + B_WIDTH
FOFF_VCMP = FOFF_KCMP + C_KV_WIDTH
N_F32_COLS = FOFF_VCMP + C_KV_WIDTH
BOFF_QA = 0
BOFF_KA = BOFF_QA + A_WIDTH
BOFF_VA = BOFF_KA + A_WIDTH
BOFF_QC = BOFF_VA + A_WIDTH
BOFF_KSLC = BOFF_QC + C_WIDTH
BOFF_VSLC = BOFF_KSLC + C_KV_WIDTH
BOFF_KWIN = BOFF_VSLC + C_KV_WIDTH
BOFF_VWIN = BOFF_KWIN + C_KV_WIDTH
N_BF16_COLS = BOFF_VWIN + C_KV_WIDTH

VMEM_LIMIT = 56 * 1024 * 1024
ROW_TILE = 1024
COL_TILE = 512
FFN_ROW_TILE = 1024

def _params(*sem):
    return pltpu.CompilerParams(dimension_semantics=sem, vmem_limit_bytes=VMEM_LIMIT)


def _dot(a, b):
    return jnp.dot(a, b, preferred_element_type=F32)


def _dot_nt(a, b):
    return lax.dot_general(a, b, (((1,), (1,)), ((), ())), preferred_element_type=F32)


def _sigmoid(x):
    return 0.5 * jnp.tanh(0.5 * x) + 0.5


def _rms_scaled(x, g):
    return x * lax.rsqrt(jnp.mean(x * x, axis=-1, keepdims=True) + EPS) * g


def _row_norm_kernel(x_ref, g_ref, o_ref):
    o_ref[...] = _rms_scaled(x_ref[...], g_ref[...]).astype(o_ref.dtype)


def _row_norm(x, g, layer, *, tm=512):
    m, d = x.shape
    return pl.pallas_call(
        _row_norm_kernel,
        out_shape=jax.ShapeDtypeStruct((m, d), BF16),
        grid=(m // tm,),
        in_specs=[pl.BlockSpec((tm, d), lambda i: (i, 0)), pl.BlockSpec((None, 1, d), lambda i: (layer, 0, 0))],
        out_specs=pl.BlockSpec((tm, d), lambda i: (i, 0)),
        compiler_params=_params("parallel"),
        name="row_norm",
    )(x, g)


def _ffn_up_kernel(xn_ref, wa_ref, wb_ref, wo_ref, h_ref, wo_bf_ref, w_ref):
    tn = h_ref.shape[-1]

    @pl.when(pl.program_id(1) == 0)
    def _():
        w_ref[:, :tn] = wa_ref[...].astype(BF16)
        w_ref[:, tn:] = wb_ref[...].astype(BF16)

    wo_bf_ref[...] = wo_ref[...].astype(BF16)
    xn = xn_ref[...]
    a = _dot(xn, w_ref[:, :tn])
    b = _dot(xn, w_ref[:, tn:])
    h_ref[...] = (a * _sigmoid(a) * b).astype(BF16)


def _ffn_up(xn, wi, wo, layer, *, tm=FFN_ROW_TILE, tn=COL_TILE):
    m, d = xn.shape
    d_ff = wi.shape[-1] // 2
    nj, ni = d_ff // tn, m // tm
    wo_rows = d_ff // (nj * ni)
    assert wo_rows * nj * ni == d_ff and wo_rows % 16 == 0
    return pl.pallas_call(
        _ffn_up_kernel,
        out_shape=(jax.ShapeDtypeStruct((m, d_ff), BF16), jax.ShapeDtypeStruct((d_ff, d), BF16)),
        grid=(nj, ni),
        in_specs=[
            pl.BlockSpec((tm, d), lambda j, i: (i, 0)),
            pl.BlockSpec((None, d, tn), lambda j, i: (layer, 0, j)),
            pl.BlockSpec((None, d, tn), lambda j, i: (layer, 0, j + nj)),
            pl.BlockSpec((None, wo_rows, d), lambda j, i: (layer, j * ni + i, 0)),
        ],
        out_specs=(pl.BlockSpec((tm, tn), lambda j, i: (i, j)),
                   pl.BlockSpec((wo_rows, d), lambda j, i: (j * ni + i, 0))),
        scratch_shapes=[pltpu.VMEM((d, 2 * tn), BF16)],
        compiler_params=_params("parallel", "arbitrary"),
        name="ffn_up",
    )(xn, wi, wi, wo)


def _ffn_down_kernel(h_ref, w_ref, r_ref, g_ref, *o_refs, last):
    y = r_ref[...] + 0.5 * _dot(h_ref[...], w_ref[...])
    if last:
        o_refs[0][...] = _rms_scaled(y, g_ref[...])
    else:
        o_refs[0][...] = y
        o_refs[1][...] = _rms_scaled(y, g_ref[...]).astype(BF16)


def _ffn_down(h, wo, res, g_next, g_layer, *, last=False, tm=256):
    m, d_ff = h.shape
    d = res.shape[-1]
    rows = pl.BlockSpec((tm, d), lambda i: (i, 0))
    x_out = jax.ShapeDtypeStruct((m, d), F32)
    return pl.pallas_call(
        functools.partial(_ffn_down_kernel, last=last),
        out_shape=x_out if last else (x_out, jax.ShapeDtypeStruct((m, d), BF16)),
        grid=(m // tm,),
        in_specs=[
            pl.BlockSpec((tm, d_ff), lambda i: (i, 0)),
            pl.BlockSpec((d_ff, d), lambda i: (0, 0), pipeline_mode=pl.Buffered(1)),
            rows,
            pl.BlockSpec((None, 1, d), lambda i: (g_layer, 0, 0)),
        ],
        out_specs=rows if last else (rows, rows),
        compiler_params=_params("parallel"),
        name="ffn_down",
    )(h, wo, res, g_next)


def _in_proj_kernel(xn_ref, wgm_ref, wr_ref, wg_ref, sc_ref, *refs, n_gm_tiles, n_f32_tiles):
    n_cast = (len(refs) - 3) // 2
    cast_in, (of_ref, ob_ref, og_ref), cast_out = refs[:n_cast], refs[n_cast:n_cast + 3], refs[n_cast + 3:]
    j = pl.program_id(1)

    @pl.when(j == 0)
    def _():
        og_ref[...] = _dot(xn_ref[...], wg_ref[...])
        for src, dst in zip(cast_in, cast_out):
            dst[...] = src[...].astype(BF16)

    @pl.when(j < n_gm_tiles)
    def _():
        of_ref[...] = _dot(xn_ref[...], wgm_ref[...])

    @pl.when((j >= n_gm_tiles) & (j < n_f32_tiles))
    def _():
        of_ref[...] = _dot(xn_ref[...], wr_ref[...])

    @pl.when(j >= n_f32_tiles)
    def _():
        ob_ref[...] = (_dot(xn_ref[...], wr_ref[...]) * sc_ref[...]).astype(BF16)


def _in_proj(xn, w_gm, w_rest, w_gate, col_scale, merge_weights, layer, *, tm=ROW_TILE, tn=2 * COL_TILE):
    m, d = xn.shape
    ng = w_gate.shape[-1]
    ngm = w_gm.shape[-1] // tn
    nf = N_F32_COLS // tn
    nb = N_BF16_COLS // tn
    ni = m // tm
    assert w_gm.shape[-1] + w_rest.shape[-1] == N_F32_COLS + N_BF16_COLS and ngm < nf
    assert all(w.shape[1] % (16 * ni) == 0 for w in merge_weights)
    outs = pl.pallas_call(
        functools.partial(_in_proj_kernel, n_gm_tiles=ngm, n_f32_tiles=nf),
        out_shape=(jax.ShapeDtypeStruct((m, N_F32_COLS), F32), jax.ShapeDtypeStruct((m, N_BF16_COLS), BF16),
                   jax.ShapeDtypeStruct((m, ng), F32))
        + tuple(jax.ShapeDtypeStruct(w.shape[1:], BF16) for w in merge_weights),
        grid=(ni, nf + nb),
        in_specs=[
            pl.BlockSpec((tm, d), lambda i, j: (i, 0)),
            pl.BlockSpec((None, d, tn), lambda i, j: (layer, 0, jnp.minimum(j, ngm - 1))),
            pl.BlockSpec((None, d, tn), lambda i, j: (layer, 0, jnp.maximum(j - ngm, 0))),
            pl.BlockSpec((None, d, ng), lambda i, j: (layer, 0, 0)),
            pl.BlockSpec((1, tn), lambda i, j: (0, j)),
        ] + [pl.BlockSpec((None, w.shape[1] // ni, w.shape[2]), lambda i, j: (layer, i, 0)) for w in merge_weights],
        out_specs=(
            pl.BlockSpec((tm, tn), lambda i, j: (i, jnp.minimum(j, nf - 1))),
            pl.BlockSpec((tm, tn), lambda i, j: (i, jnp.maximum(j - nf, 0))),
            pl.BlockSpec((tm, ng), lambda i, j: (i, 0)),
        ) + tuple(pl.BlockSpec((w.shape[1] // ni, w.shape[2]), lambda i, j: (i, 0)) for w in merge_weights),
        compiler_params=_params("parallel", "arbitrary"),
        name="in_proj",
    )(xn, w_gm, w_rest, w_gate, col_scale, *merge_weights)
    return outs[0], outs[1], outs[2], outs[3:]


def _dil_heads(q, kc, vc, kp, vp, first):
    r = lax.broadcasted_iota(jnp.int32, (Q_BLOCK, Q_BLOCK), 0)
    c = lax.broadcasted_iota(jnp.int32, (Q_BLOCK, Q_BLOCK), 1)
    bias_cur = jnp.where(c <= r, 0.0, NEG)
    if kp is not None:
        bias_prev = jnp.where(c >= r + jnp.where(first, Q_BLOCK, 0), 0.0, NEG)
    head = lax.shift_right_logical(lax.broadcasted_iota(jnp.int32, (Q_BLOCK, A_GROUP_WIDTH), 1),
                                   int(np.log2(A_HEAD_DIM)))
    o_acc = jnp.zeros((Q_BLOCK, A_GROUP_WIDTH), F32)
    l_acc = jnp.zeros((Q_BLOCK, A_GROUP_WIDTH), F32)
    for h in range(A_HEADS_PER_GROUP):
        mine = head == h
        qh = q * jnp.where(mine, 1.0, 0.0).astype(BF16)
        s_c = _dot_nt(qh, kc) + bias_cur
        m = jnp.max(s_c, axis=-1, keepdims=True)
        if kp is not None:
            s_p = _dot_nt(qh, kp) + bias_prev
            m = jnp.maximum(m, jnp.max(s_p, axis=-1, keepdims=True))
        p_c = jnp.exp2(s_c - m)
        l = jnp.sum(p_c, axis=-1, keepdims=True)
        pv = _dot(p_c.astype(BF16), vc)
        if kp is not None:
            p_p = jnp.exp2(s_p - m)
            l = l + jnp.sum(p_p, axis=-1, keepdims=True)
            pv = pv + _dot(p_p.astype(BF16), vp)
        o_acc = jnp.where(mine, pv / l, o_acc)
        l_acc = jnp.where(mine, m + jnp.log2(l), l_acc)
    return o_acc, l_acc


def _dil_attn_dense_kernel(q_ref, kc_ref, kp_ref, vc_ref, vp_ref, o_ref, l_ref, *, unroll):
    for u in range(unroll):
        rs = slice(u * Q_BLOCK, (u + 1) * Q_BLOCK)
        ps = slice((u - 1) * Q_BLOCK, u * Q_BLOCK)
        kp, vp = (kp_ref[...], vp_ref[...]) if u == 0 else (kc_ref[ps, :], vc_ref[ps, :])
        first = (pl.program_id(1) == 0) if u == 0 else False
        o, lse = _dil_heads(q_ref[rs, :], kc_ref[rs, :], vc_ref[rs, :], kp, vp, first)
        for half in range(A_GROUP_WIDTH // LANES):
            o_ref[half, rs, :] = o[:, half * LANES:(half + 1) * LANES]
            l_ref[half, rs, :] = lse[:, half * LANES:(half + 1) * LANES]


def _dil_attn_strided_kernel(q_ref, k_ref, v_ref, o_ref, l_ref, qf_ref, kf_ref, vf_ref, *, dil, has_prev, unroll):
    i = pl.program_id(1)
    step = pl.program_id(2)
    halves = A_GROUP_WIDTH // LANES
    slot = lax.rem(i, 2) if has_prev else 0

    @pl.when(step == 0)
    def _():
        for half in range(halves):
            cs = slice(half * LANES, (half + 1) * LANES)
            qf_ref[half] = q_ref[:, cs].astype(F32)
            kf_ref[slot, half] = k_ref[:, cs].astype(F32)
            vf_ref[slot, half] = v_ref[:, cs].astype(F32)

    if has_prev:
        @pl.when((step == 0) & (i == 0))
        def _():
            kf_ref[1] = jnp.zeros(kf_ref.shape[1:], F32)
            vf_ref[1] = jnp.zeros(vf_ref.shape[1:], F32)

    for u in range(unroll):
        r = step * unroll + u

        def rows_of_class(ref, *lead, r=r):
            return jnp.concatenate([ref[(*lead, half, pl.ds(r, Q_BLOCK, stride=dil), slice(None))]
                                    for half in range(halves)], axis=1).astype(BF16)

        kp = rows_of_class(kf_ref, 1 - slot) if has_prev else None
        vp = rows_of_class(vf_ref, 1 - slot) if has_prev else None
        o, lse = _dil_heads(rows_of_class(qf_ref), rows_of_class(kf_ref, slot), rows_of_class(vf_ref, slot),
                            kp, vp, i == 0)
        for half in range(halves):
            o_ref[half, pl.ds(r, Q_BLOCK, stride=dil), :] = o[:, half * LANES:(half + 1) * LANES]
            l_ref[half, pl.ds(r, Q_BLOCK, stride=dil), :] = lse[:, half * LANES:(half + 1) * LANES]


def _dil_attn(cb3, gi):
    bsz, s, _ = cb3.shape
    dil = A_GROUPS[gi][1]
    w = A_GROUP_WIDTH
    halves = w // LANES
    out = jax.ShapeDtypeStruct((bsz, halves, s, LANES), F32)
    name = f"dilated_attention_g{gi}"
    unroll = DIL_TILES[gi]
    if dil == 1:
        step_rows = Q_BLOCK * unroll

        def col(off, prev):
            if prev:
                return pl.BlockSpec((None, Q_BLOCK, w),
                                    lambda b, i: (b, jnp.maximum(i * unroll - 1, 0), (off + gi * w) // w))
            return pl.BlockSpec((None, step_rows, w), lambda b, i: (b, i, (off + gi * w) // w))

        ospec = pl.BlockSpec((None, halves, step_rows, LANES), lambda b, i: (b, 0, i, 0))
        return pl.pallas_call(
            functools.partial(_dil_attn_dense_kernel, unroll=unroll),
            out_shape=(out, out),
            grid=(bsz, s // step_rows),
            in_specs=[col(BOFF_QA, False), col(BOFF_KA, False), col(BOFF_KA, True), col(BOFF_VA, False), col(BOFF_VA, True)],
            out_specs=(ospec, ospec),
            compiler_params=_params("parallel", "arbitrary"),
            name=name,
        )(cb3, cb3, cb3, cb3, cb3)

    rows = Q_BLOCK * dil
    tiles = s // rows
    has_prev = tiles > 1

    def col(off):
        return pl.BlockSpec((None, rows, w), lambda b, i, r: (b, i, (off + gi * w) // w))

    ospec = pl.BlockSpec((None, halves, rows, LANES), lambda b, i, r: (b, 0, i, 0))
    return pl.pallas_call(
        functools.partial(_dil_attn_strided_kernel, dil=dil, has_prev=has_prev, unroll=unroll),
        out_shape=(out, out),
        grid=(bsz, tiles, dil // unroll),
        in_specs=[col(BOFF_QA), col(BOFF_KA), col(BOFF_VA)],
        out_specs=(ospec, ospec),
        scratch_shapes=[pltpu.VMEM((halves, rows, LANES), F32),
                        pltpu.VMEM((2 if has_prev else 1, halves, rows, LANES), F32),
                        pltpu.VMEM((2 if has_prev else 1, halves, rows, LANES), F32)],
        compiler_params=_params("parallel", "arbitrary", "arbitrary"),
        name=name,
    )(cb3, cb3, cb3)


def _pool_kernel(x_ref, w_ref, sc_ref, o_ref):
    s = x_ref.shape[0]
    t = lax.broadcasted_iota(jnp.int32, (s, B_GROUP_DIM), 0)
    for gi, win in enumerate(B_WINDOWS):
        cs = slice(gi * B_GROUP_DIM, (gi + 1) * B_GROUP_DIM)
        x = x_ref[:, cs]
        acc = x
        k = 1
        while k < win:
            acc = acc + jnp.where(t >= k, pltpu.roll(acc, k, axis=0), 0.0)
            k *= 2
        cnt = jnp.minimum(t + 1, win).astype(F32)
        z = acc / cnt - x
        o_ref[:, cs] = _dot(z.astype(BF16), w_ref[gi]) * sc_ref[:, cs]


def _pool_mixer(cf3, pool_w, pool_scale, layer):
    bsz, s, _ = cf3.shape
    ng = len(B_WINDOWS)
    return pl.pallas_call(
        _pool_kernel,
        out_shape=jax.ShapeDtypeStruct((bsz, s, B_WIDTH), F32),
        grid=(bsz,),
        in_specs=[
            pl.BlockSpec((None, s, B_WIDTH), lambda b: (b, 0, FOFF_XB // B_WIDTH)),
            pl.BlockSpec((None, ng, B_GROUP_DIM, B_GROUP_DIM), lambda b: (layer, 0, 0, 0)),
            pl.BlockSpec((None, 1, B_WIDTH), lambda b: (layer, 0, 0)),
        ],
        out_specs=pl.BlockSpec((None, s, B_WIDTH), lambda b: (b, 0, 0)),
        compiler_params=_params("parallel"),
        name="pool_mixer",
    )(cf3, pool_w, pool_scale)


def _gelu_tanh(x):
    return 0.5 * x * (1.0 + jnp.tanh(np.float32(np.sqrt(2.0 / np.pi)) * (x + 0.044715 * (x * x * x))))


def _compress_one(z_ref, pe_ref, w1_ref, w2_ref, o_ref):
    nch = z_ref.shape[0] // CMP_STRIDE
    dh = C_HEAD_DIM
    first = jnp.zeros((nch, dh), F32)
    second = jnp.zeros((nch, dh), F32)
    for p in range(CMP_STRIDE):
        zp = z_ref[pl.ds(p, nch, stride=CMP_STRIDE), :]
        a = (zp + pe_ref[p:p + 1, :]).astype(BF16)
        b = (zp + pe_ref[CMP_STRIDE + p:CMP_STRIDE + p + 1, :]).astype(BF16)
        first = first + _dot(a, w1_ref[p * dh:(p + 1) * dh, :])
        second = second + _dot(b, w1_ref[(CMP_STRIDE + p) * dh:(CMP_STRIDE + p + 1) * dh, :])
    pre = first + pltpu.roll(second, nch - 1, axis=0)
    o_ref[...] = _dot(_gelu_tanh(pre).astype(BF16), w2_ref[...]).astype(BF16)


def _compress_kernel(zk_ref, zv_ref, pek_ref, w1k_ref, w2k_ref, pev_ref, w1v_ref, w2v_ref, kc_ref, vc_ref):
    _compress_one(zk_ref, pek_ref, w1k_ref, w2k_ref, kc_ref)
    _compress_one(zv_ref, pev_ref, w1v_ref, w2v_ref, vc_ref)


def _compress(cf3, pe_k, w1_k, w2_k, pe_v, w1_v, w2_v, layer):
    bsz, s, _ = cf3.shape
    nch = s // CMP_STRIDE
    dh = C_HEAD_DIM
    cdim = CMP_LEN * dh

    def col(off):
        return pl.BlockSpec((None, s, dh), lambda b, g: (b, 0, off // dh + g))

    def wspec(shape):
        return pl.BlockSpec((None,) + shape, lambda b, g: (layer,) + (0,) * len(shape))

    out = jax.ShapeDtypeStruct((bsz, C_KV_GROUPS, nch, dh), BF16)
    ospec = pl.BlockSpec((None, None, nch, dh), lambda b, g: (b, g, 0, 0))
    return pl.pallas_call(
        _compress_kernel,
        out_shape=(out, out),
        grid=(bsz, C_KV_GROUPS),
        in_specs=[col(FOFF_KCMP), col(FOFF_VCMP),
                  wspec((CMP_LEN, dh)), wspec((cdim, dh)), wspec((dh, dh)),
                  wspec((CMP_LEN, dh)), wspec((cdim, dh)), wspec((dh, dh))],
        out_specs=(ospec, ospec),
        compiler_params=_params("parallel", "parallel"),
        name="nsa_compress",
    )(cf3, cf3, pe_k, w1_k, w2_k, pe_v, w1_v, w2_v)


def _split3(x):
    hi = x.astype(BF16)
    r1 = x - hi.astype(F32)
    mid = r1.astype(BF16)
    lo = (r1 - mid.astype(F32)).astype(BF16)
    return hi, mid, lo


def _softmax_pv(s, v):
    m = jnp.max(s, axis=-1, keepdims=True)
    p = jnp.exp2(s - m)
    return _dot(p.astype(BF16), v) / jnp.sum(p, axis=-1, keepdims=True)


def _nsa_tile(t0, q, gates, kc_ref, vc_ref, kw_ref, vw_ref, ovt_ref, nb):
    hg = C_HEADS_PER_GROUP
    dh = C_HEAD_DIM
    row = t0 + lax.broadcasted_iota(jnp.int32, (Q_BLOCK, LANES), 0)
    lane = lax.broadcasted_iota(jnp.int32, (Q_BLOCK, LANES), 1)

    q3 = jnp.concatenate([q[:, h * dh:(h + 1) * dh] for h in range(hg)], axis=0)
    rows = [slice(h * Q_BLOCK, (h + 1) * Q_BLOCK) for h in range(hg)]
    gt = _sigmoid(gates)

    cvalid = lane * CMP_STRIDE + (CMP_LEN - 1) <= row
    s3 = _dot_nt(q3, kc_ref[...])
    vc = vc_ref[...]
    out = []
    psum = jnp.zeros((Q_BLOCK, LANES), F32)
    for h in range(hg):
        s = jnp.where(cvalid, s3[rows[h]], NEG)
        p = jnp.exp2(s - jnp.max(s, axis=-1, keepdims=True))
        p = jnp.where(cvalid, p / jnp.sum(p, axis=-1, keepdims=True), 0.0)
        out.append(gt[:, 3 * h:3 * h + 1] * _dot(p.astype(BF16), vc))
        psum = psum + p

    ovt = ovt_ref[...]
    imp_t = sum(_dot_nt(ovt, part) for part in _split3(psum))
    blk = lax.broadcasted_iota(jnp.int32, (nb, Q_BLOCK), 0)
    tb = lax.shift_right_logical(t0 + lax.broadcasted_iota(jnp.int32, (nb, Q_BLOCK), 1), int(np.log2(SEL_BLOCK)))
    forced = (blk == 0) | (blk == tb) | (blk == tb - 1)
    score = jnp.where(blk > tb, -1.0, imp_t + jnp.where(forced, FORCE_BONUS, 0.0))
    blk_f = blk.astype(F32)
    sel_t = jnp.zeros((nb, Q_BLOCK), F32)
    for _ in range(min(SEL_TOPN, nb)):
        mx = jnp.max(score, axis=0, keepdims=True)
        idx = jnp.min(jnp.where(score == mx, blk_f, float(nb)), axis=0, keepdims=True)
        hit = blk_f == idx
        sel_t = jnp.where(hit, 1.0, sel_t)
        score = jnp.where(hit, -3.0, score)
    sel = jnp.concatenate([sel_t, jnp.zeros((LANES - nb, Q_BLOCK), F32)], axis=0).T
    sel = jnp.where(sel > 0.5, 0.0, NEG).astype(BF16)

    wlen = WIN + Q_BLOCK
    w0 = pl.multiple_of(jnp.maximum(t0 - WIN, 0), Q_BLOCK)
    kpos = w0 + lax.broadcasted_iota(jnp.int32, (Q_BLOCK, wlen), 1)
    qpos = t0 + lax.broadcasted_iota(jnp.int32, (Q_BLOCK, wlen), 0)
    bias = jnp.where((kpos <= qpos) & (kpos > qpos - WIN), 0.0, NEG)
    s3 = _dot_nt(q3, kw_ref[pl.ds(w0, wlen), :])
    vw = vw_ref[pl.ds(w0, wlen), :]
    for h in range(hg):
        out[h] = out[h] + gt[:, 3 * h + 2:3 * h + 3] * _softmax_pv(s3[rows[h]] + bias, vw)
    return q3, gt, out, sel


def _nsa_kernel(q_ref, kc_ref, vc_ref, ks_ref, vs_ref, kw_ref, vw_ref, gt_ref, ovt_ref, ex_ref, o_ref, *, s_len):
    n = pl.program_id(2)
    hg = C_HEADS_PER_GROUP
    dh = C_HEAD_DIM
    tiles = []
    for u in range(NSA_TILES):
        rs = slice(u * Q_BLOCK, (u + 1) * Q_BLOCK)
        t0 = (n * NSA_TILES + u) * Q_BLOCK
        tiles.append((rs, t0) + _nsa_tile(t0, q_ref[rs, :], gt_ref[rs, :], kc_ref, vc_ref, kw_ref, vw_ref, ovt_ref,
                                          s_len // SEL_BLOCK))

    r = lax.broadcasted_iota(jnp.int32, (Q_BLOCK, Q_BLOCK), 0)
    c = lax.broadcasted_iota(jnp.int32, (Q_BLOCK, Q_BLOCK), 1)
    causal = jnp.where(c <= r, 0.0, NEG)
    for step in range(s_len // (Q_BLOCK * NSA_TILES)):

        @pl.when(n == step)
        def _(step=step):
            for u, (rs, _, q3, gt, out, sel) in enumerate(tiles):
                klen = (step * NSA_TILES + u + 1) * Q_BLOCK
                blocked = _dot(sel, ex_ref[:, :klen])
                bias = blocked[:, klen - Q_BLOCK:] + causal
                if klen > Q_BLOCK:
                    bias = jnp.concatenate([blocked[:, :klen - Q_BLOCK], bias], axis=1)
                s3 = _dot_nt(q3, ks_ref[:klen, :])
                vs = vs_ref[:klen, :]
                for h in range(hg):
                    hs = slice(h * Q_BLOCK, (h + 1) * Q_BLOCK)
                    o_ref[rs, h * dh:(h + 1) * dh] = out[h] + gt[:, 3 * h + 1:3 * h + 2] * _softmax_pv(s3[hs] + bias, vs)


def _nsa_constants(s):
    nch = s // CMP_STRIDE
    nb = s // SEL_BLOCK
    n_cmp = (s - CMP_LEN) // CMP_STRIDE + 1
    ci = np.arange(nch)[None, :] * CMP_STRIDE
    bj = np.arange(nb)[:, None] * SEL_BLOCK
    overlap_t = (ci < bj + SEL_BLOCK) & (ci + CMP_LEN > bj) & (np.arange(nch)[None, :] < n_cmp)
    expand = np.arange(LANES)[:, None] == (np.arange(s)[None, :] // SEL_BLOCK)
    return jnp.asarray(overlap_t, BF16), jnp.asarray(expand, BF16)


def _nsa(cb3, kc, vc, gates3):
    bsz, s, _ = cb3.shape
    nq = s // Q_BLOCK
    nch = s // CMP_STRIDE
    nb = s // SEL_BLOCK
    dh = C_HEAD_DIM
    step_rows = Q_BLOCK * NSA_TILES
    assert nch == LANES and nb <= LANES and nb % 8 == 0 and s % step_rows == 0 and s >= WIN + Q_BLOCK
    overlap_t, expand = _nsa_constants(s)

    def kv(off):
        return pl.BlockSpec((None, s, dh), lambda b, g, n: (b, 0, off // dh + g))

    cmp_spec = pl.BlockSpec((None, None, nch, dh), lambda b, g, n: (b, g, 0, 0))
    return pl.pallas_call(
        functools.partial(_nsa_kernel, s_len=s),
        out_shape=jax.ShapeDtypeStruct((bsz, s, C_WIDTH), F32),
        grid=(bsz, C_KV_GROUPS, s // step_rows),
        in_specs=[
            pl.BlockSpec((None, step_rows, C_GROUP_WIDTH), lambda b, g, n: (b, n, BOFF_QC // C_GROUP_WIDTH + g)),
            cmp_spec, cmp_spec,
            kv(BOFF_KSLC), kv(BOFF_VSLC), kv(BOFF_KWIN), kv(BOFF_VWIN),
            pl.BlockSpec((None, step_rows, LANES), lambda b, g, n: (b, n, g)),
            pl.BlockSpec((nb, nch), lambda b, g, n: (0, 0)),
            pl.BlockSpec((LANES, s), lambda b, g, n: (0, 0)),
        ],
        out_specs=pl.BlockSpec((None, step_rows, C_GROUP_WIDTH), lambda b, g, n: (b, n, g)),
        compiler_params=_params("parallel", "parallel", "arbitrary"),
        name="nsa_attention",
    )(cb3, kc, vc, cb3, cb3, cb3, cb3, gates3, overlap_t, expand)


def _merge_kernel(*refs):
    ng = len(A_GROUPS)
    oa_refs, la_refs = refs[:ng], refs[ng:2 * ng]
    zb_ref, oc_ref, g0_ref, g1_ref, g2_ref, x_ref, pa_ref, pb_ref, pc_ref, wo_ref, gn_ref, o_ref, xn_ref = refs[2 * ng:]

    def lanes(ref):
        return jnp.concatenate([ref[half] for half in range(ref.shape[0])], axis=1)

    lse = [lanes(r) for r in la_refs]
    m = functools.reduce(jnp.maximum, lse)
    e = [jnp.exp2(l - m) for l in lse]
    oa = sum(w * lanes(r) for w, r in zip(e, oa_refs)) / sum(e)
    ya = _dot(oa.astype(BF16), pa_ref[...])
    yb = _dot(zb_ref[...].astype(BF16), pb_ref[...])
    yc = _dot(oc_ref[...].astype(BF16), pc_ref[...])
    mix = _sigmoid(g0_ref[...]) * ya + _sigmoid(g1_ref[...]) * yb + _sigmoid(g2_ref[...]) * yc
    y = x_ref[...] + _dot(mix.astype(BF16), wo_ref[...])
    o_ref[...] = y
    xn_ref[...] = _rms_scaled(y, gn_ref[...]).astype(BF16)


def _merge(oa, la, zb, oc, cf, x, proj_a, proj_b, proj_c, w_out, g_next, layer, *, tm=256):
    m, d = x.shape
    assert d == D_GATE

    def rows(w):
        return pl.BlockSpec((tm, w), lambda i: (i, 0))

    def gate(k):
        return pl.BlockSpec((tm, d), lambda i: (i, FOFF_GM // d + k))

    def weight(k):
        return pl.BlockSpec((k, d), lambda i: (0, 0), pipeline_mode=pl.Buffered(1))

    _, halves, s, _ = oa[0].shape
    per_seq = s // tm
    group = pl.BlockSpec((None, halves, tm, LANES), lambda i: (i // per_seq, 0, i % per_seq, 0))

    return pl.pallas_call(
        _merge_kernel,
        out_shape=(jax.ShapeDtypeStruct((m, d), F32), jax.ShapeDtypeStruct((m, d), BF16)),
        grid=(m // tm,),
        in_specs=[group] * (2 * len(A_GROUPS))
        + [rows(B_WIDTH), rows(C_WIDTH), gate(0), gate(1), gate(2), rows(d),
           weight(A_GROUP_WIDTH), weight(B_WIDTH), weight(C_WIDTH), weight(d),
           pl.BlockSpec((None, 1, d), lambda i: (layer, 0, 0))],
        out_specs=(rows(d), rows(d)),
        compiler_params=_params("parallel"),
        name="merge",
    )(*oa, *la, zb, oc, cf, cf, cf, x, proj_a, proj_b, proj_c, w_out, g_next)


_A3 = 3 * A_WIDTH
_XB_END = _A3 + B_WIDTH
_QC_END = _XB_END + C_WIDTH
_CMP_END = _QC_END + 2 * C_KV_WIDTH
_KV_END = _QC_END + 6 * C_KV_WIDTH
_GM_START = _KV_END + N_GATES
_REST_RANGES = ((_A3, _XB_END), (_QC_END, _CMP_END), (0, _A3), (_XB_END, _QC_END), (_CMP_END, _KV_END))


def _reorder_w_in(w_in):
    w_gm = w_in[..., _GM_START:].astype(BF16)
    w_rest = jnp.concatenate([w_in[..., lo:hi] for lo, hi in _REST_RANGES], axis=-1).astype(BF16)
    per_group = 3 * C_HEADS_PER_GROUP
    pad = jnp.zeros(w_in.shape[:-1] + (LANES - per_group,), w_in.dtype)
    w_gate = jnp.concatenate(
        [piece for g in range(C_KV_GROUPS)
         for piece in (w_in[..., _KV_END + g * per_group:_KV_END + (g + 1) * per_group], pad)], axis=-1).astype(BF16)
    col_scale = np.ones((1, N_F32_COLS + N_BF16_COLS), np.float32)
    col_scale[:, N_F32_COLS + BOFF_QA:N_F32_COLS + BOFF_QA + A_WIDTH] = A_HEAD_DIM ** -0.5 * LOG2E
    col_scale[:, N_F32_COLS + BOFF_QC:N_F32_COLS + BOFF_QC + C_WIDTH] = C_HEAD_DIM ** -0.5 * LOG2E
    return w_gm, w_rest, w_gate, jnp.asarray(col_scale)


def _mixing(x, xn, layer, w_gm, w_rest, w_gate, col_scale, pool_w, pool_scale, pe_k, w1_k, w2_k, pe_v, w1_v, w2_v,
            proj_a, proj_b, proj_c, w_out, g_next, bsz, s):
    m, d = x.shape
    cf, cb, gates, (proj_a, proj_b, proj_c, w_out) = _in_proj(
        xn, w_gm, w_rest, w_gate, col_scale, (proj_a, proj_b, proj_c, w_out), layer)
    cf3 = cf.reshape(bsz, s, N_F32_COLS)
    cb3 = cb.reshape(bsz, s, N_BF16_COLS)
    gates3 = gates.reshape(bsz, s, C_KV_GROUPS * LANES)

    oa, la = zip(*[_dil_attn(cb3, gi) for gi in range(len(A_GROUPS))])
    zb = _pool_mixer(cf3, pool_w, pool_scale, layer).reshape(m, B_WIDTH)
    kc, vc = _compress(cf3, pe_k, w1_k, w2_k, pe_v, w1_v, w2_v, layer)
    oc = _nsa(cb3, kc, vc, gates3).reshape(m, C_WIDTH)
    return _merge(oa, la, zb, oc, cf, x, proj_a, proj_b, proj_c, w_out, g_next, layer)


def kernel(x, ffn1_norm, ffn1_wi, ffn1_wo, mix_norm, w_in, pool_w, pool_scale, cmp_pe_k, cmp_w1_k, cmp_w2_k,
           cmp_pe_v, cmp_w1_v, cmp_w2_v, proj_a, proj_b, proj_c, w_out, ffn2_norm, ffn2_wi, ffn2_wo, final_norm):
    bsz, s, d = x.shape
    depth = ffn1_wi.shape[0]
    for win, dil in A_GROUPS:
        assert win // dil == Q_BLOCK and s % (dil * Q_BLOCK) == 0
    assert CMP_LEN == 2 * CMP_STRIDE and all(w & (w - 1) == 0 for w in B_WINDOWS)

    bf = lambda w: w.astype(BF16)
    row3 = lambda g: g.reshape(g.shape[0], 1, g.shape[-1])
    w_gm, w_rest, w_gate, col_scale = _reorder_w_in(w_in)
    pool_w, cmp_w1_k, cmp_w2_k, cmp_w1_v, cmp_w2_v = bf(pool_w), bf(cmp_w1_k), bf(cmp_w2_k), bf(cmp_w1_v), bf(cmp_w2_v)
    ffn1_norm, mix_norm, ffn2_norm, pool_scale = row3(ffn1_norm), row3(mix_norm), row3(ffn2_norm), row3(pool_scale)

    final_norm = final_norm.reshape(1, 1, d)
    x = x.reshape(bsz * s, d)
    xn = _row_norm(x, ffn1_norm, 0)
    for layer in range(depth):
        h, wo = _ffn_up(xn, ffn1_wi, ffn1_wo, layer)
        x, xn = _ffn_down(h, wo, x, mix_norm, layer)
        x, xn = _mixing(x, xn, layer, w_gm, w_rest, w_gate, col_scale, pool_w, pool_scale, cmp_pe_k, cmp_w1_k, cmp_w2_k,
                        cmp_pe_v, cmp_w1_v, cmp_w2_v, proj_a, proj_b, proj_c, w_out, ffn2_norm, bsz, s)
        h, wo = _ffn_up(xn, ffn2_wi, ffn2_wo, layer)
        if layer + 1 < depth:
            x, xn = _ffn_down(h, wo, x, ffn1_norm, layer + 1)
        else:
            x = _ffn_down(h, wo, x, final_norm, 0, last=True)
    return x.reshape(bsz, s, d)
```

```python
import functools

import numpy as np
import jax
import jax.numpy as jnp
from jax import lax
from jax.experimental import pallas as pl
from jax.experimental.pallas import tpu as pltpu

F32 = jnp.float32
BF16 = jnp.bfloat16

EPS = 1e-6
NEG = -1e30
LOG2E = float(np.log2(np.e))
Q_BLOCK = 128
LANES = 128

A_GROUPS = ((128, 1), (512, 4), (2048, 16))
A_HEADS_PER_GROUP = 4
A_HEAD_DIM = 64
A_GROUP_WIDTH = A_HEADS_PER_GROUP * A_HEAD_DIM
A_WIDTH = A_GROUP_WIDTH * len(A_GROUPS)

B_WINDOWS = (2, 4, 8, 16)
B_GROUP_DIM = 128
B_WIDTH = B_GROUP_DIM * len(B_WINDOWS)

C_KV_GROUPS = 2
C_HEADS_PER_GROUP = 3
C_HEADS = C_KV_GROUPS * C_HEADS_PER_GROUP
C_HEAD_DIM = 128
C_GROUP_WIDTH = C_HEADS_PER_GROUP * C_HEAD_DIM
C_WIDTH = C_HEADS * C_HEAD_DIM
C_KV_WIDTH = C_KV_GROUPS * C_HEAD_DIM
CMP_LEN = 32
CMP_STRIDE = 16
SEL_BLOCK = 64
SEL_TOPN = 8
FORCE_BONUS = 100.0
WIN = 512
N_GATES = 3 * C_HEADS
NSA_TILES = 4
DIL_TILES = (8, 4, 8)

N_BRANCH = 3
D_GATE = 2048

FOFF_GM = 0
FOFF_XB = N_BRANCH * D_GATE
FOFF_KCMP = FOFF_XB + B_WIDTH
FOFF_VCMP = FOFF_KCMP + C_KV_WIDTH
N_F32_COLS = FOFF_VCMP + C_KV_WIDTH
BOFF_QA = 0
BOFF_KA = BOFF_QA + A_WIDTH
BOFF_VA = BOFF_KA + A_WIDTH
BOFF_QC = BOFF_VA + A_WIDTH
BOFF_KSLC = BOFF_QC + C_WIDTH
BOFF_VSLC = BOFF_KSLC + C_KV_WIDTH
BOFF_KWIN = BOFF_VSLC + C_KV_WIDTH
BOFF_VWIN = BOFF_KWIN + C_KV_WIDTH
N_BF16_COLS = BOFF_VWIN + C_KV_WIDTH

VMEM_LIMIT = 56 * 1024 * 1024
ROW_TILE = 1024
COL_TILE = 512
FFN_ROW_TILE = 1024

def _params(*sem):
    return pltpu.CompilerParams(dimension_semantics=sem, vmem_limit_bytes=VMEM_LIMIT)


def _dot(a, b):
    return jnp.dot(a, b, preferred_element_type=F32)


def _dot_nt(a, b):
    return lax.dot_general(a, b, (((1,), (1,)), ((), ())), preferred_element_type=F32)


def _sigmoid(x):
    return 0.5 * jnp.tanh(0.5 * x) + 0.5


def _rms_scaled(x, g):
    return x * lax.rsqrt(jnp.mean(x * x, axis=-1, keepdims=True) + EPS) * g


def _row_norm_kernel(x_ref, g_ref, o_ref):
    o_ref[...] = _rms_scaled(x_ref[...], g_ref[...]).astype(o_ref.dtype)


def _row_norm(x, g, layer, *, tm=512):
    m, d = x.shape
    return pl.pallas_call(
        _row_norm_kernel,
        out_shape=jax.ShapeDtypeStruct((m, d), BF16),
        grid=(m // tm,),
        in_specs=[pl.BlockSpec((tm, d), lambda i: (i, 0)), pl.BlockSpec((None, 1, d), lambda i: (layer, 0, 0))],
        out_specs=pl.BlockSpec((tm, d), lambda i: (i, 0)),
        compiler_params=_params("parallel"),
        name="row_norm",
    )(x, g)


def _ffn_up_kernel(xn_ref, wa_ref, wb_ref, wo_ref, h_ref, wo_bf_ref, w_ref):
    tn = h_ref.shape[-1]

    @pl.when(pl.program_id(1) == 0)
    def _():
        w_ref[:, :tn] = wa_ref[...].astype(BF16)
        w_ref[:, tn:] = wb_ref[...].astype(BF16)

    wo_bf_ref[...] = wo_ref[...].astype(BF16)
    xn = xn_ref[...]
    a = _dot(xn, w_ref[:, :tn])
    b = _dot(xn, w_ref[:, tn:])
    h_ref[...] = (a * _sigmoid(a) * b).astype(BF16)


def _ffn_up(xn, wi, wo, layer, *, tm=FFN_ROW_TILE, tn=COL_TILE):
    m, d = xn.shape
    d_ff = wi.shape[-1] // 2
    nj, ni = d_ff // tn, m // tm
    wo_rows = d_ff // (nj * ni)
    assert wo_rows * nj * ni == d_ff and wo_rows % 16 == 0
    return pl.pallas_call(
        _ffn_up_kernel,
        out_shape=(jax.ShapeDtypeStruct((m, d_ff), BF16), jax.ShapeDtypeStruct((d_ff, d), BF16)),
        grid=(nj, ni),
        in_specs=[
            pl.BlockSpec((tm, d), lambda j, i: (i, 0)),
            pl.BlockSpec((None, d, tn), lambda j, i: (layer, 0, j)),
            pl.BlockSpec((None, d, tn), lambda j, i: (layer, 0, j + nj)),
            pl.BlockSpec((None, wo_rows, d), lambda j, i: (layer, j * ni + i, 0)),
        ],
        out_specs=(pl.BlockSpec((tm, tn), lambda j, i: (i, j)),
                   pl.BlockSpec((wo_rows, d), lambda j, i: (j * ni + i, 0))),
        scratch_shapes=[pltpu.VMEM((d, 2 * tn), BF16)],
        compiler_params=_params("parallel", "arbitrary"),
        name="ffn_up",
    )(xn, wi, wi, wo)


def _ffn_down_kernel(h_ref, w_ref, r_ref, g_ref, *o_refs, last):
    y = r_ref[...] + 0.5 * _dot(h_ref[...], w_ref[...])
    if last:
        o_refs[0][...] = _rms_scaled(y, g_ref[...])
    else:
        o_refs[0][...] = y
        o_refs[1][...] = _rms_scaled(y, g_ref[...]).astype(BF16)


def _ffn_down(h, wo, res, g_next, g_layer, *, last=False, tm=256):
    m, d_ff = h.shape
    d = res.shape[-1]
    rows = pl.BlockSpec((tm, d), lambda i: (i, 0))
    x_out = jax.ShapeDtypeStruct((m, d), F32)
    return pl.pallas_call(
        functools.partial(_ffn_down_kernel, last=last),
        out_shape=x_out if last else (x_out, jax.ShapeDtypeStruct((m, d), BF16)),
        grid=(m // tm,),
        in_specs=[
            pl.BlockSpec((tm, d_ff), lambda i: (i, 0)),
            pl.BlockSpec((d_ff, d), lambda i: (0, 0), pipeline_mode=pl.Buffered(1)),
            rows,
            pl.BlockSpec((None, 1, d), lambda i: (g_layer, 0, 0)),
        ],
        out_specs=rows if last else (rows, rows),
        compiler_params=_params("parallel"),
        name="ffn_down",
    )(h, wo, res, g_next)


def _in_proj_kernel(xn_ref, wgm_ref, wr_ref, wg_ref, sc_ref, of_ref, ob_ref, og_ref, *, n_gm_tiles, n_f32_tiles):
    j = pl.program_id(1)

    @pl.when(j == 0)
    def _():
        og_ref[...] = _dot(xn_ref[...], wg_ref[...])

    @pl.when(j < n_gm_tiles)
    def _():
        of_ref[...] = _dot(xn_ref[...], wgm_ref[...])

    @pl.when((j >= n_gm_tiles) & (j < n_f32_tiles))
    def _():
        of_ref[...] = _dot(xn_ref[...], wr_ref[...])

    @pl.when(j >= n_f32_tiles)
    def _():
        ob_ref[...] = (_dot(xn_ref[...], wr_ref[...]) * sc_ref[...]).astype(BF16)


def _in_proj(xn, w_gm, w_rest, w_gate, col_scale, layer, *, tm=ROW_TILE, tn=2 * COL_TILE):
    m, d = xn.shape
    ng = w_gate.shape[-1]
    ngm = w_gm.shape[-1] // tn
    nf = N_F32_COLS // tn
    nb = N_BF16_COLS // tn
    assert w_gm.shape[-1] + w_rest.shape[-1] == N_F32_COLS + N_BF16_COLS and ngm < nf
    return pl.pallas_call(
        functools.partial(_in_proj_kernel, n_gm_tiles=ngm, n_f32_tiles=nf),
        out_shape=(jax.ShapeDtypeStruct((m, N_F32_COLS), F32), jax.ShapeDtypeStruct((m, N_BF16_COLS), BF16),
                   jax.ShapeDtypeStruct((m, ng), F32)),
        grid=(m // tm, nf + nb),
        in_specs=[
            pl.BlockSpec((tm, d), lambda i, j: (i, 0)),
            pl.BlockSpec((None, d, tn), lambda i, j: (layer, 0, jnp.minimum(j, ngm - 1))),
            pl.BlockSpec((None, d, tn), lambda i, j: (layer, 0, jnp.maximum(j - ngm, 0))),
            pl.BlockSpec((None, d, ng), lambda i, j: (layer, 0, 0)),
            pl.BlockSpec((1, tn), lambda i, j: (0, j)),
        ],
        out_specs=(
            pl.BlockSpec((tm, tn), lambda i, j: (i, jnp.minimum(j, nf - 1))),
            pl.BlockSpec((tm, tn), lambda i, j: (i, jnp.maximum(j - nf, 0))),
            pl.BlockSpec((tm, ng), lambda i, j: (i, 0)),
        ),
        compiler_params=_params("parallel", "arbitrary"),
        name="in_proj",
    )(xn, w_gm, w_rest, w_gate, col_scale)


def _dil_heads(q, kc, vc, kp, vp, first):
    r = lax.broadcasted_iota(jnp.int32, (Q_BLOCK, Q_BLOCK), 0)
    c = lax.broadcasted_iota(jnp.int32, (Q_BLOCK, Q_BLOCK), 1)
    bias_cur = jnp.where(c <= r, 0.0, NEG)
    if kp is not None:
        bias_prev = jnp.where(c >= r + jnp.where(first, Q_BLOCK, 0), 0.0, NEG)
    head = lax.shift_right_logical(lax.broadcasted_iota(jnp.int32, (Q_BLOCK, A_GROUP_WIDTH), 1),
                                   int(np.log2(A_HEAD_DIM)))
    o_acc = jnp.zeros((Q_BLOCK, A_GROUP_WIDTH), F32)
    l_acc = jnp.zeros((Q_BLOCK, A_GROUP_WIDTH), F32)
    for h in range(A_HEADS_PER_GROUP):
        mine = head == h
        qh = q * jnp.where(mine, 1.0, 0.0).astype(BF16)
        s_c = _dot_nt(qh, kc) + bias_cur
        m = jnp.max(s_c, axis=-1, keepdims=True)
        if kp is not None:
            s_p = _dot_nt(qh, kp) + bias_prev
            m = jnp.maximum(m, jnp.max(s_p, axis=-1, keepdims=True))
        p_c = jnp.exp2(s_c - m)
        l = jnp.sum(p_c, axis=-1, keepdims=True)
        pv = _dot(p_c.astype(BF16), vc)
        if kp is not None:
            p_p = jnp.exp2(s_p - m)
            l = l + jnp.sum(p_p, axis=-1, keepdims=True)
            pv = pv + _dot(p_p.astype(BF16), vp)
        o_acc = jnp.where(mine, pv / l, o_acc)
        l_acc = jnp.where(mine, m + jnp.log2(l), l_acc)
    return o_acc, l_acc


def _dil_attn_dense_kernel(q_ref, kc_ref, kp_ref, vc_ref, vp_ref, o_ref, l_ref, *, unroll):
    for u in range(unroll):
        rs = slice(u * Q_BLOCK, (u + 1) * Q_BLOCK)
        ps = slice((u - 1) * Q_BLOCK, u * Q_BLOCK)
        kp, vp = (kp_ref[...], vp_ref[...]) if u == 0 else (kc_ref[ps, :], vc_ref[ps, :])
        first = (pl.program_id(1) == 0) if u == 0 else False
        o, lse = _dil_heads(q_ref[rs, :], kc_ref[rs, :], vc_ref[rs, :], kp, vp, first)
        for half in range(A_GROUP_WIDTH // LANES):
            o_ref[half, rs, :] = o[:, half * LANES:(half + 1) * LANES]
            l_ref[half, rs, :] = lse[:, half * LANES:(half + 1) * LANES]


def _dil_attn_strided_kernel(q_ref, k_ref, v_ref, o_ref, l_ref, qf_ref, kf_ref, vf_ref, *, dil, has_prev, unroll):
    i = pl.program_id(1)
    step = pl.program_id(2)
    halves = A_GROUP_WIDTH // LANES
    slot = lax.rem(i, 2) if has_prev else 0

    @pl.when(step == 0)
    def _():
        for half in range(halves):
            cs = slice(half * LANES, (half + 1) * LANES)
            qf_ref[half] = q_ref[:, cs].astype(F32)
            kf_ref[slot, half] = k_ref[:, cs].astype(F32)
            vf_ref[slot, half] = v_ref[:, cs].astype(F32)

    if has_prev:
        @pl.when((step == 0) & (i == 0))
        def _():
            kf_ref[1] = jnp.zeros(kf_ref.shape[1:], F32)
            vf_ref[1] = jnp.zeros(vf_ref.shape[1:], F32)

    for u in range(unroll):
        r = step * unroll + u

        def rows_of_class(ref, *lead, r=r):
            return jnp.concatenate([ref[(*lead, half, pl.ds(r, Q_BLOCK, stride=dil), slice(None))]
                                    for half in range(halves)], axis=1).astype(BF16)

        kp = rows_of_class(kf_ref, 1 - slot) if has_prev else None
        vp = rows_of_class(vf_ref, 1 - slot) if has_prev else None
        o, lse = _dil_heads(rows_of_class(qf_ref), rows_of_class(kf_ref, slot), rows_of_class(vf_ref, slot),
                            kp, vp, i == 0)
        for half in range(halves):
            o_ref[half, pl.ds(r, Q_BLOCK, stride=dil), :] = o[:, half * LANES:(half + 1) * LANES]
            l_ref[half, pl.ds(r, Q_BLOCK, stride=dil), :] = lse[:, half * LANES:(half + 1) * LANES]


def _dil_attn(cb3, gi):
    bsz, s, _ = cb3.shape
    dil = A_GROUPS[gi][1]
    w = A_GROUP_WIDTH
    halves = w // LANES
    out = jax.ShapeDtypeStruct((bsz, halves, s, LANES), F32)
    name = f"dilated_attention_g{gi}"
    unroll = DIL_TILES[gi]
    if dil == 1:
        step_rows = Q_BLOCK * unroll

        def col(off, prev):
            if prev:
                return pl.BlockSpec((None, Q_BLOCK, w),
                                    lambda b, i: (b, jnp.maximum(i * unroll - 1, 0), (off + gi * w) // w))
            return pl.BlockSpec((None, step_rows, w), lambda b, i: (b, i, (off + gi * w) // w))

        ospec = pl.BlockSpec((None, halves, step_rows, LANES), lambda b, i: (b, 0, i, 0))
        return pl.pallas_call(
            functools.partial(_dil_attn_dense_kernel, unroll=unroll),
            out_shape=(out, out),
            grid=(bsz, s // step_rows),
            in_specs=[col(BOFF_QA, False), col(BOFF_KA, False), col(BOFF_KA, True), col(BOFF_VA, False), col(BOFF_VA, True)],
            out_specs=(ospec, ospec),
            compiler_params=_params("parallel", "arbitrary"),
            name=name,
        )(cb3, cb3, cb3, cb3, cb3)

    rows = Q_BLOCK * dil
    tiles = s // rows
    has_prev = tiles > 1

    def col(off):
        return pl.BlockSpec((None, rows, w), lambda b, i, r: (b, i, (off + gi * w) // w))

    ospec = pl.BlockSpec((None, halves, rows, LANES), lambda b, i, r: (b, 0, i, 0))
    return pl.pallas_call(
        functools.partial(_dil_attn_strided_kernel, dil=dil, has_prev=has_prev, unroll=unroll),
        out_shape=(out, out),
        grid=(bsz, tiles, dil // unroll),
        in_specs=[col(BOFF_QA), col(BOFF_KA), col(BOFF_VA)],
        out_specs=(ospec, ospec),
        scratch_shapes=[pltpu.VMEM((halves, rows, LANES), F32),
                        pltpu.VMEM((2 if has_prev else 1, halves, rows, LANES), F32),
                        pltpu.VMEM((2 if has_prev else 1, halves, rows, LANES), F32)],
        compiler_params=_params("parallel", "arbitrary", "arbitrary"),
        name=name,
    )(cb3, cb3, cb3)


def _pool_kernel(x_ref, w_ref, sc_ref, o_ref):
    s = x_ref.shape[0]
    t = lax.broadcasted_iota(jnp.int32, (s, B_GROUP_DIM), 0)
    for gi, win in enumerate(B_WINDOWS):
        cs = slice(gi * B_GROUP_DIM, (gi + 1) * B_GROUP_DIM)
        x = x_ref[:, cs]
        acc = x
        k = 1
        while k < win:
            acc = acc + jnp.where(t >= k, pltpu.roll(acc, k, axis=0), 0.0)
            k *= 2
        cnt = jnp.minimum(t + 1, win).astype(F32)
        z = acc / cnt - x
        o_ref[:, cs] = _dot(z.astype(BF16), w_ref[gi]) * sc_ref[:, cs]


def _pool_mixer(cf3, pool_w, pool_scale, layer):
    bsz, s, _ = cf3.shape
    ng = len(B_WINDOWS)
    return pl.pallas_call(
        _pool_kernel,
        out_shape=jax.ShapeDtypeStruct((bsz, s, B_WIDTH), F32),
        grid=(bsz,),
        in_specs=[
            pl.BlockSpec((None, s, B_WIDTH), lambda b: (b, 0, FOFF_XB // B_WIDTH)),
            pl.BlockSpec((None, ng, B_GROUP_DIM, B_GROUP_DIM), lambda b: (layer, 0, 0, 0)),
            pl.BlockSpec((None, 1, B_WIDTH), lambda b: (layer, 0, 0)),
        ],
        out_specs=pl.BlockSpec((None, s, B_WIDTH), lambda b: (b, 0, 0)),
        compiler_params=_params("parallel"),
        name="pool_mixer",
    )(cf3, pool_w, pool_scale)


def _gelu_tanh(x):
    return 0.5 * x * (1.0 + jnp.tanh(np.float32(np.sqrt(2.0 / np.pi)) * (x + 0.044715 * (x * x * x))))


def _compress_one(z_ref, pe_ref, w1_ref, w2_ref, o_ref):
    nch = z_ref.shape[0] // CMP_STRIDE
    dh = C_HEAD_DIM
    first = jnp.zeros((nch, dh), F32)
    second = jnp.zeros((nch, dh), F32)
    for p in range(CMP_STRIDE):
        zp = z_ref[pl.ds(p, nch, stride=CMP_STRIDE), :]
        a = (zp + pe_ref[p:p + 1, :]).astype(BF16)
        b = (zp + pe_ref[CMP_STRIDE + p:CMP_STRIDE + p + 1, :]).astype(BF16)
        first = first + _dot(a, w1_ref[p * dh:(p + 1) * dh, :])
        second = second + _dot(b, w1_ref[(CMP_STRIDE + p) * dh:(CMP_STRIDE + p + 1) * dh, :])
    pre = first + pltpu.roll(second, nch - 1, axis=0)
    o_ref[...] = _dot(_gelu_tanh(pre).astype(BF16), w2_ref[...]).astype(BF16)


def _compress_kernel(zk_ref, zv_ref, pek_ref, w1k_ref, w2k_ref, pev_ref, w1v_ref, w2v_ref, kc_ref, vc_ref):
    _compress_one(zk_ref, pek_ref, w1k_ref, w2k_ref, kc_ref)
    _compress_one(zv_ref, pev_ref, w1v_ref, w2v_ref, vc_ref)


def _compress(cf3, pe_k, w1_k, w2_k, pe_v, w1_v, w2_v, layer):
    bsz, s, _ = cf3.shape
    nch = s // CMP_STRIDE
    dh = C_HEAD_DIM
    cdim = CMP_LEN * dh

    def col(off):
        return pl.BlockSpec((None, s, dh), lambda b, g: (b, 0, off // dh + g))

    def wspec(shape):
        return pl.BlockSpec((None,) + shape, lambda b, g: (layer,) + (0,) * len(shape))

    out = jax.ShapeDtypeStruct((bsz, C_KV_GROUPS, nch, dh), BF16)
    ospec = pl.BlockSpec((None, None, nch, dh), lambda b, g: (b, g, 0, 0))
    return pl.pallas_call(
        _compress_kernel,
        out_shape=(out, out),
        grid=(bsz, C_KV_GROUPS),
        in_specs=[col(FOFF_KCMP), col(FOFF_VCMP),
                  wspec((CMP_LEN, dh)), wspec((cdim, dh)), wspec((dh, dh)),
                  wspec((CMP_LEN, dh)), wspec((cdim, dh)), wspec((dh, dh))],
        out_specs=(ospec, ospec),
        compiler_params=_params("parallel", "parallel"),
        name="nsa_compress",
    )(cf3, cf3, pe_k, w1_k, w2_k, pe_v, w1_v, w2_v)


def _pool_compress_kernel(x_ref, pw_ref, sc_ref, zk0_ref, zk1_ref, zv0_ref, zv1_ref,
                          pek_ref, w1k_ref, w2k_ref, pev_ref, w1v_ref, w2v_ref, o_ref, kc_ref, vc_ref):
    _pool_kernel(x_ref, pw_ref, sc_ref, o_ref)
    for g, (zk_ref, zv_ref) in enumerate(((zk0_ref, zv0_ref), (zk1_ref, zv1_ref))):
        _compress_one(zk_ref, pek_ref, w1k_ref, w2k_ref, kc_ref.at[g])
        _compress_one(zv_ref, pev_ref, w1v_ref, w2v_ref, vc_ref.at[g])


def _pool_compress(cf3, pool_w, pool_scale, pe_k, w1_k, w2_k, pe_v, w1_v, w2_v, layer):
    bsz, s, _ = cf3.shape
    assert C_KV_GROUPS == 2
    nch = s // CMP_STRIDE
    dh = C_HEAD_DIM
    cdim = CMP_LEN * dh

    def col(off, g):
        return pl.BlockSpec((None, s, dh), lambda b: (b, 0, off // dh + g))

    def wspec(shape):
        return pl.BlockSpec((None,) + shape, lambda b: (layer,) + (0,) * len(shape))

    cmp_out = jax.ShapeDtypeStruct((bsz, C_KV_GROUPS, nch, dh), BF16)
    cmp_spec = pl.BlockSpec((None, C_KV_GROUPS, nch, dh), lambda b: (b, 0, 0, 0))
    return pl.pallas_call(
        _pool_compress_kernel,
        out_shape=(jax.ShapeDtypeStruct((bsz, s, B_WIDTH), F32), cmp_out, cmp_out),
        grid=(bsz,),
        in_specs=[pl.BlockSpec((None, s, B_WIDTH), lambda b: (b, 0, FOFF_XB // B_WIDTH)),
                  wspec((len(B_WINDOWS), B_GROUP_DIM, B_GROUP_DIM)), wspec((1, B_WIDTH)),
                  col(FOFF_KCMP, 0), col(FOFF_KCMP, 1), col(FOFF_VCMP, 0), col(FOFF_VCMP, 1),
                  wspec((CMP_LEN, dh)), wspec((cdim, dh)), wspec((dh, dh)),
                  wspec((CMP_LEN, dh)), wspec((cdim, dh)), wspec((dh, dh))],
        out_specs=(pl.BlockSpec((None, s, B_WIDTH), lambda b: (b, 0, 0)), cmp_spec, cmp_spec),
        compiler_params=_params("parallel"),
        name="pool_compress",
    )(cf3, pool_w, pool_scale, cf3, cf3, cf3, cf3, pe_k, w1_k, w2_k, pe_v, w1_v, w2_v)


def _split3(x):
    hi = x.astype(BF16)
    r1 = x - hi.astype(F32)
    mid = r1.astype(BF16)
    lo = (r1 - mid.astype(F32)).astype(BF16)
    return hi, mid, lo


def _softmax_pv(s, v):
    m = jnp.max(s, axis=-1, keepdims=True)
    p = jnp.exp2(s - m)
    return _dot(p.astype(BF16), v) / jnp.sum(p, axis=-1, keepdims=True)


def _nsa_tile(t0, q, gates, kc_ref, vc_ref, kw_ref, vw_ref, ovt_ref, nb):
    hg = C_HEADS_PER_GROUP
    dh = C_HEAD_DIM
    row = t0 + lax.broadcasted_iota(jnp.int32, (Q_BLOCK, LANES), 0)
    lane = lax.broadcasted_iota(jnp.int32, (Q_BLOCK, LANES), 1)

    q3 = jnp.concatenate([q[:, h * dh:(h + 1) * dh] for h in range(hg)], axis=0)
    rows = [slice(h * Q_BLOCK, (h + 1) * Q_BLOCK) for h in range(hg)]
    gt = _sigmoid(gates)

    cvalid = lane * CMP_STRIDE + (CMP_LEN - 1) <= row
    s3 = _dot_nt(q3, kc_ref[...])
    vc = vc_ref[...]
    out = []
    psum = jnp.zeros((Q_BLOCK, LANES), F32)
    for h in range(hg):
        s = jnp.where(cvalid, s3[rows[h]], NEG)
        p = jnp.exp2(s - jnp.max(s, axis=-1, keepdims=True))
        p = jnp.where(cvalid, p / jnp.sum(p, axis=-1, keepdims=True), 0.0)
        out.append(gt[:, 3 * h:3 * h + 1] * _dot(p.astype(BF16), vc))
        psum = psum + p

    ovt = ovt_ref[...]
    imp_t = sum(_dot_nt(ovt, part) for part in _split3(psum))
    blk = lax.broadcasted_iota(jnp.int32, (nb, Q_BLOCK), 0)
    tb = lax.shift_right_logical(t0 + lax.broadcasted_iota(jnp.int32, (nb, Q_BLOCK), 1), int(np.log2(SEL_BLOCK)))
    forced = (blk == 0) | (blk == tb) | (blk == tb - 1)
    score = jnp.where(blk > tb, -1.0, imp_t + jnp.where(forced, FORCE_BONUS, 0.0))
    blk_f = blk.astype(F32)
    sel_t = jnp.zeros((nb, Q_BLOCK), F32)
    for _ in range(min(SEL_TOPN, nb)):
        mx = jnp.max(score, axis=0, keepdims=True)
        idx = jnp.min(jnp.where(score == mx, blk_f, float(nb)), axis=0, keepdims=True)
        hit = blk_f == idx
        sel_t = jnp.where(hit, 1.0, sel_t)
        score = jnp.where(hit, -3.0, score)
    sel = jnp.concatenate([sel_t, jnp.zeros((LANES - nb, Q_BLOCK), F32)], axis=0).T
    sel = jnp.where(sel > 0.5, 0.0, NEG).astype(BF16)

    wlen = WIN + Q_BLOCK
    w0 = pl.multiple_of(jnp.maximum(t0 - WIN, 0), Q_BLOCK)
    kpos = w0 + lax.broadcasted_iota(jnp.int32, (Q_BLOCK, wlen), 1)
    qpos = t0 + lax.broadcasted_iota(jnp.int32, (Q_BLOCK, wlen), 0)
    bias = jnp.where((kpos <= qpos) & (kpos > qpos - WIN), 0.0, NEG)
    s3 = _dot_nt(q3, kw_ref[pl.ds(w0, wlen), :])
    vw = vw_ref[pl.ds(w0, wlen), :]
    for h in range(hg):
        out[h] = out[h] + gt[:, 3 * h + 2:3 * h + 3] * _softmax_pv(s3[rows[h]] + bias, vw)
    return q3, gt, out, sel


def _nsa_kernel(q_ref, kc_ref, vc_ref, ks_ref, vs_ref, kw_ref, vw_ref, gt_ref, ovt_ref, ex_ref, o_ref, *, s_len):
    n = pl.program_id(2)
    hg = C_HEADS_PER_GROUP
    dh = C_HEAD_DIM
    tiles = []
    for u in range(NSA_TILES):
        rs = slice(u * Q_BLOCK, (u + 1) * Q_BLOCK)
        t0 = (n * NSA_TILES + u) * Q_BLOCK
        tiles.append((rs, t0) + _nsa_tile(t0, q_ref[rs, :], gt_ref[rs, :], kc_ref, vc_ref, kw_ref, vw_ref, ovt_ref,
                                          s_len // SEL_BLOCK))

    r = lax.broadcasted_iota(jnp.int32, (Q_BLOCK, Q_BLOCK), 0)
    c = lax.broadcasted_iota(jnp.int32, (Q_BLOCK, Q_BLOCK), 1)
    causal = jnp.where(c <= r, 0.0, NEG)
    for step in range(s_len // (Q_BLOCK * NSA_TILES)):

        @pl.when(n == step)
        def _(step=step):
            for u, (rs, _, q3, gt, out, sel) in enumerate(tiles):
                klen = (step * NSA_TILES + u + 1) * Q_BLOCK
                blocked = _dot(sel, ex_ref[:, :klen])
                bias = blocked[:, klen - Q_BLOCK:] + causal
                if klen > Q_BLOCK:
                    bias = jnp.concatenate([blocked[:, :klen - Q_BLOCK], bias], axis=1)
                s3 = _dot_nt(q3, ks_ref[:klen, :])
                vs = vs_ref[:klen, :]
                for h in range(hg):
                    hs = slice(h * Q_BLOCK, (h + 1) * Q_BLOCK)
                    o_ref[rs, h * dh:(h + 1) * dh] = out[h] + gt[:, 3 * h + 1:3 * h + 2] * _softmax_pv(s3[hs] + bias, vs)


def _nsa_constants(s):
    nch = s // CMP_STRIDE
    nb = s // SEL_BLOCK
    n_cmp = (s - CMP_LEN) // CMP_STRIDE + 1
    ci = np.arange(nch)[None, :] * CMP_STRIDE
    bj = np.arange(nb)[:, None] * SEL_BLOCK
    overlap_t = (ci < bj + SEL_BLOCK) & (ci + CMP_LEN > bj) & (np.arange(nch)[None, :] < n_cmp)
    expand = np.arange(LANES)[:, None] == (np.arange(s)[None, :] // SEL_BLOCK)
    return jnp.asarray(overlap_t, BF16), jnp.asarray(expand, BF16)


def _nsa(cb3, kc, vc, gates3):
    bsz, s, _ = cb3.shape
    nq = s // Q_BLOCK
    nch = s // CMP_STRIDE
    nb = s // SEL_BLOCK
    dh = C_HEAD_DIM
    step_rows = Q_BLOCK * NSA_TILES
    assert nch == LANES and nb <= LANES and nb % 8 == 0 and s % step_rows == 0 and s >= WIN + Q_BLOCK
    overlap_t, expand = _nsa_constants(s)

    def kv(off):
        return pl.BlockSpec((None, s, dh), lambda b, g, n: (b, 0, off // dh + g))

    cmp_spec = pl.BlockSpec((None, None, nch, dh), lambda b, g, n: (b, g, 0, 0))
    return pl.pallas_call(
        functools.partial(_nsa_kernel, s_len=s),
        out_shape=jax.ShapeDtypeStruct((bsz, s, C_WIDTH), F32),
        grid=(bsz, C_KV_GROUPS, s // step_rows),
        in_specs=[
            pl.BlockSpec((None, step_rows, C_GROUP_WIDTH), lambda b, g, n: (b, n, BOFF_QC // C_GROUP_WIDTH + g)),
            cmp_spec, cmp_spec,
            kv(BOFF_KSLC), kv(BOFF_VSLC), kv(BOFF_KWIN), kv(BOFF_VWIN),
            pl.BlockSpec((None, step_rows, LANES), lambda b, g, n: (b, n, g)),
            pl.BlockSpec((nb, nch), lambda b, g, n: (0, 0)),
            pl.BlockSpec((LANES, s), lambda b, g, n: (0, 0)),
        ],
        out_specs=pl.BlockSpec((None, step_rows, C_GROUP_WIDTH), lambda b, g, n: (b, n, g)),
        compiler_params=_params("parallel", "parallel", "arbitrary"),
        name="nsa_attention",
    )(cb3, kc, vc, cb3, cb3, cb3, cb3, gates3, overlap_t, expand)


def _merge_kernel(*refs):
    ng = len(A_GROUPS)
    oa_refs, la_refs = refs[:ng], refs[ng:2 * ng]
    zb_ref, oc_ref, g0_ref, g1_ref, g2_ref, x_ref, pa_ref, pb_ref, pc_ref, wo_ref, gn_ref, o_ref, xn_ref = refs[2 * ng:]

    def lanes(ref):
        return jnp.concatenate([ref[half] for half in range(ref.shape[0])], axis=1)

    lse = [lanes(r) for r in la_refs]
    m = functools.reduce(jnp.maximum, lse)
    e = [jnp.exp2(l - m) for l in lse]
    oa = sum(w * lanes(r) for w, r in zip(e, oa_refs)) / sum(e)
    ya = _dot(oa.astype(BF16), pa_ref[...])
    yb = _dot(zb_ref[...].astype(BF16), pb_ref[...])
    yc = _dot(oc_ref[...].astype(BF16), pc_ref[...])
    mix = _sigmoid(g0_ref[...]) * ya + _sigmoid(g1_ref[...]) * yb + _sigmoid(g2_ref[...]) * yc
    y = x_ref[...] + _dot(mix.astype(BF16), wo_ref[...])
    o_ref[...] = y
    xn_ref[...] = _rms_scaled(y, gn_ref[...]).astype(BF16)


def _merge(oa, la, zb, oc, cf, x, proj_a, proj_b, proj_c, w_out, g_next, layer, *, tm=256):
    m, d = x.shape
    assert d == D_GATE

    def rows(w):
        return pl.BlockSpec((tm, w), lambda i: (i, 0))

    def gate(k):
        return pl.BlockSpec((tm, d), lambda i: (i, FOFF_GM // d + k))

    def weight(k):
        return pl.BlockSpec((None, k, d), lambda i: (layer, 0, 0), pipeline_mode=pl.Buffered(1))

    _, halves, s, _ = oa[0].shape
    per_seq = s // tm
    group = pl.BlockSpec((None, halves, tm, LANES), lambda i: (i // per_seq, 0, i % per_seq, 0))

    return pl.pallas_call(
        _merge_kernel,
        out_shape=(jax.ShapeDtypeStruct((m, d), F32), jax.ShapeDtypeStruct((m, d), BF16)),
        grid=(m // tm,),
        in_specs=[group] * (2 * len(A_GROUPS))
        + [rows(B_WIDTH), rows(C_WIDTH), gate(0), gate(1), gate(2), rows(d),
           weight(A_GROUP_WIDTH), weight(B_WIDTH), weight(C_WIDTH), weight(d),
           pl.BlockSpec((None, 1, d), lambda i: (layer, 0, 0))],
        out_specs=(rows(d), rows(d)),
        compiler_params=_params("parallel"),
        name="merge",
    )(*oa, *la, zb, oc, cf, cf, cf, x, proj_a, proj_b, proj_c, w_out, g_next)


_A3 = 3 * A_WIDTH
_XB_END = _A3 + B_WIDTH
_QC_END = _XB_END + C_WIDTH
_CMP_END = _QC_END + 2 * C_KV_WIDTH
_KV_END = _QC_END + 6 * C_KV_WIDTH
_GM_START = _KV_END + N_GATES
_REST_RANGES = ((_A3, _XB_END), (_QC_END, _CMP_END), (0, _A3), (_XB_END, _QC_END), (_CMP_END, _KV_END))


def _reorder_w_in(w_in):
    w_gm = w_in[..., _GM_START:].astype(BF16)
    w_rest = jnp.concatenate([w_in[..., lo:hi] for lo, hi in _REST_RANGES], axis=-1).astype(BF16)
    per_group = 3 * C_HEADS_PER_GROUP
    pad = jnp.zeros(w_in.shape[:-1] + (LANES - per_group,), w_in.dtype)
    w_gate = jnp.concatenate(
        [piece for g in range(C_KV_GROUPS)
         for piece in (w_in[..., _KV_END + g * per_group:_KV_END + (g + 1) * per_group], pad)], axis=-1).astype(BF16)
    col_scale = np.ones((1, N_F32_COLS + N_BF16_COLS), np.float32)
    col_scale[:, N_F32_COLS + BOFF_QA:N_F32_COLS + BOFF_QA + A_WIDTH] = A_HEAD_DIM ** -0.5 * LOG2E
    col_scale[:, N_F32_COLS + BOFF_QC:N_F32_COLS + BOFF_QC + C_WIDTH] = C_HEAD_DIM ** -0.5 * LOG2E
    return w_gm, w_rest, w_gate, jnp.asarray(col_scale)


def _mixing(x, xn, layer, w_gm, w_rest, w_gate, col_scale, pool_w, pool_scale, pe_k, w1_k, w2_k, pe_v, w1_v, w2_v,
            proj_a, proj_b, proj_c, w_out, g_next, bsz, s):
    m, d = x.shape
    cf, cb, gates = _in_proj(xn, w_gm, w_rest, w_gate, col_scale, layer)
    cf3 = cf.reshape(bsz, s, N_F32_COLS)
    cb3 = cb.reshape(bsz, s, N_BF16_COLS)
    gates3 = gates.reshape(bsz, s, C_KV_GROUPS * LANES)

    oa, la = zip(*[_dil_attn(cb3, gi) for gi in range(len(A_GROUPS))])
    zb, kc, vc = _pool_compress(cf3, pool_w, pool_scale, pe_k, w1_k, w2_k, pe_v, w1_v, w2_v, layer)
    zb = zb.reshape(m, B_WIDTH)
    oc = _nsa(cb3, kc, vc, gates3).reshape(m, C_WIDTH)
    return _merge(oa, la, zb, oc, cf, x, proj_a, proj_b, proj_c, w_out, g_next, layer)


def kernel(x, ffn1_norm, ffn1_wi, ffn1_wo, mix_norm, w_in, pool_w, pool_scale, cmp_pe_k, cmp_w1_k, cmp_w2_k,
           cmp_pe_v, cmp_w1_v, cmp_w2_v, proj_a, proj_b, proj_c, w_out, ffn2_norm, ffn2_wi, ffn2_wo, final_norm):
    bsz, s, d = x.shape
    depth = ffn1_wi.shape[0]
    for win, dil in A_GROUPS:
        assert win // dil == Q_BLOCK and s % (dil * Q_BLOCK) == 0
    assert CMP_LEN == 2 * CMP_STRIDE and all(w & (w - 1) == 0 for w in B_WINDOWS)

    bf = lambda w: w.astype(BF16)
    row3 = lambda g: g.reshape(g.shape[0], 1, g.shape[-1])
    w_gm, w_rest, w_gate, col_scale = _reorder_w_in(w_in)
    pool_w, cmp_w1_k, cmp_w2_k, cmp_w1_v, cmp_w2_v = bf(pool_w), bf(cmp_w1_k), bf(cmp_w2_k), bf(cmp_w1_v), bf(cmp_w2_v)
    proj_a, proj_b, proj_c, w_out = bf(proj_a), bf(proj_b), bf(proj_c), bf(w_out)
    ffn1_norm, mix_norm, ffn2_norm, pool_scale = row3(ffn1_norm), row3(mix_norm), row3(ffn2_norm), row3(pool_scale)

    final_norm = final_norm.reshape(1, 1, d)
    x = x.reshape(bsz * s, d)
    xn = _row_norm(x, ffn1_norm, 0)
    for layer in range(depth):
        h, wo = _ffn_up(xn, ffn1_wi, ffn1_wo, layer)
        x, xn = _ffn_down(h, wo, x, mix_norm, layer)
        x, xn = _mixing(x, xn, layer, w_gm, w_rest, w_gate, col_scale, pool_w, pool_scale, cmp_pe_k, cmp_w1_k, cmp_w2_k,
                        cmp_pe_v, cmp_w1_v, cmp_w2_v, proj_a, proj_b, proj_c, w_out, ffn2_norm, bsz, s)
        h, wo = _ffn_up(xn, ffn2_wi, ffn2_wo, layer)
        if layer + 1 < depth:
            x, xn = _ffn_down(h, wo, x, ffn1_norm, layer + 1)
        else:
            x = _ffn_down(h, wo, x, final_norm, 0, last=True)
    return x.reshape(bsz, s, d)
```
